```python
import math
import jax
import jax.numpy as jnp
from jax import lax
import numpy as np

D_MODEL = 4096
BATCH = 8
SEQ = 4096
DEPTH = 2

CTX_LEN = 256
GRID_W = 64
MIX_WIDTH = D_MODEL
BRANCH_W = MIX_WIDTH // 4
CHUNK = 64
EPS = 1e-6
F32 = jnp.float32

A_HEAD_DIM = 128
A_HEADS = BRANCH_W // A_HEAD_DIM
A_MIN_FORGET = 1e-6
B_HEADS = 4
B_KEY_W = BRANCH_W // 2
B_DK = B_KEY_W // B_HEADS
B_DV = BRANCH_W // B_HEADS
B_GATE_RANK = 16
B_GATE_NORM = 16.0
C_GROUP = 16
C_GROUPS = BRANCH_W // C_GROUP
C_STATE = 64
C_MAX_RE = -1e-4
DT_MIN = 1e-3
DT_MAX = 1e-1
D_HEADS = 4
D_KEY_W = BRANCH_W // 2
D_DK = D_KEY_W // D_HEADS
D_DV = BRANCH_W // D_HEADS
ROPE_BASE = 10000.0

IN_SPLITS = (
    BRANCH_W, BRANCH_W, BRANCH_W, BRANCH_W, BRANCH_W,
    B_KEY_W, B_KEY_W, BRANCH_W, B_GATE_RANK, B_GATE_RANK, BRANCH_W,
    BRANCH_W, BRANCH_W,
    D_KEY_W, D_KEY_W, BRANCH_W, BRANCH_W,
)
IN_WIDTH = sum(IN_SPLITS)

kernel_name = 'hybrid_hgrn2_gla_s5_retention_prefix_dit'


def rms_norm(x, g):
    xf = x.astype(F32)
    y = xf * lax.rsqrt(jnp.mean(xf * xf, axis=-1, keepdims=True) + EPS)
    return (y * g.astype(F32)).astype(x.dtype)


def head_layer_norm(x, g):
    xf = x.astype(F32)
    mu = jnp.mean(xf, axis=-1, keepdims=True)
    var = jnp.mean(jnp.square(xf - mu), axis=-1, keepdims=True)
    return ((xf - mu) * lax.rsqrt(var + EPS) * g.astype(F32)).astype(x.dtype)


def seg_flip(z, n_ctx):
    return jnp.concatenate([jnp.flip(z[:, :n_ctx], axis=1), jnp.flip(z[:, n_ctx:], axis=1)], axis=1)


def to_chunks(a):
    bsz, l = a.shape[:2]
    return jnp.moveaxis(a.reshape(bsz, l // CHUNK, CHUNK, *a.shape[2:]), 1, 0)


def from_chunks(a):
    a = jnp.moveaxis(a, 0, 1)
    return a.reshape(a.shape[0], -1, *a.shape[3:])


def split_columns(proj):
    offsets = np.cumsum(IN_SPLITS)[:-1].tolist()
    return jnp.split(proj, offsets, axis=-1)


def axial_rope(rows, n_ctx):
    quarter = D_DK // 4
    freqs = ROPE_BASE ** (-jnp.arange(quarter, dtype=F32) / quarter)
    t = jnp.arange(rows * GRID_W)
    r = (t // GRID_W).astype(F32)
    col = (t % GRID_W).astype(F32)
    ang = jnp.concatenate([r[:, None] * freqs, col[:, None] * freqs], axis=-1)
    ang = jnp.concatenate([jnp.zeros((n_ctx, D_DK // 2), F32), ang], axis=0)
    return jnp.cos(ang), jnp.sin(ang)


def apply_rope(x, cos, sin):
    half = x.shape[-1] // 2
    x1 = x[..., :half].astype(F32)
    x2 = x[..., half:].astype(F32)
    cs = cos[None, :, None, :]
    sn = sin[None, :, None, :]
    return jnp.concatenate([x1 * cs - x2 * sn, x1 * sn + x2 * cs], axis=-1).astype(x.dtype)


def gla_chunked(q, k, v, log_decay):
    bsz, _, h, dk = q.shape
    dv = v.shape[-1]
    lower = jnp.tril(jnp.ones((CHUNK, CHUNK), dtype=bool))[None, :, :, None, None]

    def step(state, inp):
        qi, ki, vi, gi = inp
        qi = qi.astype(F32)
        ki = ki.astype(F32)
        vi = vi.astype(F32)
        b = jnp.cumsum(gi, axis=1)
        diff = jnp.where(lower, b[:, :, None] - b[:, None, :], 0.0)
        rel = jnp.where(lower, jnp.exp(diff), 0.0)
        scores = jnp.einsum('bihd,bjhd,bijhd->bhij', qi, ki, rel)
        b_last = b[:, -1]
        o = (jnp.einsum('bhij,bjhe->bihe', scores, vi)
             + jnp.einsum('bihd,bhde->bihe', qi * jnp.exp(b), state))
        state = (state * jnp.exp(b_last)[..., None]
                 + jnp.einsum('bjhd,bjhe->bhde', ki * jnp.exp(b_last[:, None] - b), vi))
        return state, o

    s0 = jnp.zeros((bsz, h, dk, dv), F32)
    _, o = lax.scan(step, s0, (to_chunks(q), to_chunks(k), to_chunks(v), to_chunks(log_decay.astype(F32))))
    return from_chunks(o).astype(v.dtype)


def retention_chunked(q, k, v, log_gamma):
    bsz, _, h, dk = q.shape
    dv = v.shape[-1]
    pos = jnp.arange(CHUNK, dtype=F32)
    lg = log_gamma.astype(F32)
    rel = pos[:, None] - pos[None, :]
    dmat = jnp.where(rel[None] >= 0, jnp.exp(lg[:, None, None] * jnp.maximum(rel, 0.0)[None]), 0.0)
    xi = jnp.exp(lg[None, :] * (pos[:, None] + 1.0))[None, :, :, None]
    zeta = jnp.exp(lg[None, :] * (CHUNK - 1.0 - pos[:, None]))[None, :, :, None]
    chunk_decay = jnp.exp(lg * CHUNK)[None, :, None, None]

    def step(state, inp):
        qi, ki, vi = inp
        qi = qi.astype(F32)
        ki = ki.astype(F32)
        vi = vi.astype(F32)
        scores = jnp.einsum('bihd,bjhd->bhij', qi, ki) * dmat[None]
        o = (jnp.einsum('bhij,bjhe->bihe', scores, vi)
             + jnp.einsum('bihd,bhde->bihe', qi, state) * xi)
        state = state * chunk_decay + jnp.einsum('bjhd,bjhe->bhde', ki * zeta, vi)
        return state, o

    s0 = jnp.zeros((bsz, h, dk, dv), F32)
    _, o = lax.scan(step, s0, (to_chunks(q), to_chunks(k), to_chunks(v)))
    return from_chunks(o).astype(v.dtype)


def s5_scan(u, lam_re, lam_im, log_dt, b_re, b_im, c_re, c_im):
    lam = lax.complex(jnp.minimum(lam_re.astype(F32), C_MAX_RE), lam_im.astype(F32))
    dt = jnp.exp(log_dt.astype(F32))[:, None]
    lam_bar = jnp.exp(lam * dt)
    b_bar = ((lam_bar - 1.0) / lam)[..., None] * lax.complex(b_re.astype(F32), b_im.astype(F32))
    bu = jnp.einsum('gph,blgh->blgp', b_bar, u.astype(F32).astype(jnp.complex64))
    a = jnp.broadcast_to(lam_bar, bu.shape)

    def combine(e1, e2):
        a1, b1 = e1
        a2, b2 = e2
        return a1 * a2, a2 * b1 + b2

    _, states = lax.associative_scan(combine, (a, bu), axis=1)
    c_mat = lax.complex(c_re.astype(F32), c_im.astype(F32))
    return jnp.einsum('ghp,blgp->blgh', c_mat, states).real


def hgrn2_branch(q, f_fwd, f_bwd, i, gate, lower_bound, norm_g, n_ctx):
    bsz, l, _ = q.shape

    def heads(a):
        return a.reshape(bsz, l, A_HEADS, A_HEAD_DIM)

    qh, ih = heads(q), heads(i)

    def direction(qd, zd, idd, lb):
        lb = lb.reshape(A_HEADS, A_HEAD_DIM).astype(F32)
        z = heads(zd).astype(F32)
        f = lb + (1.0 - lb) * jax.nn.sigmoid(z)
        log_f = jnp.log(jnp.maximum(f, A_MIN_FORGET))
        key = (1.0 - lb) * jax.nn.sigmoid(-z)
        return gla_chunked(qd, key.astype(qd.dtype), idd, log_f)

    o_f = direction(qh, f_fwd, ih, lower_bound[0])
    o_b = seg_flip(direction(seg_flip(qh, n_ctx), seg_flip(f_bwd, n_ctx), seg_flip(ih, n_ctx), lower_bound[1]), n_ctx)
    o = rms_norm(o_f + o_b, norm_g).reshape(bsz, l, BRANCH_W)
    return o * jax.nn.silu(gate)


def gla_branch(q, k, v, lr_fwd, lr_bwd, gate, w_gk, b_gk, norm_g, n_ctx):
    bsz, l, _ = q.shape
    qh = q.reshape(bsz, l, B_HEADS, B_DK) * (B_DK ** -0.5)
    kh = k.reshape(bsz, l, B_HEADS, B_DK)
    vh = v.reshape(bsz, l, B_HEADS, B_DV)

    def log_decay(lr, w, b):
        g = jax.nn.log_sigmoid((lr @ w + b).astype(F32)) / B_GATE_NORM
        return g.reshape(bsz, l, B_HEADS, B_DK)

    o_f = gla_chunked(qh, kh, vh, log_decay(lr_fwd, w_gk[0], b_gk[0]))
    o_b = seg_flip(gla_chunked(seg_flip(qh, n_ctx), seg_flip(kh, n_ctx), seg_flip(vh, n_ctx),
                               log_decay(seg_flip(lr_bwd, n_ctx), w_gk[1], b_gk[1])), n_ctx)
    o = rms_norm(o_f + o_b, norm_g).reshape(bsz, l, BRANCH_W)
    return o * jax.nn.silu(gate)


def s5_branch(u, gate, lam_re, lam_im, log_dt, b_re, b_im, c_re, c_im, d, w_glu, b_glu, n_ctx):
    bsz, l, _ = u.shape
    ug = u.reshape(bsz, l, C_GROUPS, C_GROUP)
    y_f = s5_scan(ug, lam_re[0], lam_im[0], log_dt[0], b_re[0], b_im[0], c_re[0], c_im[0])
    y_b = seg_flip(s5_scan(seg_flip(ug, n_ctx), lam_re[1], lam_im[1], log_dt[1],
                           b_re[1], b_im[1], c_re[1], c_im[1]), n_ctx)
    y = (y_f + y_b + d.astype(F32) * ug.astype(F32)).reshape(bsz, l, BRANCH_W).astype(u.dtype)
    z = jax.nn.gelu(y)
    out = z * jax.nn.sigmoid(z @ w_glu + b_glu)
    return out * jax.nn.silu(gate)


def retention_branch(q, k, v, gate, log_gamma, norm_g, cos, sin, n_ctx):
    bsz, l, _ = q.shape
    qh = apply_rope(q.reshape(bsz, l, D_HEADS, D_DK), cos, sin) * (D_DK ** -0.5)
    kh = apply_rope(k.reshape(bsz, l, D_HEADS, D_DK), cos, sin)
    vh = v.reshape(bsz, l, D_HEADS, D_DV)
    o_f = retention_chunked(qh, kh, vh, log_gamma[0])
    o_b = seg_flip(retention_chunked(seg_flip(qh, n_ctx), seg_flip(kh, n_ctx), seg_flip(vh, n_ctx), log_gamma[1]), n_ctx)
    o = head_layer_norm(o_f + o_b, norm_g).reshape(bsz, l, BRANCH_W)
    return o * jax.nn.silu(gate)


def _fwd_setup_inputs(seed: int = 0) -> dict:
    key = jax.random.key(seed)
    ks = jax.random.split(key, 27)

    def nrm(k, shape, scale):
        return jax.random.normal(k, shape, F32) * scale

    n_idx = jnp.arange(C_STATE, dtype=F32)
    gammas = 1.0 - 2.0 ** (-5.0 - jnp.arange(D_HEADS, dtype=F32))
    gamma_logit = jnp.log(gammas) - jnp.log1p(-gammas)
    return {
        'x': nrm(ks[0], (BATCH, SEQ, D_MODEL), 1.0),
        'c': nrm(ks[1], (BATCH, D_MODEL), 1.0),
        'ctx': nrm(ks[2], (BATCH, CTX_LEN, D_MODEL), 1.0),
        'c_ctx': nrm(ks[3], (D_MODEL,), 1.0),
        'norm_g': 1.0 + nrm(ks[4], (DEPTH, D_MODEL), 0.02),
        'w_ada': nrm(ks[5], (DEPTH, D_MODEL, 3 * D_MODEL), 0.5 * D_MODEL ** -0.5),
        'b_ada': nrm(ks[6], (DEPTH, 3 * D_MODEL), 0.02),
        'w_in': nrm(ks[7], (DEPTH, D_MODEL, IN_WIDTH), D_MODEL ** -0.5),
        'hgrn_lb_logits': nrm(ks[8], (DEPTH, 2, BRANCH_W), 0.1),
        'hgrn_norm_g': 1.0 + nrm(ks[9], (DEPTH, A_HEAD_DIM), 0.02),
        'gla_w_gk': nrm(ks[10], (DEPTH, 2, B_GATE_RANK, B_KEY_W), B_GATE_RANK ** -0.5),
        'gla_b_gk': nrm(ks[11], (DEPTH, 2, B_KEY_W), 0.1),
        'gla_norm_g': 1.0 + nrm(ks[12], (DEPTH, B_DV), 0.02),
        's5_lam_re': -0.5 + nrm(ks[13], (DEPTH, 2, C_GROUPS, C_STATE), 0.01),
        's5_lam_im': math.pi * n_idx + nrm(ks[14], (DEPTH, 2, C_GROUPS, C_STATE), 0.01),
        's5_log_dt': jax.random.uniform(ks[15], (DEPTH, 2, C_GROUPS), F32, math.log(DT_MIN), math.log(DT_MAX)),
        's5_b_re': nrm(ks[16], (DEPTH, 2, C_GROUPS, C_STATE, C_GROUP), (2 * C_GROUP) ** -0.5),
        's5_b_im': nrm(ks[17], (DEPTH, 2, C_GROUPS, C_STATE, C_GROUP), (2 * C_GROUP) ** -0.5),
        's5_c_re': nrm(ks[18], (DEPTH, 2, C_GROUPS, C_GROUP, C_STATE), (2 * C_STATE) ** -0.5),
        's5_c_im': nrm(ks[19], (DEPTH, 2, C_GROUPS, C_GROUP, C_STATE), (2 * C_STATE) ** -0.5),
        's5_d': nrm(ks[20], (DEPTH, C_GROUPS, C_GROUP), 0.5),
        's5_w_glu': nrm(ks[21], (DEPTH, BRANCH_W, BRANCH_W), BRANCH_W ** -0.5),
        's5_b_glu': nrm(ks[22], (DEPTH, BRANCH_W), 0.02),
        'ret_decay_logit': gamma_logit + nrm(ks[23], (DEPTH, 2, D_HEADS), 0.01),
        'ret_norm_g': 1.0 + nrm(ks[24], (DEPTH, D_DV), 0.02),
        'w_out': nrm(ks[25], (DEPTH, MIX_WIDTH, D_MODEL), MIX_WIDTH ** -0.5),
        'final_norm_g': 1.0 + nrm(ks[26], (D_MODEL,), 0.02),
    }


def _fwd_reference(x, c, ctx, c_ctx, norm_g, w_ada, b_ada, w_in, hgrn_lb_logits, hgrn_norm_g,
              gla_w_gk, gla_b_gk, gla_norm_g, s5_lam_re, s5_lam_im, s5_log_dt, s5_b_re, s5_b_im,
              s5_c_re, s5_c_im, s5_d, s5_w_glu, s5_b_glu, ret_decay_logit, ret_norm_g, w_out,
              final_norm_g):
    n_ctx = ctx.shape[1]
    rows = x.shape[1] // GRID_W
    cos, sin = axial_rope(rows, n_ctx)
    lb_p = jax.nn.softmax(hgrn_lb_logits.astype(F32), axis=0)
    lower_bounds = jnp.cumsum(lb_p, axis=0) - lb_p[0:1]

    h_ctx, h_lat = ctx, x
    for layer in range(DEPTH):
        last = layer == DEPTH - 1
        mod_lat = jax.nn.silu(c) @ w_ada[layer] + b_ada[layer]
        mod_ctx = jax.nn.silu(c_ctx) @ w_ada[layer] + b_ada[layer]
        sh_l, sc_l, gt_l = jnp.split(mod_lat, 3, axis=-1)
        sh_c, sc_c, gt_c = jnp.split(mod_ctx, 3, axis=-1)
        hn = jnp.concatenate([
            rms_norm(h_ctx, norm_g[layer]) * (1.0 + sc_c) + sh_c,
            rms_norm(h_lat, norm_g[layer]) * (1.0 + sc_l[:, None]) + sh_l[:, None],
        ], axis=1)
        (a_q, a_ff, a_fb, a_i, a_g,
         b_q, b_k, b_v, b_lf, b_lb, b_g,
         c_u, c_g,
         d_q, d_k, d_v, d_g) = split_columns(hn @ w_in[layer])
        o_a = hgrn2_branch(a_q, a_ff, a_fb, a_i, a_g, lower_bounds[layer], hgrn_norm_g[layer], n_ctx)
        o_b = gla_branch(b_q, b_k, b_v, b_lf, b_lb, b_g, gla_w_gk[layer], gla_b_gk[layer], gla_norm_g[layer], n_ctx)
        o_c = s5_branch(c_u, c_g, s5_lam_re[layer], s5_lam_im[layer], s5_log_dt[layer], s5_b_re[layer],
                        s5_b_im[layer], s5_c_re[layer], s5_c_im[layer], s5_d[layer], s5_w_glu[layer],
                        s5_b_glu[layer], n_ctx)
        log_gamma = jax.nn.log_sigmoid(ret_decay_logit[layer].astype(F32))
        o_d = retention_branch(d_q, d_k, d_v, d_g, log_gamma, ret_norm_g[layer], cos, sin, n_ctx)
        o = jnp.concatenate([o_a, o_b, o_c, o_d], axis=-1)
        if last:
            h_lat = h_lat + gt_l[:, None] * (o[:, n_ctx:] @ w_out[layer])
        else:
            y = o @ w_out[layer]
            h_ctx = h_ctx + gt_c * y[:, :n_ctx]
            h_lat = h_lat + gt_l[:, None] * y[:, n_ctx:]
    return rms_norm(h_lat, final_norm_g)


import jax as _jax
import jax.numpy as _jnp

TWIN_FORMAT = 'train_step'
FWD_PARAMS = ['x', 'c', 'ctx', 'c_ctx', 'norm_g', 'w_ada', 'b_ada', 'w_in', 'hgrn_lb_logits', 'hgrn_norm_g', 'gla_w_gk', 'gla_b_gk', 'gla_norm_g', 's5_lam_re', 's5_lam_im', 's5_log_dt', 's5_b_re', 's5_b_im', 's5_c_re', 's5_c_im', 's5_d', 's5_w_glu', 's5_b_glu', 'ret_decay_logit', 'ret_norm_g', 'w_out', 'final_norm_g']
TWIN_WEIGHTS = ['c_ctx', 'norm_g', 'w_ada', 'b_ada', 'w_in', 'hgrn_lb_logits', 'hgrn_norm_g', 'gla_w_gk', 'gla_b_gk', 'gla_norm_g', 's5_lam_re', 's5_lam_im', 's5_log_dt', 's5_b_re', 's5_b_im', 's5_c_re', 's5_c_im', 's5_d', 's5_w_glu', 's5_b_glu', 'ret_decay_logit', 'ret_norm_g', 'w_out', 'final_norm_g']
TWIN_DIFF_INPUT = 'x'
TWIN_INPUTS = ['x', 'c', 'ctx', 'c_ctx', 'norm_g', 'w_ada', 'b_ada', 'w_in', 'hgrn_lb_logits', 'hgrn_norm_g', 'gla_w_gk', 'gla_b_gk', 'gla_norm_g', 's5_lam_re', 's5_lam_im', 's5_log_dt', 's5_b_re', 's5_b_im', 's5_c_re', 's5_c_im', 's5_d', 's5_w_glu', 's5_b_glu', 'ret_decay_logit', 'ret_norm_g', 'w_out', 'final_norm_g', 'loss_target', 'm_c_ctx', 'm_norm_g', 'm_w_ada', 'm_b_ada', 'm_w_in', 'm_hgrn_lb_logits', 'm_hgrn_norm_g', 'm_gla_w_gk', 'm_gla_b_gk', 'm_gla_norm_g', 'm_s5_lam_re', 'm_s5_lam_im', 'm_s5_log_dt', 'm_s5_b_re', 'm_s5_b_im', 'm_s5_c_re', 'm_s5_c_im', 'm_s5_d', 'm_s5_w_glu', 'm_s5_b_glu', 'm_ret_decay_logit', 'm_ret_norm_g', 'm_w_out', 'm_final_norm_g', 'v_c_ctx', 'v_norm_g', 'v_w_ada', 'v_b_ada', 'v_w_in', 'v_hgrn_lb_logits', 'v_hgrn_norm_g', 'v_gla_w_gk', 'v_gla_b_gk', 'v_gla_norm_g', 'v_s5_lam_re', 'v_s5_lam_im', 'v_s5_log_dt', 'v_s5_b_re', 'v_s5_b_im', 'v_s5_c_re', 'v_s5_c_im', 'v_s5_d', 'v_s5_w_glu', 'v_s5_b_glu', 'v_ret_decay_logit', 'v_ret_norm_g', 'v_w_out', 'v_final_norm_g']
TWIN_OUTPUTS = ['loss', 'grad_x', 'grad_c_ctx', 'grad_norm_g', 'grad_w_ada', 'grad_b_ada', 'grad_w_in', 'grad_hgrn_lb_logits', 'grad_hgrn_norm_g', 'grad_gla_w_gk', 'grad_gla_b_gk', 'grad_gla_norm_g', 'grad_s5_lam_re', 'grad_s5_lam_im', 'grad_s5_log_dt', 'grad_s5_b_re', 'grad_s5_b_im', 'grad_s5_c_re', 'grad_s5_c_im', 'grad_s5_d', 'grad_s5_w_glu', 'grad_s5_b_glu', 'grad_ret_decay_logit', 'grad_ret_norm_g', 'grad_w_out', 'grad_final_norm_g', 'delta_c_ctx', 'delta_norm_g', 'delta_w_ada', 'delta_b_ada', 'delta_w_in', 'delta_hgrn_lb_logits', 'delta_hgrn_norm_g', 'delta_gla_w_gk', 'delta_gla_b_gk', 'delta_gla_norm_g', 'delta_s5_lam_re', 'delta_s5_lam_im', 'delta_s5_log_dt', 'delta_s5_b_re', 'delta_s5_b_im', 'delta_s5_c_re', 'delta_s5_c_im', 'delta_s5_d', 'delta_s5_w_glu', 'delta_s5_b_glu', 'delta_ret_decay_logit', 'delta_ret_norm_g', 'delta_w_out', 'delta_final_norm_g', 'new_m_c_ctx', 'new_m_norm_g', 'new_m_w_ada', 'new_m_b_ada', 'new_m_w_in', 'new_m_hgrn_lb_logits', 'new_m_hgrn_norm_g', 'new_m_gla_w_gk', 'new_m_gla_b_gk', 'new_m_gla_norm_g', 'new_m_s5_lam_re', 'new_m_s5_lam_im', 'new_m_s5_log_dt', 'new_m_s5_b_re', 'new_m_s5_b_im', 'new_m_s5_c_re', 'new_m_s5_c_im', 'new_m_s5_d', 'new_m_s5_w_glu', 'new_m_s5_b_glu', 'new_m_ret_decay_logit', 'new_m_ret_norm_g', 'new_m_w_out', 'new_m_final_norm_g', 'new_v_c_ctx', 'new_v_norm_g', 'new_v_w_ada', 'new_v_b_ada', 'new_v_w_in', 'new_v_hgrn_lb_logits', 'new_v_hgrn_norm_g', 'new_v_gla_w_gk', 'new_v_gla_b_gk', 'new_v_gla_norm_g', 'new_v_s5_lam_re', 'new_v_s5_lam_im', 'new_v_s5_log_dt', 'new_v_s5_b_re', 'new_v_s5_b_im', 'new_v_s5_c_re', 'new_v_s5_c_im', 'new_v_s5_d', 'new_v_s5_w_glu', 'new_v_s5_b_glu', 'new_v_ret_decay_logit', 'new_v_ret_norm_g', 'new_v_w_out', 'new_v_final_norm_g']
TWIN_LEAF_KINDS = {'loss': 'loss', 'grad_x': 'grad_x', 'grad_c_ctx': 'grad_w', 'grad_norm_g': 'grad_w', 'grad_w_ada': 'grad_w', 'grad_b_ada': 'grad_w', 'grad_w_in': 'grad_w', 'grad_hgrn_lb_logits': 'grad_w', 'grad_hgrn_norm_g': 'grad_w', 'grad_gla_w_gk': 'grad_w', 'grad_gla_b_gk': 'grad_w', 'grad_gla_norm_g': 'grad_w', 'grad_s5_lam_re': 'grad_w', 'grad_s5_lam_im': 'grad_w', 'grad_s5_log_dt': 'grad_w', 'grad_s5_b_re': 'grad_w', 'grad_s5_b_im': 'grad_w', 'grad_s5_c_re': 'grad_w', 'grad_s5_c_im': 'grad_w', 'grad_s5_d': 'grad_w', 'grad_s5_w_glu': 'grad_w', 'grad_s5_b_glu': 'grad_w', 'grad_ret_decay_logit': 'grad_w', 'grad_ret_norm_g': 'grad_w', 'grad_w_out': 'grad_w', 'grad_final_norm_g': 'grad_w', 'delta_c_ctx': 'delta_w', 'delta_norm_g': 'delta_w', 'delta_w_ada': 'delta_w', 'delta_b_ada': 'delta_w', 'delta_w_in': 'delta_w', 'delta_hgrn_lb_logits': 'delta_w', 'delta_hgrn_norm_g': 'delta_w', 'delta_gla_w_gk': 'delta_w', 'delta_gla_b_gk': 'delta_w', 'delta_gla_norm_g': 'delta_w', 'delta_s5_lam_re': 'delta_w', 'delta_s5_lam_im': 'delta_w', 'delta_s5_log_dt': 'delta_w', 'delta_s5_b_re': 'delta_w', 'delta_s5_b_im': 'delta_w', 'delta_s5_c_re': 'delta_w', 'delta_s5_c_im': 'delta_w', 'delta_s5_d': 'delta_w', 'delta_s5_w_glu': 'delta_w', 'delta_s5_b_glu': 'delta_w', 'delta_ret_decay_logit': 'delta_w', 'delta_ret_norm_g': 'delta_w', 'delta_w_out': 'delta_w', 'delta_final_norm_g': 'delta_w', 'new_m_c_ctx': 'new_m', 'new_m_norm_g': 'new_m', 'new_m_w_ada': 'new_m', 'new_m_b_ada': 'new_m', 'new_m_w_in': 'new_m', 'new_m_hgrn_lb_logits': 'new_m', 'new_m_hgrn_norm_g': 'new_m', 'new_m_gla_w_gk': 'new_m', 'new_m_gla_b_gk': 'new_m', 'new_m_gla_norm_g': 'new_m', 'new_m_s5_lam_re': 'new_m', 'new_m_s5_lam_im': 'new_m', 'new_m_s5_log_dt': 'new_m', 'new_m_s5_b_re': 'new_m', 'new_m_s5_b_im': 'new_m', 'new_m_s5_c_re': 'new_m', 'new_m_s5_c_im': 'new_m', 'new_m_s5_d': 'new_m', 'new_m_s5_w_glu': 'new_m', 'new_m_s5_b_glu': 'new_m', 'new_m_ret_decay_logit': 'new_m', 'new_m_ret_norm_g': 'new_m', 'new_m_w_out': 'new_m', 'new_m_final_norm_g': 'new_m', 'new_v_c_ctx': 'new_v', 'new_v_norm_g': 'new_v', 'new_v_w_ada': 'new_v', 'new_v_b_ada': 'new_v', 'new_v_w_in': 'new_v', 'new_v_hgrn_lb_logits': 'new_v', 'new_v_hgrn_norm_g': 'new_v', 'new_v_gla_w_gk': 'new_v', 'new_v_gla_b_gk': 'new_v', 'new_v_gla_norm_g': 'new_v', 'new_v_s5_lam_re': 'new_v', 'new_v_s5_lam_im': 'new_v', 'new_v_s5_log_dt': 'new_v', 'new_v_s5_b_re': 'new_v', 'new_v_s5_b_im': 'new_v', 'new_v_s5_c_re': 'new_v', 'new_v_s5_c_im': 'new_v', 'new_v_s5_d': 'new_v', 'new_v_s5_w_glu': 'new_v', 'new_v_s5_b_glu': 'new_v', 'new_v_ret_decay_logit': 'new_v', 'new_v_ret_norm_g': 'new_v', 'new_v_w_out': 'new_v', 'new_v_final_norm_g': 'new_v'}


def _forward(args):
    return _fwd_reference(*[args[k] for k in FWD_PARAMS])


def _output_shape():
    out = _jax.eval_shape(lambda: _forward(_fwd_setup_inputs(0)))
    return out.shape, out.dtype

N_MICROBATCH = 1
ADAM_LR = 0.001
ADAM_B1 = 0.9
ADAM_B2 = 0.999
ADAM_EPS = 1e-08
ADAM_WD = 0.01
ADAM_STEP = 10
PER_EXAMPLE_BATCH_AXIS = {'x': 0, 'c': 0, 'ctx': 0, 'loss_target': 0}
SHARED_INPUTS = []
_WEIGHT_DTYPES = {'c_ctx': _jnp.float32, 'norm_g': _jnp.float32, 'w_ada': _jnp.float32, 'b_ada': _jnp.float32, 'w_in': _jnp.float32, 'hgrn_lb_logits': _jnp.float32, 'hgrn_norm_g': _jnp.float32, 'gla_w_gk': _jnp.float32, 'gla_b_gk': _jnp.float32, 'gla_norm_g': _jnp.float32, 's5_lam_re': _jnp.float32, 's5_lam_im': _jnp.float32, 's5_log_dt': _jnp.float32, 's5_b_re': _jnp.float32, 's5_b_im': _jnp.float32, 's5_c_re': _jnp.float32, 's5_c_im': _jnp.float32, 's5_d': _jnp.float32, 's5_w_glu': _jnp.float32, 's5_b_glu': _jnp.float32, 'ret_decay_logit': _jnp.float32, 'ret_norm_g': _jnp.float32, 'w_out': _jnp.float32, 'final_norm_g': _jnp.float32}
MOMENT_SCALE = {'c_ctx': 3.759891e-03, 'norm_g': 1.496026e-02, 'w_ada': 1.341050e-02, 'b_ada': 2.261435e-02, 'w_in': 9.282820e-03, 'hgrn_lb_logits': 3.268209e-03, 'hgrn_norm_g': 2.447654e-02, 'gla_w_gk': 1.439495e-03, 'gla_b_gk': 4.127069e-03, 'gla_norm_g': 1.718854e-02, 's5_lam_re': 1.910539e-04, 's5_lam_im': 1.802851e-04, 's5_log_dt': 8.162953e-02, 's5_b_re': 1.147766e-04, 's5_b_im': 1.178394e-04, 's5_c_re': 2.301802e-04, 's5_c_im': 2.333069e-04, 's5_d': 3.240174e-03, 's5_w_glu': 2.555635e-04, 's5_b_glu': 7.242249e-04, 'ret_decay_logit': 5.198089e-02, 'ret_norm_g': 1.717802e-02, 'w_out': 7.587511e-03, 'final_norm_g': 7.989877e+00}


def _to_microbatches(a, axis):
    t = _jnp.moveaxis(a, axis, 0)
    t = t.reshape((N_MICROBATCH, t.shape[0] // N_MICROBATCH) + t.shape[1:])
    return _jnp.moveaxis(t, 1, axis + 1)


def setup_inputs(seed: int = 0) -> dict:
    inp = _fwd_setup_inputs(seed)
    key = _jax.random.fold_in(_jax.random.key(seed), 7919)
    shape, _ = _output_shape()
    out = dict(inp)
    out["loss_target"] = _jax.random.normal(_jax.random.fold_in(key, 0), shape, _jnp.float32)
    for i, name in enumerate(TWIN_WEIGHTS):
        w = inp[name].astype(_jnp.float32)
        if MOMENT_SCALE is None:
            s = _jnp.sqrt(_jnp.mean(_jnp.square(w)) + 1e-30)
        else:
            s = MOMENT_SCALE[name]
        km, kv = _jax.random.split(_jax.random.fold_in(key, i + 1))
        out[name] = w
        out["m_" + name] = s * _jax.random.normal(km, w.shape, _jnp.float32)
        out["v_" + name] = (s * s) * _jax.random.uniform(kv, w.shape, _jnp.float32, 0.5, 1.5)
    if N_MICROBATCH > 1:
        for name, axis in PER_EXAMPLE_BATCH_AXIS.items():
            out[name] = _to_microbatches(out[name], axis)
    return {'x': out['x'], 'c': out['c'], 'ctx': out['ctx'], 'c_ctx': out['c_ctx'], 'norm_g': out['norm_g'], 'w_ada': out['w_ada'], 'b_ada': out['b_ada'], 'w_in': out['w_in'], 'hgrn_lb_logits': out['hgrn_lb_logits'], 'hgrn_norm_g': out['hgrn_norm_g'], 'gla_w_gk': out['gla_w_gk'], 'gla_b_gk': out['gla_b_gk'], 'gla_norm_g': out['gla_norm_g'], 's5_lam_re': out['s5_lam_re'], 's5_lam_im': out['s5_lam_im'], 's5_log_dt': out['s5_log_dt'], 's5_b_re': out['s5_b_re'], 's5_b_im': out['s5_b_im'], 's5_c_re': out['s5_c_re'], 's5_c_im': out['s5_c_im'], 's5_d': out['s5_d'], 's5_w_glu': out['s5_w_glu'], 's5_b_glu': out['s5_b_glu'], 'ret_decay_logit': out['ret_decay_logit'], 'ret_norm_g': out['ret_norm_g'], 'w_out': out['w_out'], 'final_norm_g': out['final_norm_g'], 'loss_target': out['loss_target'], 'm_c_ctx': out['m_c_ctx'], 'm_norm_g': out['m_norm_g'], 'm_w_ada': out['m_w_ada'], 'm_b_ada': out['m_b_ada'], 'm_w_in': out['m_w_in'], 'm_hgrn_lb_logits': out['m_hgrn_lb_logits'], 'm_hgrn_norm_g': out['m_hgrn_norm_g'], 'm_gla_w_gk': out['m_gla_w_gk'], 'm_gla_b_gk': out['m_gla_b_gk'], 'm_gla_norm_g': out['m_gla_norm_g'], 'm_s5_lam_re': out['m_s5_lam_re'], 'm_s5_lam_im': out['m_s5_lam_im'], 'm_s5_log_dt': out['m_s5_log_dt'], 'm_s5_b_re': out['m_s5_b_re'], 'm_s5_b_im': out['m_s5_b_im'], 'm_s5_c_re': out['m_s5_c_re'], 'm_s5_c_im': out['m_s5_c_im'], 'm_s5_d': out['m_s5_d'], 'm_s5_w_glu': out['m_s5_w_glu'], 'm_s5_b_glu': out['m_s5_b_glu'], 'm_ret_decay_logit': out['m_ret_decay_logit'], 'm_ret_norm_g': out['m_ret_norm_g'], 'm_w_out': out['m_w_out'], 'm_final_norm_g': out['m_final_norm_g'], 'v_c_ctx': out['v_c_ctx'], 'v_norm_g': out['v_norm_g'], 'v_w_ada': out['v_w_ada'], 'v_b_ada': out['v_b_ada'], 'v_w_in': out['v_w_in'], 'v_hgrn_lb_logits': out['v_hgrn_lb_logits'], 'v_hgrn_norm_g': out['v_hgrn_norm_g'], 'v_gla_w_gk': out['v_gla_w_gk'], 'v_gla_b_gk': out['v_gla_b_gk'], 'v_gla_norm_g': out['v_gla_norm_g'], 'v_s5_lam_re': out['v_s5_lam_re'], 'v_s5_lam_im': out['v_s5_lam_im'], 'v_s5_log_dt': out['v_s5_log_dt'], 'v_s5_b_re': out['v_s5_b_re'], 'v_s5_b_im': out['v_s5_b_im'], 'v_s5_c_re': out['v_s5_c_re'], 'v_s5_c_im': out['v_s5_c_im'], 'v_s5_d': out['v_s5_d'], 'v_s5_w_glu': out['v_s5_w_glu'], 'v_s5_b_glu': out['v_s5_b_glu'], 'v_ret_decay_logit': out['v_ret_decay_logit'], 'v_ret_norm_g': out['v_ret_norm_g'], 'v_w_out': out['v_w_out'], 'v_final_norm_g': out['v_final_norm_g']}


def _loss(weights, diff, rest, loss_target):
    with _jax.named_scope("forward"):
        args = {**rest, TWIN_DIFF_INPUT: diff, **{k: w.astype(_WEIGHT_DTYPES[k]) for k, w in weights.items()}}
        y = _forward(args)
    with _jax.named_scope("loss_head"):
        err = _jnp.square(y.astype(_jnp.float32) - loss_target)
        return 0.5 * _jnp.sum(_jnp.mean(err, axis=-1)) if err.ndim else 0.5 * err


def _adamw(w, g, m, v):
    m = ADAM_B1 * m + (1.0 - ADAM_B1) * g
    v = ADAM_B2 * v + (1.0 - ADAM_B2) * _jnp.square(g)
    m_hat = m / (1.0 - ADAM_B1 ** ADAM_STEP)
    v_hat = v / (1.0 - ADAM_B2 ** ADAM_STEP)
    delta = -ADAM_LR * (m_hat / (_jnp.sqrt(v_hat) + ADAM_EPS) + ADAM_WD * w)
    return delta, m, v


def reference(x, c, ctx, c_ctx, norm_g, w_ada, b_ada, w_in, hgrn_lb_logits, hgrn_norm_g, gla_w_gk, gla_b_gk, gla_norm_g, s5_lam_re, s5_lam_im, s5_log_dt, s5_b_re, s5_b_im, s5_c_re, s5_c_im, s5_d, s5_w_glu, s5_b_glu, ret_decay_logit, ret_norm_g, w_out, final_norm_g, loss_target, m_c_ctx, m_norm_g, m_w_ada, m_b_ada, m_w_in, m_hgrn_lb_logits, m_hgrn_norm_g, m_gla_w_gk, m_gla_b_gk, m_gla_norm_g, m_s5_lam_re, m_s5_lam_im, m_s5_log_dt, m_s5_b_re, m_s5_b_im, m_s5_c_re, m_s5_c_im, m_s5_d, m_s5_w_glu, m_s5_b_glu, m_ret_decay_logit, m_ret_norm_g, m_w_out, m_final_norm_g, v_c_ctx, v_norm_g, v_w_ada, v_b_ada, v_w_in, v_hgrn_lb_logits, v_hgrn_norm_g, v_gla_w_gk, v_gla_b_gk, v_gla_norm_g, v_s5_lam_re, v_s5_lam_im, v_s5_log_dt, v_s5_b_re, v_s5_b_im, v_s5_c_re, v_s5_c_im, v_s5_d, v_s5_w_glu, v_s5_b_glu, v_ret_decay_logit, v_ret_norm_g, v_w_out, v_final_norm_g):
    given = dict(x=x, c=c, ctx=ctx, c_ctx=c_ctx, norm_g=norm_g, w_ada=w_ada, b_ada=b_ada, w_in=w_in, hgrn_lb_logits=hgrn_lb_logits, hgrn_norm_g=hgrn_norm_g, gla_w_gk=gla_w_gk, gla_b_gk=gla_b_gk, gla_norm_g=gla_norm_g, s5_lam_re=s5_lam_re, s5_lam_im=s5_lam_im, s5_log_dt=s5_log_dt, s5_b_re=s5_b_re, s5_b_im=s5_b_im, s5_c_re=s5_c_re, s5_c_im=s5_c_im, s5_d=s5_d, s5_w_glu=s5_w_glu, s5_b_glu=s5_b_glu, ret_decay_logit=ret_decay_logit, ret_norm_g=ret_norm_g, w_out=w_out, final_norm_g=final_norm_g, loss_target=loss_target, m_c_ctx=m_c_ctx, m_norm_g=m_norm_g, m_w_ada=m_w_ada, m_b_ada=m_b_ada, m_w_in=m_w_in, m_hgrn_lb_logits=m_hgrn_lb_logits, m_hgrn_norm_g=m_hgrn_norm_g, m_gla_w_gk=m_gla_w_gk, m_gla_b_gk=m_gla_b_gk, m_gla_norm_g=m_gla_norm_g, m_s5_lam_re=m_s5_lam_re, m_s5_lam_im=m_s5_lam_im, m_s5_log_dt=m_s5_log_dt, m_s5_b_re=m_s5_b_re, m_s5_b_im=m_s5_b_im, m_s5_c_re=m_s5_c_re, m_s5_c_im=m_s5_c_im, m_s5_d=m_s5_d, m_s5_w_glu=m_s5_w_glu, m_s5_b_glu=m_s5_b_glu, m_ret_decay_logit=m_ret_decay_logit, m_ret_norm_g=m_ret_norm_g, m_w_out=m_w_out, m_final_norm_g=m_final_norm_g, v_c_ctx=v_c_ctx, v_norm_g=v_norm_g, v_w_ada=v_w_ada, v_b_ada=v_b_ada, v_w_in=v_w_in, v_hgrn_lb_logits=v_hgrn_lb_logits, v_hgrn_norm_g=v_hgrn_norm_g, v_gla_w_gk=v_gla_w_gk, v_gla_b_gk=v_gla_b_gk, v_gla_norm_g=v_gla_norm_g, v_s5_lam_re=v_s5_lam_re, v_s5_lam_im=v_s5_lam_im, v_s5_log_dt=v_s5_log_dt, v_s5_b_re=v_s5_b_re, v_s5_b_im=v_s5_b_im, v_s5_c_re=v_s5_c_re, v_s5_c_im=v_s5_c_im, v_s5_d=v_s5_d, v_s5_w_glu=v_s5_w_glu, v_s5_b_glu=v_s5_b_glu, v_ret_decay_logit=v_ret_decay_logit, v_ret_norm_g=v_ret_norm_g, v_w_out=v_w_out, v_final_norm_g=v_final_norm_g)
    weights = {n: given[n] for n in TWIN_WEIGHTS}
    shared = {n: given[n] for n in SHARED_INPUTS}
    per_example = {n: given[n] for n in ['x', 'c', 'ctx']}
    grad_fn = _jax.value_and_grad(_loss, argnums=(0, 1))

    def one_microbatch(ex, loss_target):
        ex = dict(ex)
        diff = ex.pop(TWIN_DIFF_INPUT)
        return grad_fn(weights, diff, {**shared, **ex}, loss_target)

    if N_MICROBATCH == 1:
        loss, (grad_w, grad_x) = one_microbatch(per_example, given["loss_target"])
    else:
        def body(carry, xs):
            loss_sum, grad_sum = carry
            l_k, (gw_k, gx_k) = one_microbatch(xs[0], xs[1])
            with _jax.named_scope("update"):
                return (loss_sum + l_k, _jax.tree.map(_jnp.add, grad_sum, gw_k)), gx_k

        init = (_jnp.zeros((), _jnp.float32), _jax.tree.map(_jnp.zeros_like, weights))
        (loss, grad_w), grad_x = _jax.lax.scan(body, init, (per_example, given["loss_target"]))
    with _jax.named_scope("update"):
        delta_w, new_m, new_v = {}, {}, {}
        for n in TWIN_WEIGHTS:
            delta_w[n], new_m[n], new_v[n] = _adamw(weights[n], grad_w[n], given["m_" + n], given["v_" + n])
    return (loss, grad_x, *[grad_w[n] for n in TWIN_WEIGHTS], *[delta_w[n] for n in TWIN_WEIGHTS],
            *[new_m[n] for n in TWIN_WEIGHTS], *[new_v[n] for n in TWIN_WEIGHTS])
```

```python
import functools
import math

import jax
import jax.numpy as jnp
from jax import lax
from jax.experimental import pallas as pl
from jax.experimental.pallas import tpu as pltpu

F32 = jnp.float32
BF16 = jnp.bfloat16

EPS = 1e-6
BRANCH_W = 1024
CHUNK = 64
HEAD_DK = 128
A_MIN_FORGET = 1e-6
B_GATE_NORM = 16.0
B_GATE_RANK = 16
C_GROUP = 16
C_GROUPS = 64
C_STATE = 64
C_MAX_RE = -1e-4
GRID_W = 64
ROPE_BASE = 10000.0
N_DEV = 8

ADAM_LR = 0.001
ADAM_B1 = 0.9
ADAM_B2 = 0.999
ADAM_EPS = 1e-08
ADAM_WD = 0.01
ADAM_STEP = 10

VMEM_LIMIT = 52 * 1024 * 1024
SCAN_ROWS = 256
S5_ROWS = 128
S5_GB = 8
S5_GBW = S5_GB * C_STATE


def _tile(dim, cands):
    for c in cands:
        if dim % c == 0:
            return c
    return dim


def _cparams(sem):
    return pltpu.CompilerParams(dimension_semantics=sem, vmem_limit_bytes=VMEM_LIMIT)


def _mm(a, b, *, nt=False, out_dtype=F32, name, tm=None, tn=None, tk=None):
    M, K = a.shape
    N = b.shape[0] if nt else b.shape[1]
    tm = tm or _tile(M, (1088, 1024, 512, 256, 128))
    tn = tn or _tile(N, (896, 1024, 640, 512, 256, 128))
    tk = tk or _tile(K, (2176, 1024, 896, 512, 256, 128))
    nk = K // tk
    dims = (((1,), (1,)), ((), ())) if nt else (((1,), (0,)), ((), ()))

    def body(a_ref, b_ref, o_ref, acc_ref):
        k = pl.program_id(2)

        @pl.when(k == 0)
        def _():
            acc_ref[...] = jnp.zeros_like(acc_ref)

        acc_ref[...] += lax.dot_general(a_ref[...].astype(BF16), b_ref[...].astype(BF16), dims,
                                        preferred_element_type=F32)

        @pl.when(k == nk - 1)
        def _():
            o_ref[...] = acc_ref[...].astype(o_ref.dtype)

    b_spec = (pl.BlockSpec((tn, tk), lambda i, j, k: (j, k)) if nt
              else pl.BlockSpec((tk, tn), lambda i, j, k: (k, j)))
    return pl.pallas_call(
        body, name=name, grid=(M // tm, N // tn, nk),
        in_specs=[pl.BlockSpec((tm, tk), lambda i, j, k: (i, k)), b_spec],
        out_specs=pl.BlockSpec((tm, tn), lambda i, j, k: (i, j)),
        out_shape=jax.ShapeDtypeStruct((M, N), out_dtype),
        scratch_shapes=[pltpu.VMEM((tm, tn), F32)],
        compiler_params=_cparams(("parallel", "parallel", "arbitrary")),
    )(a, b)


def _linear(name):
    @jax.custom_vjp
    def op(a, w, shadow):
        return _mm(a, w, name=name + "_fwd")

    def fwd(a, w, shadow):
        return _mm(a, w, name=name + "_fwd"), (a, w)

    def bwd(res, dy):
        a, w = res
        da = _mm(dy, w, nt=True, out_dtype=a.dtype, name=name + "_da")
        dw = _mm(a.T, dy, name=name + "_dw", tn=_tile(dy.shape[1], (640, 512, 256, 128)))
        return da, jnp.zeros_like(w), dw

    op.defvjp(fwd, bwd)
    return op


def _rowwise(fn, specs, out_defs, *, ncb, tr, n_ctx, name):
    ncx = n_ctx // tr
    diff = [k for k, s in enumerate(specs) if s in ("row", "const", "col", "seg")]

    def in_spec(s, arr):
        if s == "row":
            return pl.BlockSpec((tr, arr.shape[1] // ncb), lambda j, i: (i, j))
        if s == "rowc":
            return pl.BlockSpec((tr, arr.shape[1]), lambda j, i: (i, 0))
        if s in ("const", "fixed"):
            return pl.BlockSpec(arr.shape, lambda j, i: (0, 0))
        if s == "col":
            return pl.BlockSpec((None,) + arr.shape[1:], lambda j, i: (j, 0, 0))
        if s == "seg":
            return pl.BlockSpec((None,) + arr.shape[1:], lambda j, i: ((i >= ncx).astype(jnp.int32), 0, 0))
        raise ValueError(s)

    def fwd_call(*arrays):
        L = arrays[specs.index("row")].shape[0]
        n_in = len(arrays)

        def body(*refs):
            outs = fn(*[r[...] for r in refs[:n_in]])
            for o_ref, o in zip(refs[n_in:], outs):
                o_ref[...] = o.astype(o_ref.dtype)

        return pl.pallas_call(
            body, name=name + "_fwd", grid=(ncb, L // tr),
            in_specs=[in_spec(s, a) for s, a in zip(specs, arrays)],
            out_specs=[pl.BlockSpec((tr, c // ncb), lambda j, i: (i, j)) for c, _ in out_defs],
            out_shape=[jax.ShapeDtypeStruct((L, c), dt) for c, dt in out_defs],
            compiler_params=_cparams(("parallel", "arbitrary")),
        )(*arrays)

    def bwd_call(arrays, cts):
        L = arrays[specs.index("row")].shape[0]
        n_in, n_ct = len(arrays), len(cts)

        def body(*refs):
            i = pl.program_id(1)
            vals = [r[...] for r in refs[:n_in]]
            ct_vals = tuple(r[...] for r in refs[n_in:n_in + n_ct])
            g_refs = refs[n_in + n_ct:]

            def f(*dv):
                full = list(vals)
                for k, v in zip(diff, dv):
                    full[k] = v
                return tuple(o.astype(dt) for o, (_, dt) in zip(fn(*full), out_defs))

            _, vjp = jax.vjp(f, *[vals[k] for k in diff])
            grads = vjp(ct_vals)
            for k, g_ref, g in zip(diff, g_refs, grads):
                if specs[k] == "row":
                    g_ref[...] = g.astype(g_ref.dtype)
                else:
                    first = (i == 0) | (i == ncx) if specs[k] == "seg" else (i == 0)

                    @pl.when(first)
                    def _(g_ref=g_ref, g=g):
                        g_ref[...] = g

                    @pl.when(jnp.logical_not(first))
                    def _(g_ref=g_ref, g=g):
                        g_ref[...] += g

        out_specs, out_shape = [], []
        for k in diff:
            s, a = specs[k], arrays[k]
            if s == "row":
                out_specs.append(pl.BlockSpec((tr, a.shape[1] // ncb), lambda j, i: (i, j)))
                out_shape.append(jax.ShapeDtypeStruct(a.shape, a.dtype))
            elif s == "const":
                out_specs.append(pl.BlockSpec((None,) + a.shape, lambda j, i: (j, 0, 0)))
                out_shape.append(jax.ShapeDtypeStruct((ncb,) + a.shape, F32))
            elif s == "col":
                out_specs.append(pl.BlockSpec((None,) + a.shape[1:], lambda j, i: (j, 0, 0)))
                out_shape.append(jax.ShapeDtypeStruct(a.shape, F32))
            else:
                out_specs.append(pl.BlockSpec((None, None) + a.shape[1:],
                                              lambda j, i: (j, (i >= ncx).astype(jnp.int32), 0, 0)))
                out_shape.append(jax.ShapeDtypeStruct((ncb,) + a.shape, F32))
        ct_specs = [pl.BlockSpec((tr, c // ncb), lambda j, i: (i, j)) for c, _ in out_defs]
        grads = pl.pallas_call(
            body, name=name + "_bwd", grid=(ncb, L // tr),
            in_specs=[in_spec(s, a) for s, a in zip(specs, arrays)] + ct_specs,
            out_specs=out_specs, out_shape=out_shape,
            compiler_params=_cparams(("parallel", "arbitrary")),
        )(*arrays, *cts)
        full = [jnp.zeros_like(a) for a in arrays]
        for k, g in zip(diff, grads):
            full[k] = jnp.sum(g, axis=0) if specs[k] in ("const", "seg") else g
        return tuple(full)

    @jax.custom_vjp
    def op(*arrays):
        return tuple(fwd_call(*arrays))

    def op_fwd(*arrays):
        return tuple(fwd_call(*arrays)), arrays

    def op_bwd(arrays, cts):
        return bwd_call(arrays, cts)

    op.defvjp(op_fwd, op_bwd)
    return op


def _sigmoid(x):
    return 1.0 / (1.0 + jnp.exp(-x))


def _log_sigmoid(x):
    return jnp.minimum(x, 0.0) - jnp.log(1.0 + jnp.exp(-jnp.abs(x)))


def _silu(x):
    return x * _sigmoid(x)


def _rms(x, g):
    return x * lax.rsqrt(jnp.mean(x * x, axis=-1, keepdims=True) + EPS) * g


def _modnorm_fn(h, g, sc, sh):
    return ((_rms(h, g) * (1.0 + sc) + sh).astype(BF16),)


def _resid_fn(h, y, gt):
    return (h + gt * y,)


def _hgrn_pro_fn(zf, zb, lbf, lbb):
    def one(z, lb):
        f = lb + (1.0 - lb) * _sigmoid(z)
        return (1.0 - lb) * _sigmoid(-z), jnp.log(jnp.maximum(f, A_MIN_FORGET))

    kf, gf = one(zf, lbf)
    kb, gb = one(zb, lbb)
    return kf, gf, kb, gb


def _hgrn_epi_fn(of, ob, gate, g):
    return ((_rms(of + ob, g) * _silu(gate)).astype(BF16),)


def _gla_pro_fn(q, lr, wf, wb, bf, bb):
    gf = _log_sigmoid(jnp.dot(lr, wf, preferred_element_type=F32) + bf) / B_GATE_NORM
    gb = _log_sigmoid(jnp.dot(lr, wb, preferred_element_type=F32) + bb) / B_GATE_NORM
    return q * (HEAD_DK ** -0.5), gf, gb


def _ret_pro_fn(q, k, cos2, sin2, lgf, lgb, swap):
    def rope(x):
        return x * cos2 + jnp.dot(x, swap, preferred_element_type=F32, precision=lax.Precision.HIGHEST) * sin2

    zero = jnp.zeros_like(q)
    return rope(q) * (HEAD_DK ** -0.5), rope(k), zero + lgf, zero + lgb


def _ret_epi_fn(of, ob, gate, g):
    x = of + ob
    mu = jnp.mean(x, axis=-1, keepdims=True)
    var = jnp.mean(jnp.square(x - mu), axis=-1, keepdims=True)
    return (((x - mu) * lax.rsqrt(var + EPS) * g * _silu(gate)).astype(BF16),)


def _s5_mid_fn(yf, yb, u, d):
    return (jax.nn.gelu(yf + yb + d * u),)


def _s5_epi_fn(z, t, gate, b):
    return ((z * _sigmoid(t + b) * _silu(gate)).astype(BF16),)


def _blk(mode, i, nb, nc):
    if mode == 0:
        return i
    if mode == 1:
        return nb - 1 - i
    if mode == 2:
        return jnp.where(i < nc, nc - 1 - i, nb - 1 - i + nc)
    return jnp.where(i < nb - nc, i + nc, nc - nb + i)


def _split3(x):
    hi = x.astype(BF16)
    r1 = x - hi.astype(F32)
    mid = r1.astype(BF16)
    lo = (r1 - mid.astype(F32)).astype(BF16)
    return hi, mid, lo


def _cum(tri, x):
    hi, mid, lo = _split3(x)
    d = lambda v: jnp.dot(tri, v, preferred_element_type=F32)
    return d(hi) + d(mid) + d(lo)


def _dot(a, b):
    return jnp.dot(a.astype(BF16), b.astype(BF16), preferred_element_type=F32)


def _dot_nt(a, b):
    return lax.dot_general(a.astype(BF16), b.astype(BF16), (((1,), (1,)), ((), ())), preferred_element_type=F32)


def _dot_tn(a, b):
    return jnp.dot(a.astype(F32).T.astype(BF16), b.astype(BF16), preferred_element_type=F32)


def _dot3(a, b):
    ah = a.astype(BF16)
    al = (a - ah.astype(F32)).astype(BF16)
    bh = b.astype(BF16)
    bl = (b - bh.astype(F32)).astype(BF16)
    d = lambda p, q: jnp.dot(p, q, preferred_element_type=F32)
    return d(ah, bh) + d(ah, bl) + d(al, bh)


def _dot3_tn(a, b):
    return _dot3(a.astype(F32).T, b)


def _gla_masks(flip):
    r = lax.broadcasted_iota(jnp.int32, (CHUNK, CHUNK), 0)
    c = lax.broadcasted_iota(jnp.int32, (CHUNK, CHUNK), 1)
    causal = (c >= r) if flip else (c <= r)
    tri = causal.astype(BF16)
    tri_t = jnp.logical_not(causal) | (r == c)
    return causal, tri, tri_t.astype(BF16)


def _gla_chunk_terms(q, k, g, tri, flip):
    b = _cum(tri, g)
    mid = CHUNK // 2
    bm = b[mid:mid + 1, :]
    bl = b[0:1, :] if flip else b[CHUNK - 1:CHUNK, :]
    e_q = jnp.exp(b - bm)
    e_k = jnp.exp(bm - b)
    return b, bl, e_q, e_k, q * e_q, k * e_k


def _gla_fwd_call(q, k, v, g, *, flip, n_ctx, name):
    L = q.shape[0]
    H = q.shape[1] // HEAD_DK
    dv = v.shape[1] // H
    nb, nc, sub = L // SCAN_ROWS, n_ctx // SCAN_ROWS, SCAN_ROWS // CHUNK
    mode = 2 if flip else 0
    order = list(range(sub))[::-1] if flip else list(range(sub))

    def body(q_ref, k_ref, v_ref, g_ref, o_ref, sp_ref, st_ref):
        @pl.when(pl.program_id(1) == 0)
        def _():
            st_ref[...] = jnp.zeros_like(st_ref)

        causal, tri, _ = _gla_masks(flip)
        for s in order:
            sl = pl.ds(s * CHUNK, CHUNK)
            qc, kc, vc, gc = q_ref[sl, :], k_ref[sl, :], v_ref[sl, :], g_ref[sl, :]
            b, bl, _, _, qt, kt = _gla_chunk_terms(qc, kc, gc, tri, flip)
            a = jnp.where(causal, _dot_nt(qt, kt), 0.0)
            sp = st_ref[...]
            sp_ref[s] = sp
            o_ref[sl, :] = _dot(a, vc) + _dot_nt(qc * jnp.exp(b), sp)
            st_ref[...] = sp * jnp.exp(bl) + _dot3_tn(vc, kc * jnp.exp(bl - b))

    row = lambda w: pl.BlockSpec((SCAN_ROWS, w), lambda h, i: (_blk(mode, i, nb, nc), h))
    return pl.pallas_call(
        body, name=name, grid=(H, nb),
        in_specs=[row(HEAD_DK), row(HEAD_DK), row(dv), row(HEAD_DK)],
        out_specs=[row(dv), pl.BlockSpec((None, None, sub, dv, HEAD_DK),
                                         lambda h, i: (h, _blk(mode, i, nb, nc), 0, 0, 0))],
        out_shape=[jax.ShapeDtypeStruct((L, H * dv), F32),
                   jax.ShapeDtypeStruct((H, nb, sub, dv, HEAD_DK), F32)],
        scratch_shapes=[pltpu.VMEM((dv, HEAD_DK), F32)],
        compiler_params=_cparams(("parallel", "arbitrary")),
    )(q, k, v, g)


def _gla_bwd_call(q, k, v, g, sp, do, *, flip, n_ctx, name):
    L = q.shape[0]
    H = q.shape[1] // HEAD_DK
    dv = v.shape[1] // H
    nb, nc, sub = L // SCAN_ROWS, n_ctx // SCAN_ROWS, SCAN_ROWS // CHUNK
    mode = 3 if flip else 1
    order = list(range(sub)) if flip else list(range(sub))[::-1]
    last = 0 if flip else CHUNK - 1

    def body(q_ref, k_ref, v_ref, g_ref, sp_ref, do_ref, dq_ref, dk_ref, dv_ref, dg_ref, ds_ref):
        @pl.when(pl.program_id(1) == 0)
        def _():
            ds_ref[...] = jnp.zeros_like(ds_ref)

        causal, tri, tri_t = _gla_masks(flip)
        is_last = lax.broadcasted_iota(jnp.int32, (CHUNK, HEAD_DK), 0) == last
        for s in order:
            sl = pl.ds(s * CHUNK, CHUNK)
            qc, kc, vc, gc, doc = q_ref[sl, :], k_ref[sl, :], v_ref[sl, :], g_ref[sl, :], do_ref[sl, :]
            b, bl, e_q, e_k, qt, kt = _gla_chunk_terms(qc, kc, gc, tri, flip)
            e_b, e_l = jnp.exp(b), jnp.exp(bl - b)
            a = jnp.where(causal, _dot_nt(qt, kt), 0.0)
            p, kd = qc * e_b, kc * e_l
            spv, ds = sp_ref[s], ds_ref[...]
            dv_ref[sl, :] = _dot_tn(a, doc) + _dot_nt(kd, ds)
            da = jnp.where(causal, _dot_nt(doc, vc), 0.0)
            dkd = _dot3(vc, ds)
            dq = _dot3(da, kt) * e_q + _dot3(doc, spv) * e_b
            dk = _dot3_tn(da, qt) * e_k + dkd * e_l
            dq_ref[sl, :] = dq
            dk_ref[sl, :] = dk
            db_last = (jnp.sum(dkd * kd, axis=0, keepdims=True)
                       + jnp.exp(bl) * jnp.sum(ds * spv, axis=0, keepdims=True))
            db = qc * dq - kc * dk + jnp.where(is_last, db_last, 0.0)
            dg_ref[sl, :] = _cum(tri_t, db)
            ds_ref[...] = ds * jnp.exp(bl) + _dot3_tn(doc, p)

    row = lambda w: pl.BlockSpec((SCAN_ROWS, w), lambda h, i: (_blk(mode, i, nb, nc), h))
    return pl.pallas_call(
        body, name=name, grid=(H, nb),
        in_specs=[row(HEAD_DK), row(HEAD_DK), row(dv), row(HEAD_DK),
                  pl.BlockSpec((None, None, sub, dv, HEAD_DK), lambda h, i: (h, _blk(mode, i, nb, nc), 0, 0, 0)),
                  row(dv)],
        out_specs=[row(HEAD_DK), row(HEAD_DK), row(dv), row(HEAD_DK)],
        out_shape=[jax.ShapeDtypeStruct(q.shape, F32), jax.ShapeDtypeStruct(k.shape, F32),
                   jax.ShapeDtypeStruct(v.shape, F32), jax.ShapeDtypeStruct(g.shape, F32)],
        scratch_shapes=[pltpu.VMEM((dv, HEAD_DK), F32)],
        compiler_params=_cparams(("parallel", "arbitrary")),
    )(q, k, v, g, sp, do)


def _gla(name, *, flip, n_ctx):
    @jax.custom_vjp
    def op(q, k, v, g):
        return _gla_fwd_call(q, k, v, g, flip=flip, n_ctx=n_ctx, name=name + "_fwd")[0]

    def fwd(q, k, v, g):
        o, sp = _gla_fwd_call(q, k, v, g, flip=flip, n_ctx=n_ctx, name=name + "_fwd")
        return o, (q, k, v, g, sp)

    def bwd(res, do):
        return tuple(_gla_bwd_call(*res, do, flip=flip, n_ctx=n_ctx, name=name + "_bwd"))

    op.defvjp(fwd, bwd)
    return op


def _s5_table(ar, ai, desc):
    def mul(p, q):
        return p[0] * q[0] - p[1] * q[1], p[0] * q[1] + p[1] * q[0]

    a1 = (ar.reshape(-1), ai.reshape(-1))
    ar = a1[0]
    a2 = mul(a1, a1)
    a4 = mul(a2, a2)
    pw = [a1]
    for _ in range(7):
        pw.append(mul(pw[-1], a1))
    if desc:
        pw = pw[::-1]
    zero = jnp.zeros_like(ar)
    rows = [a1[0], a1[1], a2[0], a2[1], a4[0], a4[1], zero, zero] + [p[0] for p in pw] + [p[1] for p in pw]
    return jnp.stack(rows, axis=0)


S5_LANES = C_GROUPS * C_STATE
S5_LC = 1024


def _s5_scan(xr_ref, xi_ref, tab_ref, car_ref, desc, on_slab=None):
    nslab = S5_ROWS // 8
    rows = lax.broadcasted_iota(jnp.int32, (8, S5_LC), 0)

    def slab(t, carry):
        r0 = pl.multiple_of(((nslab - 1 - t) if desc else t) * 8, 8)
        for lc in range(S5_LANES // S5_LC):
            cs = pl.ds(lc * S5_LC, S5_LC)
            xr, xi = xr_ref[pl.ds(r0, 8), cs], xi_ref[pl.ds(r0, 8), cs]
            for n, s in enumerate((1, 2, 4)):
                ar, ai = tab_ref[2 * n:2 * n + 1, cs], tab_ref[2 * n + 1:2 * n + 2, cs]
                keep = (rows < 8 - s) if desc else (rows >= s)
                sr = jnp.where(keep, pltpu.roll(xr, (8 - s) if desc else s, 0), 0.0)
                si = jnp.where(keep, pltpu.roll(xi, (8 - s) if desc else s, 0), 0.0)
                xr, xi = xr + ar * sr - ai * si, xi + ar * si + ai * sr
            cr, ci = car_ref[0:1, cs], car_ref[1:2, cs]
            pr, pi = tab_ref[8:16, cs], tab_ref[16:24, cs]
            xr, xi = xr + pr * cr - pi * ci, xi + pr * ci + pi * cr
            if on_slab is not None:
                on_slab(r0, cs, xr, xi, cr, ci, rows)
            xr_ref[pl.ds(r0, 8), cs] = xr
            xi_ref[pl.ds(r0, 8), cs] = xi
            e = 0 if desc else 7
            car_ref[0:1, cs] = xr[e:e + 1, :]
            car_ref[1:2, cs] = xi[e:e + 1, :]
        return carry

    lax.fori_loop(0, nslab, slab, 0)


def _s5_fwd_call(u, tab, wb_re, wb_im, wc_re, wc_im, *, flip, n_ctx, name):
    L = u.shape[0]
    nb, nc = L // S5_ROWS, n_ctx // S5_ROWS
    mode, desc = (2, True) if flip else (0, False)

    def body(u_ref, tab_ref, wbr_ref, wbi_ref, wcr_ref, wci_ref, y_ref, xr_ref, xi_ref, car_ref):
        @pl.when(pl.program_id(0) == 0)
        def _():
            car_ref[...] = jnp.zeros_like(car_ref)

        for gb in range(S5_GB):
            us, cs = pl.ds(gb * 128, 128), pl.ds(gb * S5_GBW, S5_GBW)
            ub = u_ref[:, us]
            xr_ref[:, cs] = _dot(ub, wbr_ref[gb])
            xi_ref[:, cs] = _dot(ub, wbi_ref[gb])
        _s5_scan(xr_ref, xi_ref, tab_ref, car_ref, desc)
        for gb in range(S5_GB):
            us, cs = pl.ds(gb * 128, 128), pl.ds(gb * S5_GBW, S5_GBW)
            y_ref[:, us] = _dot(xr_ref[:, cs], wcr_ref[gb]) - _dot(xi_ref[:, cs], wci_ref[gb])

    row = lambda w: pl.BlockSpec((S5_ROWS, w), lambda i: (_blk(mode, i, nb, nc), 0))
    full = lambda a: pl.BlockSpec(a.shape, lambda i: (0,) * a.ndim)
    return pl.pallas_call(
        body, name=name, grid=(nb,),
        in_specs=[row(BRANCH_W), full(tab), full(wb_re), full(wb_im), full(wc_re), full(wc_im)],
        out_specs=[row(BRANCH_W), row(S5_LANES), row(S5_LANES)],
        out_shape=[jax.ShapeDtypeStruct((L, BRANCH_W), F32), jax.ShapeDtypeStruct((L, S5_LANES), F32),
                   jax.ShapeDtypeStruct((L, S5_LANES), F32)],
        scratch_shapes=[pltpu.VMEM((8, S5_LANES), F32)],
        compiler_params=_cparams(("arbitrary",)),
    )(u, tab, wb_re, wb_im, wc_re, wc_im)


def _s5_bwd_call(dy, u, xr, xi, tab, wbt_re, wbt_im, wct_re, wct_im, *, flip, n_ctx, name):
    L = u.shape[0]
    nb, nc = L // S5_ROWS, n_ctx // S5_ROWS
    mode, desc = (3, False) if flip else (1, True)

    def body(dy_ref, u_ref, xr_ref, xi_ref, tab_ref, wbtr_ref, wbti_ref, wctr_ref, wcti_ref,
             du_ref, dwbr_ref, dwbi_ref, dwcr_ref, dwci_ref, ga_ref, gr_ref, gi_ref, car_ref):
        @pl.when(pl.program_id(0) == 0)
        def _():
            car_ref[...] = jnp.zeros_like(car_ref)
            ga_ref[...] = jnp.zeros_like(ga_ref)
            for r in (dwbr_ref, dwbi_ref, dwcr_ref, dwci_ref):
                r[...] = jnp.zeros_like(r)

        for gb in range(S5_GB):
            us, cs = pl.ds(gb * 128, 128), pl.ds(gb * S5_GBW, S5_GBW)
            dyb = dy_ref[:, us]
            gr_ref[:, cs] = _dot(dyb, wctr_ref[gb])
            gi_ref[:, cs] = -_dot(dyb, wcti_ref[gb])

        def lam_grad(r0, cs, gr, gi, cr, ci, rows):
            edge = 7 if desc else 0
            pr = jnp.where(rows == edge, cr, pltpu.roll(gr, 7 if desc else 1, 0))
            pi = jnp.where(rows == edge, ci, pltpu.roll(gi, 7 if desc else 1, 0))
            x_r, x_i = xr_ref[pl.ds(r0, 8), cs], xi_ref[pl.ds(r0, 8), cs]
            ga_ref[0:8, cs] += x_r * pr + x_i * pi
            ga_ref[8:16, cs] += x_r * pi - x_i * pr

        _s5_scan(gr_ref, gi_ref, tab_ref, car_ref, desc, on_slab=lam_grad)
        for gb in range(S5_GB):
            us, cs = pl.ds(gb * 128, 128), pl.ds(gb * S5_GBW, S5_GBW)
            g_r, g_i, ub, dyb = gr_ref[:, cs], gi_ref[:, cs], u_ref[:, us], dy_ref[:, us]
            du_ref[:, us] = _dot(g_r, wbtr_ref[gb]) + _dot(g_i, wbti_ref[gb])
            dwbr_ref[gb] += _dot_tn(ub, g_r)
            dwbi_ref[gb] += _dot_tn(ub, g_i)
            dwcr_ref[gb] += _dot_tn(xr_ref[:, cs], dyb)
            dwci_ref[gb] -= _dot_tn(xi_ref[:, cs], dyb)

    row = lambda w: pl.BlockSpec((S5_ROWS, w), lambda i: (_blk(mode, i, nb, nc), 0))
    full = lambda shape: pl.BlockSpec(shape, lambda i: (0,) * len(shape))
    wb_shape, wc_shape = (S5_GB, 128, S5_GBW), (S5_GB, S5_GBW, 128)
    return pl.pallas_call(
        body, name=name, grid=(nb,),
        in_specs=[row(BRANCH_W), row(BRANCH_W), row(S5_LANES), row(S5_LANES), full(tab.shape),
                  full(wc_shape), full(wc_shape), full(wb_shape), full(wb_shape)],
        out_specs=[row(BRANCH_W), full(wb_shape), full(wb_shape), full(wc_shape), full(wc_shape),
                   full((16, S5_LANES))],
        out_shape=[jax.ShapeDtypeStruct((L, BRANCH_W), F32),
                   jax.ShapeDtypeStruct(wb_shape, F32), jax.ShapeDtypeStruct(wb_shape, F32),
                   jax.ShapeDtypeStruct(wc_shape, F32), jax.ShapeDtypeStruct(wc_shape, F32),
                   jax.ShapeDtypeStruct((16, S5_LANES), F32)],
        scratch_shapes=[pltpu.VMEM((S5_ROWS, S5_LANES), F32), pltpu.VMEM((S5_ROWS, S5_LANES), F32),
                        pltpu.VMEM((8, S5_LANES), F32)],
        compiler_params=_cparams(("arbitrary",)),
    )(dy, u, xr, xi, tab, wbt_re, wbt_im, wct_re, wct_im)


def _s5(name, *, flip, n_ctx):
    desc_f = flip
    desc_b = not flip

    def run_fwd(u, lr, li, wbr, wbi, wcr, wci):
        tab = _s5_table(lr, li, desc_f)
        c16 = lambda a: a.astype(BF16)
        return _s5_fwd_call(u, tab, c16(wbr), c16(wbi), c16(wcr), c16(wci), flip=flip, n_ctx=n_ctx, name=name + "_fwd")

    @jax.custom_vjp
    def op(u, lr, li, wbr, wbi, wcr, wci):
        return run_fwd(u, lr, li, wbr, wbi, wcr, wci)[0]

    def fwd(u, lr, li, wbr, wbi, wcr, wci):
        y, xr, xi = run_fwd(u, lr, li, wbr, wbi, wcr, wci)
        return y, (u, xr, xi, lr, li, wbr, wbi, wcr, wci)

    def bwd(res, dy):
        u, xr, xi, lr, li, wbr, wbi, wcr, wci = res
        tab = _s5_table(lr, -li, desc_b)
        t16 = lambda a: jnp.swapaxes(a, 1, 2).astype(BF16)
        du, dwbr, dwbi, dwcr, dwci, ga = _s5_bwd_call(
            dy, u, xr, xi, tab, t16(wbr), t16(wbi), t16(wcr), t16(wci), flip=flip, n_ctx=n_ctx, name=name + "_bwd")
        return (du, jnp.sum(ga[0:8], axis=0, keepdims=True), jnp.sum(ga[8:16], axis=0, keepdims=True),
                dwbr, dwbi, dwcr, dwci)

    op.defvjp(fwd, bwd)
    return op


def _loss_call(h, tgt, g, *, n_ctx, name):
    L, D = h.shape
    tr = 128
    ncx = n_ctx // tr

    def body(h_ref, t_ref, g_ref, dh_ref, dg_ref, loss_ref):
        i = pl.program_id(0)

        @pl.when(i == 0)
        def _():
            dg_ref[...] = jnp.zeros_like(dg_ref)
            loss_ref[...] = jnp.zeros_like(loss_ref)

        @pl.when(i < ncx)
        def _():
            dh_ref[...] = jnp.zeros_like(dh_ref)

        @pl.when(i >= ncx)
        def _():
            t = t_ref[...]

            def f(hv, gv):
                e = _rms(hv, gv) - t
                return 0.5 * jnp.sum(e * e) / D

            l, (dh, dg) = jax.value_and_grad(f, argnums=(0, 1))(h_ref[...], g_ref[...])
            dh_ref[...] = dh
            dg_ref[...] += dg
            loss_ref[...] += l

    return pl.pallas_call(
        body, name=name, grid=(L // tr,),
        in_specs=[pl.BlockSpec((tr, D), lambda i: (i, 0)),
                  pl.BlockSpec((tr, D), lambda i: (jnp.maximum(i - ncx, 0), 0)),
                  pl.BlockSpec((1, D), lambda i: (0, 0))],
        out_specs=[pl.BlockSpec((tr, D), lambda i: (i, 0)), pl.BlockSpec((1, D), lambda i: (0, 0)),
                   pl.BlockSpec((8, 128), lambda i: (0, 0))],
        out_shape=[jax.ShapeDtypeStruct((L, D), F32), jax.ShapeDtypeStruct((1, D), F32),
                   jax.ShapeDtypeStruct((8, 128), F32)],
        compiler_params=_cparams(("arbitrary",)),
    )(h, tgt, g)


def _adamw_call(w, g, m, v, *, name):
    R, C = w.shape
    tr = R if R * C * 4 <= (1 << 20) else _tile(R, tuple(t for t in (512, 256, 128, 64, 32, 16, 8) if t * C * 4 <= (1 << 20)))
    c1 = 1.0 / (1.0 - ADAM_B1 ** ADAM_STEP)
    c2 = 1.0 / (1.0 - ADAM_B2 ** ADAM_STEP)

    def body(w_ref, g_ref, m_ref, v_ref, d_ref, nm_ref, nv_ref):
        gv = g_ref[...]
        nm = ADAM_B1 * m_ref[...] + (1.0 - ADAM_B1) * gv
        nv = ADAM_B2 * v_ref[...] + (1.0 - ADAM_B2) * gv * gv
        d_ref[...] = -ADAM_LR * ((nm * c1) / (jnp.sqrt(nv * c2) + ADAM_EPS) + ADAM_WD * w_ref[...])
        nm_ref[...] = nm
        nv_ref[...] = nv

    spec = pl.BlockSpec((tr, C), lambda i: (i, 0))
    return pl.pallas_call(
        body, name=name, grid=(R // tr,), in_specs=[spec] * 4, out_specs=[spec] * 3,
        out_shape=[jax.ShapeDtypeStruct((R, C), F32)] * 3,
        compiler_params=_cparams(("parallel",)),
    )(w, g, m, v)


def _as2d(a):
    n = a.size
    if a.ndim >= 2 and n >= (1 << 20):
        return a.reshape(-1, a.shape[-1])
    if n % 128 == 0:
        return a.reshape(n // 128, 128)
    return a.reshape(1, n)


def _adamw(w, g, m, v, *, name):
    d, nm, nv = _adamw_call(_as2d(w), _as2d(g.astype(F32)), _as2d(m), _as2d(v), name=name)
    return d.reshape(w.shape), nm.reshape(w.shape), nv.reshape(w.shape)


MESH = pl.DeviceIdType.MESH
ANY = pl.BlockSpec(memory_space=pl.ANY)


def _place():
    return lax.axis_index("x"), lax.axis_index("y"), lax.axis_index("c")


def _all_gather(x, *, name):
    M, N = x.shape

    def body(x_ref, out_ref, send_sems, recv_sems, local_sem):
        x_, y_, c_ = _place()
        me, sibling = (x_, y_, c_), (x_, y_, 1 - c_)
        chips = [(1 - x_, y_), (x_, 1 - y_), (1 - x_, 1 - y_)]

        def slab(px, py, pc):
            return out_ref.at[4 * px + 2 * py + pc]

        def copy(k, block, to, src=None):
            return pltpu.make_async_remote_copy(
                src_ref=slab(*block) if src is None else src, dst_ref=slab(*block),
                send_sem=send_sems.at[k], recv_sem=recv_sems.at[k], device_id=to, device_id_type=MESH)

        mine = pltpu.make_async_copy(x_ref, slab(*me), local_sem)
        mine.start()
        first = [copy(0, me, sibling, src=x_ref)]
        first += [copy(1 + j, me, (*chip, c_), src=x_ref) for j, chip in enumerate(chips)]
        for cp in first:
            cp.start()
        passed = [copy(4 + j, (*chip, c_), sibling) for j, chip in enumerate(chips)]
        for j, chip in enumerate(chips):
            copy(1 + j, (*chip, c_), me).wait_recv()
            passed[j].start()
        copy(0, sibling, me).wait_recv()
        for j, chip in enumerate(chips):
            copy(4 + j, (*chip, 1 - c_), me).wait_recv()
        for cp in first + passed:
            cp.wait_send()
        mine.wait()

    return pl.pallas_call(
        body, name=name, out_shape=jax.ShapeDtypeStruct((N_DEV, M, N), x.dtype),
        in_specs=[ANY], out_specs=ANY,
        scratch_shapes=[pltpu.SemaphoreType.DMA((7,)), pltpu.SemaphoreType.DMA((7,)), pltpu.SemaphoreType.DMA],
    )(x)


def _pair_exchange(g, *, name):
    _, R, C = g.shape

    def body(g_ref, out_ref, send_sems, recv_sems):
        x_, y_, c_ = _place()
        copies = []
        for k in range(4):
            cp = pltpu.make_async_remote_copy(
                src_ref=g_ref.at[2 * k + (1 - c_)], dst_ref=out_ref.at[k], send_sem=send_sems.at[k],
                recv_sem=recv_sems.at[k], device_id=(x_, y_, 1 - c_), device_id_type=MESH)
            cp.start()
            copies.append(cp)
        for cp in copies:
            cp.wait()

    return pl.pallas_call(
        body, name=name, out_shape=jax.ShapeDtypeStruct((4, R, C), g.dtype), in_specs=[ANY], out_specs=ANY,
        scratch_shapes=[pltpu.SemaphoreType.DMA((4,)), pltpu.SemaphoreType.DMA((4,))],
    )(g)


def _chip_exchange(p, *, name):
    _, R, C = p.shape

    def body(p_ref, out_ref, send_sems, recv_sems):
        x_, y_, c_ = _place()
        chips = [(1 - x_, y_), (x_, 1 - y_), (1 - x_, 1 - y_)]
        copies = []
        for j, (cx, cy) in enumerate(chips):
            cp = pltpu.make_async_remote_copy(
                src_ref=p_ref.at[2 * cx + cy], dst_ref=out_ref.at[j], send_sem=send_sems.at[j],
                recv_sem=recv_sems.at[j], device_id=(cx, cy, c_), device_id_type=MESH)
            cp.start()
            copies.append(cp)
        for cp in copies:
            cp.wait()

    return pl.pallas_call(
        body, name=name, out_shape=jax.ShapeDtypeStruct((3, R, C), p.dtype), in_specs=[ANY], out_specs=ANY,
        scratch_shapes=[pltpu.SemaphoreType.DMA((3,)), pltpu.SemaphoreType.DMA((3,))],
    )(p)


def _add_tr(R, C):
    return _tile(R, tuple(t for t in (512, 256, 128, 64, 32, 16, 8) if t * C * 4 <= (2 << 20)))


def _pair_add(g, recv, *, name):
    _, R, C = g.shape
    tr = _add_tr(R, C)
    core = lax.axis_index("c").astype(jnp.int32).reshape(1)

    def body(c_ref, g_ref, r_ref, o_ref):
        o_ref[...] = g_ref[...] + r_ref[...]

    return pl.pallas_call(
        body, name=name, out_shape=jax.ShapeDtypeStruct((4, R, C), F32),
        grid_spec=pltpu.PrefetchScalarGridSpec(
            num_scalar_prefetch=1, grid=(4, R // tr),
            in_specs=[pl.BlockSpec((None, tr, C), lambda k, i, c: (2 * k + c[0], i, 0)),
                      pl.BlockSpec((None, tr, C), lambda k, i, c: (k, i, 0))],
            out_specs=pl.BlockSpec((None, tr, C), lambda k, i, c: (k, i, 0))),
        compiler_params=_cparams(("parallel", "parallel")),
    )(core, g, recv)


def _chip_add(p, recv, *, name):
    _, R, C = p.shape
    tr = _add_tr(R, C)
    chip = (2 * lax.axis_index("x") + lax.axis_index("y")).astype(jnp.int32).reshape(1)

    def body(c_ref, p_ref, r0_ref, r1_ref, r2_ref, o_ref):
        o_ref[...] = ((p_ref[...] + r0_ref[...]) + r1_ref[...]) + r2_ref[...]

    rspec = lambda j: pl.BlockSpec((None, tr, C), lambda i, c: (j, i, 0))
    return pl.pallas_call(
        body, name=name, out_shape=jax.ShapeDtypeStruct((R, C), F32),
        grid_spec=pltpu.PrefetchScalarGridSpec(
            num_scalar_prefetch=1, grid=(R // tr,),
            in_specs=[pl.BlockSpec((None, tr, C), lambda i, c: (c[0], i, 0)), rspec(0), rspec(1), rspec(2)],
            out_specs=pl.BlockSpec((tr, C), lambda i, c: (i, 0))),
        compiler_params=_cparams(("parallel",)),
    )(chip, p, recv, recv, recv)


def _reduce_scatter(g, *, name):
    p = _pair_add(g, _pair_exchange(g, name=name + "_pair"), name=name + "_pair_add")
    return _chip_add(p, _chip_exchange(p, name=name + "_chip"), name=name + "_chip_add")


def _sum8(a, *, name):
    _, R, C = a.shape
    tr = _tile(R, tuple(t for t in (512, 256, 128, 64, 32, 16, 8) if t * C * 4 * 8 <= (4 << 20)))

    def body(a_ref, o_ref):
        acc = a_ref[0]
        for d in range(1, N_DEV):
            acc = acc + a_ref[d]
        o_ref[...] = acc

    return pl.pallas_call(
        body, name=name, grid=(R // tr,), in_specs=[pl.BlockSpec((N_DEV, tr, C), lambda i: (0, i, 0))],
        out_specs=pl.BlockSpec((tr, C), lambda i: (i, 0)), out_shape=jax.ShapeDtypeStruct((R, C), F32),
        compiler_params=_cparams(("parallel",)),
    )(a)


def _s5_params(lam_re, lam_im, log_dt, b_re, b_im, c_re, c_im):
    lr = jnp.minimum(lam_re, C_MAX_RE)
    li = lam_im
    dt = jnp.exp(log_dt)[:, None]
    mag = jnp.exp(lr * dt)
    ar, ai = mag * jnp.cos(li * dt), mag * jnp.sin(li * dt)
    nr, ni, den = ar - 1.0, ai, lr * lr + li * li
    fr, fi = (nr * lr + ni * li) / den, (ni * lr - nr * li) / den
    bbr = fr[..., None] * b_re - fi[..., None] * b_im
    bbi = fr[..., None] * b_im + fi[..., None] * b_re
    eye = jnp.eye(S5_GB, dtype=F32)

    def dense_b(bb):
        t = bb.reshape(S5_GB, S5_GB, C_STATE, C_GROUP).transpose(0, 1, 3, 2)
        return (t[:, :, :, None, :] * eye[None, :, None, :, None]).reshape(S5_GB, S5_GB * C_GROUP, S5_GBW)

    def dense_c(cc):
        t = cc.reshape(S5_GB, S5_GB, C_GROUP, C_STATE).transpose(0, 1, 3, 2)
        return (t[:, :, :, None, :] * eye[None, :, None, :, None]).reshape(S5_GB, S5_GBW, S5_GB * C_GROUP)

    flat = lambda a: a.reshape(1, S5_LANES)
    return flat(ar), flat(ai), dense_b(bbr), dense_b(bbi), dense_c(c_re), dense_c(c_im)


def _rope_tables(n_ctx, n_lat):
    quarter = HEAD_DK // 4
    freqs = ROPE_BASE ** (-jnp.arange(quarter, dtype=F32) / quarter)
    t = jnp.arange(n_lat)
    r = (t // GRID_W).astype(F32)
    col = (t % GRID_W).astype(F32)
    ang = jnp.concatenate([r[:, None] * freqs, col[:, None] * freqs], axis=-1)
    ang = jnp.concatenate([jnp.zeros((n_ctx, HEAD_DK // 2), F32), ang], axis=0)
    cos, sin = jnp.cos(ang), jnp.sin(ang)
    idx = jnp.arange(HEAD_DK)
    swap = (idx[:, None] == (idx[None, :] + HEAD_DK // 2) % HEAD_DK).astype(F32)
    return jnp.concatenate([cos, cos], axis=1), jnp.concatenate([-sin, sin], axis=1), swap


W_IN = 13344
W_INP = 13440
_COL = dict(a_q=0, a_ff=1024, a_fb=2048, a_i=3072, a_g=4096, b_q=5120, b_k=5632, b_v=6144, b_g=7168,
            c_u=8192, c_g=9216, d_q=10240, d_k=10752, d_v=11264, d_g=12288, b_lr=13312)


def _reorder_w_in(w):
    pad = jnp.zeros(w.shape[:-1] + (W_INP - W_IN,), w.dtype)
    return jnp.concatenate([w[..., :7168], w[..., 7200:], w[..., 7168:7200], pad], axis=-1)


def _restore_w_in(w):
    return jnp.concatenate([w[..., :7168], w[..., 13312:13344], w[..., 7168:13312]], axis=-1)


def _layer_ops(l, n_ctx):
    n = f"l{l}_"
    rw = functools.partial(_rowwise, n_ctx=n_ctx)
    wide = [(BRANCH_W, F32)]
    return dict(
        modnorm=rw(_modnorm_fn, ["row", "const", "seg", "seg"], [(4096, BF16)], ncb=1, tr=128, name=n + "modnorm"),
        resid=rw(_resid_fn, ["row", "row", "seg"], [(4096, F32)], ncb=1, tr=128, name=n + "resid"),
        lin_in=_linear(n + "lin_in"), lin_out=_linear(n + "lin_out"), lin_glu=_linear(n + "lin_glu"),
        hgrn_pro=rw(_hgrn_pro_fn, ["row", "row", "const", "const"], wide * 4, ncb=1, tr=256, name=n + "hgrn_pro"),
        hgrn_f=_gla(n + "hgrn_f", flip=False, n_ctx=n_ctx), hgrn_b=_gla(n + "hgrn_b", flip=True, n_ctx=n_ctx),
        hgrn_epi=rw(_hgrn_epi_fn, ["row", "row", "row", "const"], [(BRANCH_W, BF16)], ncb=8, tr=256, name=n + "hgrn_epi"),
        gla_pro=rw(_gla_pro_fn, ["row", "row", "const", "const", "const", "const"], [(512, F32)] * 3, ncb=1, tr=256,
                   name=n + "gla_pro"),
        gla_f=_gla(n + "gla_f", flip=False, n_ctx=n_ctx), gla_b=_gla(n + "gla_b", flip=True, n_ctx=n_ctx),
        gla_epi=rw(_hgrn_epi_fn, ["row", "row", "row", "const"], [(BRANCH_W, BF16)], ncb=4, tr=256, name=n + "gla_epi"),
        s5_f=_s5(n + "s5_f", flip=False, n_ctx=n_ctx), s5_b=_s5(n + "s5_b", flip=True, n_ctx=n_ctx),
        s5_mid=rw(_s5_mid_fn, ["row", "row", "row", "const"], wide, ncb=1, tr=256, name=n + "s5_mid"),
        s5_epi=rw(_s5_epi_fn, ["row", "row", "row", "const"], [(BRANCH_W, BF16)], ncb=1, tr=256, name=n + "s5_epi"),
        ret_pro=rw(_ret_pro_fn, ["row", "row", "rowc", "rowc", "col", "col", "fixed"], [(512, F32)] * 4, ncb=4, tr=256,
                   name=n + "ret_pro"),
        ret_f=_gla(n + "ret_f", flip=False, n_ctx=n_ctx), ret_b=_gla(n + "ret_b", flip=True, n_ctx=n_ctx),
        ret_epi=rw(_ret_epi_fn, ["row", "row", "row", "const"], [(BRANCH_W, BF16)], ncb=4, tr=256, name=n + "ret_epi"),
    )


def _layer(ops, h, p, w_in, w_out, w_glu, lb, rope):
    cos2, sin2, swap = rope
    (hn,) = ops["modnorm"](h, p["norm_g"], p["sc"], p["sh"])
    proj = ops["lin_in"](hn, w_in, p["shadow_in"])
    col = lambda name, w: lax.slice_in_dim(proj, _COL[name], _COL[name] + w, axis=1)

    a_q, a_i = col("a_q", 1024), col("a_i", 1024)
    kf, gf, kb, gb = ops["hgrn_pro"](col("a_ff", 1024), col("a_fb", 1024), lb[0:1], lb[1:2])
    (o_a,) = ops["hgrn_epi"](ops["hgrn_f"](a_q, kf, a_i, gf), ops["hgrn_b"](a_q, kb, a_i, gb), col("a_g", 1024),
                             p["hgrn_norm_g"])

    rank = B_GATE_RANK
    wf = jnp.pad(p["gla_w_gk"][0], ((0, 128 - rank), (0, 0)))
    wb = jnp.pad(p["gla_w_gk"][1], ((rank, 128 - 2 * rank), (0, 0)))
    b_k, b_v = col("b_k", 512), col("b_v", 1024)
    qs, gf, gb = ops["gla_pro"](col("b_q", 512), col("b_lr", 128), wf, wb, p["gla_b_gk"][0:1], p["gla_b_gk"][1:2])
    (o_b,) = ops["gla_epi"](ops["gla_f"](qs, b_k, b_v, gf), ops["gla_b"](qs, b_k, b_v, gb), col("b_g", 1024),
                            p["gla_norm_g"])

    c_u = col("c_u", 1024)
    s5p = lambda d: _s5_params(*[p[k][d] for k in ("s5_lam_re", "s5_lam_im", "s5_log_dt", "s5_b_re", "s5_b_im",
                                                  "s5_c_re", "s5_c_im")])
    (z,) = ops["s5_mid"](ops["s5_f"](c_u, *s5p(0)), ops["s5_b"](c_u, *s5p(1)), c_u, p["s5_d"].reshape(1, BRANCH_W))
    t = ops["lin_glu"](z, w_glu, p["shadow_glu"])
    (o_c,) = ops["s5_epi"](z, t, col("c_g", 1024), p["s5_b_glu"])

    log_gamma = _log_sigmoid(p["ret_decay_logit"])
    lg = lambda d: jnp.broadcast_to(log_gamma[d][:, None, None], (4, 1, HEAD_DK))
    d_v = col("d_v", 1024)
    qs, ks, gf, gb = ops["ret_pro"](col("d_q", 512), col("d_k", 512), cos2, sin2, lg(0), lg(1), swap)
    (o_d,) = ops["ret_epi"](ops["ret_f"](qs, ks, d_v, gf), ops["ret_b"](qs, ks, d_v, gb), col("d_g", 1024),
                            p["ret_norm_g"])

    o = jnp.concatenate([o_a, o_b, o_c, o_d], axis=1)
    y = ops["lin_out"](o, w_out, p["shadow_out"])
    (h,) = ops["resid"](h, y, p["gt"])
    return h


_SMALL = ("norm_g", "hgrn_lb_logits", "hgrn_norm_g", "gla_w_gk", "gla_b_gk", "gla_norm_g", "s5_lam_re", "s5_lam_im",
          "s5_log_dt", "s5_b_re", "s5_b_im", "s5_c_re", "s5_c_im", "s5_d", "s5_b_glu", "ret_decay_logit", "ret_norm_g",
          "final_norm_g")
_WEIGHTS = ("c_ctx", "norm_g", "w_ada", "b_ada", "w_in", "hgrn_lb_logits", "hgrn_norm_g", "gla_w_gk", "gla_b_gk",
            "gla_norm_g", "s5_lam_re", "s5_lam_im", "s5_log_dt", "s5_b_re", "s5_b_im", "s5_c_re", "s5_c_im", "s5_d",
            "s5_w_glu", "s5_b_glu", "ret_decay_logit", "ret_norm_g", "w_out", "final_norm_g")


def _pad_rows(flat, mult=8):
    rows = -(-flat.size // 128)
    rows = -(-rows // mult) * mult
    return jnp.pad(flat, (0, rows * 128 - flat.size)).reshape(rows, 128)


def _unshard_cols(g, inner):
    k = len(inner)
    a = g.reshape((N_DEV,) + tuple(inner))
    return jnp.moveaxis(a, 0, k - 1).reshape(tuple(inner[:-1]) + (N_DEV * inner[-1],))


def kernel(x, c, ctx, c_ctx, norm_g, w_ada, b_ada, w_in, hgrn_lb_logits, hgrn_norm_g, gla_w_gk, gla_b_gk, gla_norm_g, s5_lam_re, s5_lam_im, s5_log_dt, s5_b_re, s5_b_im, s5_c_re, s5_c_im, s5_d, s5_w_glu, s5_b_glu, ret_decay_logit, ret_norm_g, w_out, final_norm_g, loss_target, m_c_ctx, m_norm_g, m_w_ada, m_b_ada, m_w_in, m_hgrn_lb_logits, m_hgrn_norm_g, m_gla_w_gk, m_gla_b_gk, m_gla_norm_g, m_s5_lam_re, m_s5_lam_im, m_s5_log_dt, m_s5_b_re, m_s5_b_im, m_s5_c_re, m_s5_c_im, m_s5_d, m_s5_w_glu, m_s5_b_glu, m_ret_decay_logit, m_ret_norm_g, m_w_out, m_final_norm_g, v_c_ctx, v_norm_g, v_w_ada, v_b_ada, v_w_in, v_hgrn_lb_logits, v_hgrn_norm_g, v_gla_w_gk, v_gla_b_gk, v_gla_norm_g, v_s5_lam_re, v_s5_lam_im, v_s5_log_dt, v_s5_b_re, v_s5_b_im, v_s5_c_re, v_s5_c_im, v_s5_d, v_s5_w_glu, v_s5_b_glu, v_ret_decay_logit, v_ret_norm_g, v_w_out, v_final_norm_g):
    given = dict(locals())
    depth = norm_g.shape[0]
    n_ctx, n_lat, d_model = ctx.shape[1], x.shape[1], x.shape[2]
    me = 4 * lax.axis_index("x") + 2 * lax.axis_index("y") + lax.axis_index("c")
    w_ada_cols = w_ada.shape[2]

    pack = jnp.concatenate([c.reshape(-1), hgrn_lb_logits.reshape(-1), gla_w_gk.reshape(-1), gla_b_gk.reshape(-1)])
    got = _all_gather(_pad_rows(pack), name="gather_small").reshape(N_DEV, -1)
    o0 = d_model
    o1 = o0 + hgrn_lb_logits.size
    o2 = o1 + gla_w_gk.size
    o3 = o2 + gla_b_gk.size
    c_all = got[:, :o0]
    lb_logits = _unshard_cols(got[:, o0:o1], hgrn_lb_logits.shape)
    w_gk = _unshard_cols(got[:, o1:o2], gla_w_gk.shape)
    b_gk = _unshard_cols(got[:, o2:o3], gla_b_gk.shape)

    cond = jnp.concatenate([c_all, c_ctx[None, :], jnp.zeros((7, d_model), F32)], axis=0)
    act = _silu(cond).astype(BF16)
    mod_part = jnp.concatenate([_mm(act, w_ada[l], name=f"ada{l}_fwd") for l in range(depth)], axis=0)
    mod = _all_gather(mod_part, name="gather_mod").reshape(N_DEV, depth, 16, w_ada_cols)
    mod = jnp.moveaxis(mod, 0, 2).reshape(depth, 16, N_DEV * w_ada_cols) + b_ada[:, None, :]
    mod_lat = lax.dynamic_index_in_dim(mod, me, axis=1, keepdims=False)
    mod_ctx = mod[:, 8]
    seg = lambda l, j: jnp.stack([mod_ctx[l, j * d_model:(j + 1) * d_model],
                                  mod_lat[l, j * d_model:(j + 1) * d_model]])[:, None, :]

    def gathered(w, name):
        flat = w.astype(BF16).reshape(-1, w.shape[-1])
        return _all_gather(flat, name=name).reshape((N_DEV,) + w.shape)

    w_in_full = _reorder_w_in(jnp.moveaxis(gathered(w_in, "gather_w_in"), 0, 2).reshape(depth, d_model, W_IN))
    w_out_full = jnp.moveaxis(gathered(w_out, "gather_w_out"), 0, 1).reshape(depth, 4 * BRANCH_W, d_model)
    w_glu_full = jnp.moveaxis(gathered(s5_w_glu, "gather_w_glu"), 0, 1).reshape(depth, BRANCH_W, BRANCH_W)

    rope = _rope_tables(n_ctx, n_lat)
    ops = [_layer_ops(l, n_ctx) for l in range(depth)]
    params = dict(
        layers=[dict(
            norm_g=norm_g[l][None, :], sh=seg(l, 0), sc=seg(l, 1), gt=seg(l, 2),
            shadow_in=jnp.zeros((d_model, W_INP), F32), shadow_out=jnp.zeros((4 * BRANCH_W, d_model), F32),
            shadow_glu=jnp.zeros((BRANCH_W, BRANCH_W), F32),
            hgrn_norm_g=hgrn_norm_g[l][None, :], gla_w_gk=w_gk[l], gla_b_gk=b_gk[l], gla_norm_g=gla_norm_g[l][None, :],
            s5_lam_re=s5_lam_re[l], s5_lam_im=s5_lam_im[l], s5_log_dt=s5_log_dt[l], s5_b_re=s5_b_re[l],
            s5_b_im=s5_b_im[l], s5_c_re=s5_c_re[l], s5_c_im=s5_c_im[l], s5_d=s5_d[l], s5_b_glu=s5_b_glu[l][None, :],
            ret_decay_logit=ret_decay_logit[l], ret_norm_g=ret_norm_g[l][None, :]) for l in range(depth)],
        lb_logits=lb_logits,
        h0=jnp.concatenate([ctx[0], x[0]], axis=0))

    def body(p):
        lb_p = jax.nn.softmax(p["lb_logits"], axis=0)
        lower = jnp.cumsum(lb_p, axis=0) - lb_p[0:1]
        h = p["h0"]
        for l in range(depth):
            h = _layer(ops[l], h, p["layers"][l], w_in_full[l], w_out_full[l], w_glu_full[l], lower[l], rope)
        return h

    h_out, pullback = jax.vjp(body, params)
    dh, dg_final, loss_part = _loss_call(h_out, loss_target[0], final_norm_g[None, :], n_ctx=n_ctx, name="loss_head")
    (grads,) = pullback(dh)
    loss = lax.psum(loss_part[0, 0], ("x", "y", "c"))
    gl = grads["layers"]

    dmod = jnp.stack([jnp.concatenate([gl[l][k][s, 0] for k in ("sh", "sc", "gt")])
                      for l in range(depth) for s in (1, 0)])
    dmod = jnp.pad(dmod, ((0, 8 - 2 * depth), (0, 0)))
    dmod_all = _all_gather(dmod, name="gather_dmod")
    dmod_sum = _sum8(dmod_all, name="sum_dmod")
    grad_b_ada = jnp.stack([dmod_sum[2 * l] + dmod_sum[2 * l + 1] for l in range(depth)])
    grad_w_ada, ds_ctx = [], jnp.zeros((d_model,), F32)
    for l in range(depth):
        rows = jnp.concatenate([dmod_all[:, 2 * l], dmod_sum[2 * l + 1][None, :], jnp.zeros((7, 3 * d_model), F32)])
        mine = lax.dynamic_slice_in_dim(rows, me * w_ada_cols, w_ada_cols, axis=1)
        grad_w_ada.append(_mm(act.T, mine, name=f"ada{l}_dw", tk=16))
        ds_ctx = ds_ctx + _mm(mine, w_ada[l], nt=True, name=f"ada{l}_da")[8]
    grad_w_ada = jnp.stack(grad_w_ada)

    def scattered(g, axis, name):
        parts = g.reshape(g.shape[:axis] + (N_DEV, g.shape[axis] // N_DEV) + g.shape[axis + 1:])
        parts = jnp.moveaxis(parts, axis, 0)
        local = parts.shape[1:]
        return _reduce_scatter(parts.reshape(N_DEV, -1, local[-1]), name=name).reshape(local)

    grad_w_in = scattered(_restore_w_in(jnp.stack([gl[l]["shadow_in"] for l in range(depth)])), 2, "rs_w_in")
    grad_w_out = scattered(jnp.stack([gl[l]["shadow_out"] for l in range(depth)]), 1, "rs_w_out")
    grad_w_glu = scattered(jnp.stack([gl[l]["shadow_glu"] for l in range(depth)]), 1, "rs_w_glu")

    stack = lambda k, shape=None: jnp.stack([gl[l][k] for l in range(depth)]).reshape(shape or given[k].shape)
    small = dict(norm_g=stack("norm_g"), hgrn_lb_logits=grads["lb_logits"], hgrn_norm_g=stack("hgrn_norm_g"),
                 gla_w_gk=stack("gla_w_gk", w_gk.shape), gla_b_gk=stack("gla_b_gk", b_gk.shape),
                 gla_norm_g=stack("gla_norm_g"), s5_d=stack("s5_d"), s5_b_glu=stack("s5_b_glu"),
                 ret_decay_logit=stack("ret_decay_logit"), ret_norm_g=stack("ret_norm_g"), final_norm_g=dg_final[0])
    for k in ("s5_lam_re", "s5_lam_im", "s5_log_dt", "s5_b_re", "s5_b_im", "s5_c_re", "s5_c_im"):
        small[k] = stack(k)
    pieces = [small[k] for k in _SMALL] + [ds_ctx]
    flat = _pad_rows(jnp.concatenate([a.reshape(-1) for a in pieces]))
    total = _sum8(_all_gather(flat, name="gather_small_grads"), name="sum_small_grads").reshape(-1)
    reduced, off = {}, 0
    for k, a in zip(_SMALL + ("ds_ctx",), pieces):
        reduced[k] = total[off:off + a.size].reshape(a.shape)
        off += a.size

    def my_cols(a):
        w = a.shape[-1] // N_DEV
        return lax.dynamic_slice_in_dim(a, me * w, w, axis=a.ndim - 1)

    dsilu = jax.vjp(_silu, c_ctx)[1](reduced.pop("ds_ctx"))[0]
    grad = dict(reduced, c_ctx=dsilu, w_ada=grad_w_ada, b_ada=grad_b_ada, w_in=grad_w_in, w_out=grad_w_out,
                s5_w_glu=grad_w_glu)
    for k in ("hgrn_lb_logits", "gla_w_gk", "gla_b_gk"):
        grad[k] = my_cols(grad[k])

    delta, new_m, new_v = {}, {}, {}
    for k in _WEIGHTS:
        delta[k], new_m[k], new_v[k] = _adamw(given[k], grad[k], given["m_" + k], given["v_" + k], name="adamw_" + k)
    grad_x = grads["h0"][n_ctx:][None]
    return (loss, grad_x, *[grad[k] for k in _WEIGHTS], *[delta[k] for k in _WEIGHTS],
            *[new_m[k] for k in _WEIGHTS], *[new_v[k] for k in _WEIGHTS])
```

```python
import functools
import math

import jax
import jax.numpy as jnp
from jax import lax
from jax.experimental import pallas as pl
from jax.experimental.pallas import tpu as pltpu

F32 = jnp.float32
BF16 = jnp.bfloat16

EPS = 1e-6
BRANCH_W = 1024
CHUNK = 64
HEAD_DK = 128
A_MIN_FORGET = 1e-6
B_GATE_NORM = 16.0
B_GATE_RANK = 16
C_GROUP = 16
C_GROUPS = 64
C_STATE = 64
C_MAX_RE = -1e-4
GRID_W = 64
ROPE_BASE = 10000.0
N_DEV = 8

ADAM_LR = 0.001
ADAM_B1 = 0.9
ADAM_B2 = 0.999
ADAM_EPS = 1e-08
ADAM_WD = 0.01
ADAM_STEP = 10

VMEM_LIMIT = 52 * 1024 * 1024
SCAN_ROWS = 256
S5_ROWS = 128
S5_GB = 8
S5_GBW = S5_GB * C_STATE


def _tile(dim, cands):
    for c in cands:
        if dim % c == 0:
            return c
    return dim


def _cparams(sem):
    return pltpu.CompilerParams(dimension_semantics=sem, vmem_limit_bytes=VMEM_LIMIT)


def _mm(a, b, *, nt=False, out_dtype=F32, name, tm=None, tn=None, tk=None):
    M, K = a.shape
    N = b.shape[0] if nt else b.shape[1]
    tm = tm or _tile(M, (1088, 1024, 512, 256, 128))
    tn = tn or _tile(N, (896, 1024, 640, 512, 256, 128))
    tk = tk or _tile(K, (2176, 1024, 896, 512, 256, 128))
    nk = K // tk
    dims = (((1,), (1,)), ((), ())) if nt else (((1,), (0,)), ((), ()))

    def body(a_ref, b_ref, o_ref, acc_ref):
        k = pl.program_id(2)

        @pl.when(k == 0)
        def _():
            acc_ref[...] = jnp.zeros_like(acc_ref)

        acc_ref[...] += lax.dot_general(a_ref[...].astype(BF16), b_ref[...].astype(BF16), dims,
                                        preferred_element_type=F32)

        @pl.when(k == nk - 1)
        def _():
            o_ref[...] = acc_ref[...].astype(o_ref.dtype)

    b_spec = (pl.BlockSpec((tn, tk), lambda i, j, k: (j, k)) if nt
              else pl.BlockSpec((tk, tn), lambda i, j, k: (k, j)))
    return pl.pallas_call(
        body, name=name, grid=(M // tm, N // tn, nk),
        in_specs=[pl.BlockSpec((tm, tk), lambda i, j, k: (i, k)), b_spec],
        out_specs=pl.BlockSpec((tm, tn), lambda i, j, k: (i, j)),
        out_shape=jax.ShapeDtypeStruct((M, N), out_dtype),
        scratch_shapes=[pltpu.VMEM((tm, tn), F32)],
        compiler_params=_cparams(("parallel", "parallel", "arbitrary")),
    )(a, b)


def _linear(name):
    @jax.custom_vjp
    def op(a, w, shadow):
        return _mm(a, w, name=name + "_fwd")

    def fwd(a, w, shadow):
        return _mm(a, w, name=name + "_fwd"), (a, w)

    def bwd(res, dy):
        a, w = res
        da = _mm(dy, w, nt=True, out_dtype=a.dtype, name=name + "_da")
        dw = _mm(a.T, dy, name=name + "_dw", tn=_tile(dy.shape[1], (640, 512, 256, 128)))
        return da, jnp.zeros_like(w), dw

    op.defvjp(fwd, bwd)
    return op


def _rowwise(fn, specs, out_defs, *, ncb, tr, n_ctx, name):
    ncx = n_ctx // tr
    diff = [k for k, s in enumerate(specs) if s in ("row", "const", "col", "seg")]

    def in_spec(s, arr):
        if s == "row":
            return pl.BlockSpec((tr, arr.shape[1] // ncb), lambda j, i: (i, j))
        if s == "rowc":
            return pl.BlockSpec((tr, arr.shape[1]), lambda j, i: (i, 0))
        if s in ("const", "fixed"):
            return pl.BlockSpec(arr.shape, lambda j, i: (0, 0))
        if s == "col":
            return pl.BlockSpec((None,) + arr.shape[1:], lambda j, i: (j, 0, 0))
        if s == "seg":
            return pl.BlockSpec((None,) + arr.shape[1:], lambda j, i: ((i >= ncx).astype(jnp.int32), 0, 0))
        raise ValueError(s)

    def fwd_call(*arrays):
        L = arrays[specs.index("row")].shape[0]
        n_in = len(arrays)

        def body(*refs):
            outs = fn(*[r[...] for r in refs[:n_in]])
            for o_ref, o in zip(refs[n_in:], outs):
                o_ref[...] = o.astype(o_ref.dtype)

        return pl.pallas_call(
            body, name=name + "_fwd", grid=(ncb, L // tr),
            in_specs=[in_spec(s, a) for s, a in zip(specs, arrays)],
            out_specs=[pl.BlockSpec((tr, c // ncb), lambda j, i: (i, j)) for c, _ in out_defs],
            out_shape=[jax.ShapeDtypeStruct((L, c), dt) for c, dt in out_defs],
            compiler_params=_cparams(("parallel", "arbitrary")),
        )(*arrays)

    def bwd_call(arrays, cts):
        L = arrays[specs.index("row")].shape[0]
        n_in, n_ct = len(arrays), len(cts)

        def body(*refs):
            i = pl.program_id(1)
            vals = [r[...] for r in refs[:n_in]]
            ct_vals = tuple(r[...] for r in refs[n_in:n_in + n_ct])
            g_refs = refs[n_in + n_ct:]

            def f(*dv):
                full = list(vals)
                for k, v in zip(diff, dv):
                    full[k] = v
                return tuple(o.astype(dt) for o, (_, dt) in zip(fn(*full), out_defs))

            _, vjp = jax.vjp(f, *[vals[k] for k in diff])
            grads = vjp(ct_vals)
            for k, g_ref, g in zip(diff, g_refs, grads):
                if specs[k] == "row":
                    g_ref[...] = g.astype(g_ref.dtype)
                else:
                    first = (i == 0) | (i == ncx) if specs[k] == "seg" else (i == 0)

                    @pl.when(first)
                    def _(g_ref=g_ref, g=g):
                        g_ref[...] = g

                    @pl.when(jnp.logical_not(first))
                    def _(g_ref=g_ref, g=g):
                        g_ref[...] += g

        out_specs, out_shape = [], []
        for k in diff:
            s, a = specs[k], arrays[k]
            if s == "row":
                out_specs.append(pl.BlockSpec((tr, a.shape[1] // ncb), lambda j, i: (i, j)))
                out_shape.append(jax.ShapeDtypeStruct(a.shape, a.dtype))
            elif s == "const":
                out_specs.append(pl.BlockSpec((None,) + a.shape, lambda j, i: (j, 0, 0)))
                out_shape.append(jax.ShapeDtypeStruct((ncb,) + a.shape, F32))
            elif s == "col":
                out_specs.append(pl.BlockSpec((None,) + a.shape[1:], lambda j, i: (j, 0, 0)))
                out_shape.append(jax.ShapeDtypeStruct(a.shape, F32))
            else:
                out_specs.append(pl.BlockSpec((None, None) + a.shape[1:],
                                              lambda j, i: (j, (i >= ncx).astype(jnp.int32), 0, 0)))
                out_shape.append(jax.ShapeDtypeStruct((ncb,) + a.shape, F32))
        ct_specs = [pl.BlockSpec((tr, c // ncb), lambda j, i: (i, j)) for c, _ in out_defs]
        grads = pl.pallas_call(
            body, name=name + "_bwd", grid=(ncb, L // tr),
            in_specs=[in_spec(s, a) for s, a in zip(specs, arrays)] + ct_specs,
            out_specs=out_specs, out_shape=out_shape,
            compiler_params=_cparams(("parallel", "arbitrary")),
        )(*arrays, *cts)
        full = [jnp.zeros_like(a) for a in arrays]
        for k, g in zip(diff, grads):
            full[k] = jnp.sum(g, axis=0) if specs[k] in ("const", "seg") else g
        return tuple(full)

    @jax.custom_vjp
    def op(*arrays):
        return tuple(fwd_call(*arrays))

    def op_fwd(*arrays):
        return tuple(fwd_call(*arrays)), arrays

    def op_bwd(arrays, cts):
        return bwd_call(arrays, cts)

    op.defvjp(op_fwd, op_bwd)
    return op


def _sigmoid(x):
    return 1.0 / (1.0 + jnp.exp(-x))


def _log_sigmoid(x):
    return jnp.minimum(x, 0.0) - jnp.log(1.0 + jnp.exp(-jnp.abs(x)))


def _silu(x):
    return x * _sigmoid(x)


def _rms(x, g):
    return x * lax.rsqrt(jnp.mean(x * x, axis=-1, keepdims=True) + EPS) * g


def _modnorm_fn(h, g, sc, sh):
    return ((_rms(h, g) * (1.0 + sc) + sh).astype(BF16),)


def _resid_fn(h, y, gt):
    return (h + gt * y,)


def _hgrn_pro_fn(zf, zb, lbf, lbb):
    def one(z, lb):
        f = lb + (1.0 - lb) * _sigmoid(z)
        return (1.0 - lb) * _sigmoid(-z), jnp.log(jnp.maximum(f, A_MIN_FORGET))

    kf, gf = one(zf, lbf)
    kb, gb = one(zb, lbb)
    return kf, gf, kb, gb


def _hgrn_epi_fn(of, ob, gate, g):
    return ((_rms(of + ob, g) * _silu(gate)).astype(BF16),)


def _gla_pro_fn(q, lr, wf, wb, bf, bb):
    gf = _log_sigmoid(jnp.dot(lr, wf, preferred_element_type=F32) + bf) / B_GATE_NORM
    gb = _log_sigmoid(jnp.dot(lr, wb, preferred_element_type=F32) + bb) / B_GATE_NORM
    return q * (HEAD_DK ** -0.5), gf, gb


def _ret_pro_fn(q, k, cos2, sin2, lgf, lgb, swap):
    def rope(x):
        return x * cos2 + jnp.dot(x, swap, preferred_element_type=F32, precision=lax.Precision.HIGHEST) * sin2

    zero = jnp.zeros_like(q)
    return rope(q) * (HEAD_DK ** -0.5), rope(k), zero + lgf, zero + lgb


def _ret_epi_fn(of, ob, gate, g):
    x = of + ob
    mu = jnp.mean(x, axis=-1, keepdims=True)
    var = jnp.mean(jnp.square(x - mu), axis=-1, keepdims=True)
    return (((x - mu) * lax.rsqrt(var + EPS) * g * _silu(gate)).astype(BF16),)


def _s5_mid_fn(yf, yb, u, d):
    return (jax.nn.gelu(yf + yb + d * u),)


def _s5_epi_fn(z, t, gate, b):
    return ((z * _sigmoid(t + b) * _silu(gate)).astype(BF16),)


def _blk(mode, i, nb, nc):
    if mode == 0:
        return i
    if mode == 1:
        return nb - 1 - i
    if mode == 2:
        return jnp.where(i < nc, nc - 1 - i, nb - 1 - i + nc)
    return jnp.where(i < nb - nc, i + nc, nc - nb + i)


def _split3(x):
    hi = x.astype(BF16)
    r1 = x - hi.astype(F32)
    mid = r1.astype(BF16)
    lo = (r1 - mid.astype(F32)).astype(BF16)
    return hi, mid, lo


def _cum(tri, x):
    hi, mid, lo = _split3(x)
    d = lambda v: jnp.dot(tri, v, preferred_element_type=F32)
    return d(hi) + d(mid) + d(lo)


def _dot(a, b):
    return jnp.dot(a.astype(BF16), b.astype(BF16), preferred_element_type=F32)


def _dot_nt(a, b):
    return lax.dot_general(a.astype(BF16), b.astype(BF16), (((1,), (1,)), ((), ())), preferred_element_type=F32)


def _dot_tn(a, b):
    return jnp.dot(a.astype(F32).T.astype(BF16), b.astype(BF16), preferred_element_type=F32)


def _dot3(a, b):
    ah = a.astype(BF16)
    al = (a - ah.astype(F32)).astype(BF16)
    bh = b.astype(BF16)
    bl = (b - bh.astype(F32)).astype(BF16)
    d = lambda p, q: jnp.dot(p, q, preferred_element_type=F32)
    return d(ah, bh) + d(ah, bl) + d(al, bh)


def _dot3_tn(a, b):
    return _dot3(a.astype(F32).T, b)


def _gla_masks(flip):
    r = lax.broadcasted_iota(jnp.int32, (CHUNK, CHUNK), 0)
    c = lax.broadcasted_iota(jnp.int32, (CHUNK, CHUNK), 1)
    causal = (c >= r) if flip else (c <= r)
    tri = causal.astype(BF16)
    tri_t = jnp.logical_not(causal) | (r == c)
    return causal, tri, tri_t.astype(BF16)


def _gla_chunk_terms(q, k, g, tri, flip):
    b = _cum(tri, g)
    mid = CHUNK // 2
    bm = b[mid:mid + 1, :]
    bl = b[0:1, :] if flip else b[CHUNK - 1:CHUNK, :]
    e_q = jnp.exp(b - bm)
    e_k = jnp.exp(bm - b)
    return b, bl, e_q, e_k, q * e_q, k * e_k


def _gla_fwd_call(q, k, v, g, *, flip, n_ctx, name):
    L = q.shape[0]
    H = q.shape[1] // HEAD_DK
    dv = v.shape[1] // H
    nb, nc, sub = L // SCAN_ROWS, n_ctx // SCAN_ROWS, SCAN_ROWS // CHUNK
    mode = 2 if flip else 0
    order = list(range(sub))[::-1] if flip else list(range(sub))

    def body(q_ref, k_ref, v_ref, g_ref, o_ref, sp_ref, st_ref):
        @pl.when(pl.program_id(1) == 0)
        def _():
            st_ref[...] = jnp.zeros_like(st_ref)

        causal, tri, _ = _gla_masks(flip)
        for s in order:
            sl = pl.ds(s * CHUNK, CHUNK)
            qc, kc, vc, gc = q_ref[sl, :], k_ref[sl, :], v_ref[sl, :], g_ref[sl, :]
            b, bl, _, _, qt, kt = _gla_chunk_terms(qc, kc, gc, tri, flip)
            a = jnp.where(causal, _dot_nt(qt, kt), 0.0)
            sp = st_ref[...]
            sp_ref[s] = sp
            o_ref[sl, :] = _dot(a, vc) + _dot_nt(qc * jnp.exp(b), sp)
            st_ref[...] = sp * jnp.exp(bl) + _dot3_tn(vc, kc * jnp.exp(bl - b))

    row = lambda w: pl.BlockSpec((SCAN_ROWS, w), lambda h, i: (_blk(mode, i, nb, nc), h))
    return pl.pallas_call(
        body, name=name, grid=(H, nb),
        in_specs=[row(HEAD_DK), row(HEAD_DK), row(dv), row(HEAD_DK)],
        out_specs=[row(dv), pl.BlockSpec((None, None, sub, dv, HEAD_DK),
                                         lambda h, i: (h, _blk(mode, i, nb, nc), 0, 0, 0))],
        out_shape=[jax.ShapeDtypeStruct((L, H * dv), F32),
                   jax.ShapeDtypeStruct((H, nb, sub, dv, HEAD_DK), F32)],
        scratch_shapes=[pltpu.VMEM((dv, HEAD_DK), F32)],
        compiler_params=_cparams(("parallel", "arbitrary")),
    )(q, k, v, g)


def _gla_bwd_call(q, k, v, g, sp, do, *, flip, n_ctx, name):
    L = q.shape[0]
    H = q.shape[1] // HEAD_DK
    dv = v.shape[1] // H
    nb, nc, sub = L // SCAN_ROWS, n_ctx // SCAN_ROWS, SCAN_ROWS // CHUNK
    mode = 3 if flip else 1
    order = list(range(sub)) if flip else list(range(sub))[::-1]
    last = 0 if flip else CHUNK - 1

    def body(q_ref, k_ref, v_ref, g_ref, sp_ref, do_ref, dq_ref, dk_ref, dv_ref, dg_ref, ds_ref):
        @pl.when(pl.program_id(1) == 0)
        def _():
            ds_ref[...] = jnp.zeros_like(ds_ref)

        causal, tri, tri_t = _gla_masks(flip)
        is_last = lax.broadcasted_iota(jnp.int32, (CHUNK, HEAD_DK), 0) == last
        for s in order:
            sl = pl.ds(s * CHUNK, CHUNK)
            qc, kc, vc, gc, doc = q_ref[sl, :], k_ref[sl, :], v_ref[sl, :], g_ref[sl, :], do_ref[sl, :]
            b, bl, e_q, e_k, qt, kt = _gla_chunk_terms(qc, kc, gc, tri, flip)
            e_b, e_l = jnp.exp(b), jnp.exp(bl - b)
            a = jnp.where(causal, _dot_nt(qt, kt), 0.0)
            p, kd = qc * e_b, kc * e_l
            spv, ds = sp_ref[s], ds_ref[...]
            dv_ref[sl, :] = _dot_tn(a, doc) + _dot_nt(kd, ds)
            da = jnp.where(causal, _dot_nt(doc, vc), 0.0)
            dkd = _dot3(vc, ds)
            dq = _dot3(da, kt) * e_q + _dot3(doc, spv) * e_b
            dk = _dot3_tn(da, qt) * e_k + dkd * e_l
            dq_ref[sl, :] = dq
            dk_ref[sl, :] = dk
            db_last = (jnp.sum(dkd * kd, axis=0, keepdims=True)
                       + jnp.exp(bl) * jnp.sum(ds * spv, axis=0, keepdims=True))
            db = qc * dq - kc * dk + jnp.where(is_last, db_last, 0.0)
            dg_ref[sl, :] = _cum(tri_t, db)
            ds_ref[...] = ds * jnp.exp(bl) + _dot3_tn(doc, p)

    row = lambda w: pl.BlockSpec((SCAN_ROWS, w), lambda h, i: (_blk(mode, i, nb, nc), h))
    return pl.pallas_call(
        body, name=name, grid=(H, nb),
        in_specs=[row(HEAD_DK), row(HEAD_DK), row(dv), row(HEAD_DK),
                  pl.BlockSpec((None, None, sub, dv, HEAD_DK), lambda h, i: (h, _blk(mode, i, nb, nc), 0, 0, 0)),
                  row(dv)],
        out_specs=[row(HEAD_DK), row(HEAD_DK), row(dv), row(HEAD_DK)],
        out_shape=[jax.ShapeDtypeStruct(q.shape, F32), jax.ShapeDtypeStruct(k.shape, F32),
                   jax.ShapeDtypeStruct(v.shape, F32), jax.ShapeDtypeStruct(g.shape, F32)],
        scratch_shapes=[pltpu.VMEM((dv, HEAD_DK), F32)],
        compiler_params=_cparams(("parallel", "arbitrary")),
    )(q, k, v, g, sp, do)


def _gla(name, *, flip, n_ctx):
    @jax.custom_vjp
    def op(q, k, v, g):
        return _gla_fwd_call(q, k, v, g, flip=flip, n_ctx=n_ctx, name=name + "_fwd")[0]

    def fwd(q, k, v, g):
        o, sp = _gla_fwd_call(q, k, v, g, flip=flip, n_ctx=n_ctx, name=name + "_fwd")
        return o, (q, k, v, g, sp)

    def bwd(res, do):
        return tuple(_gla_bwd_call(*res, do, flip=flip, n_ctx=n_ctx, name=name + "_bwd"))

    op.defvjp(fwd, bwd)
    return op


def _s5_table(ar, ai, desc):
    def mul(p, q):
        return p[0] * q[0] - p[1] * q[1], p[0] * q[1] + p[1] * q[0]

    a1 = (ar.reshape(-1), ai.reshape(-1))
    ar = a1[0]
    a2 = mul(a1, a1)
    a4 = mul(a2, a2)
    pw = [a1]
    for _ in range(7):
        pw.append(mul(pw[-1], a1))
    if desc:
        pw = pw[::-1]
    zero = jnp.zeros_like(ar)
    rows = [a1[0], a1[1], a2[0], a2[1], a4[0], a4[1], zero, zero] + [p[0] for p in pw] + [p[1] for p in pw]
    return jnp.stack(rows, axis=0)


S5_LANES = C_GROUPS * C_STATE
S5_LC = 1024


def _s5_scan(xr_ref, xi_ref, tab_ref, car_ref, desc, on_slab=None):
    nslab = S5_ROWS // 8
    rows = lax.broadcasted_iota(jnp.int32, (8, S5_LC), 0)

    def slab(t, carry):
        r0 = pl.multiple_of(((nslab - 1 - t) if desc else t) * 8, 8)
        for lc in range(S5_LANES // S5_LC):
            cs = pl.ds(lc * S5_LC, S5_LC)
            xr, xi = xr_ref[pl.ds(r0, 8), cs], xi_ref[pl.ds(r0, 8), cs]
            for n, s in enumerate((1, 2, 4)):
                ar, ai = tab_ref[2 * n:2 * n + 1, cs], tab_ref[2 * n + 1:2 * n + 2, cs]
                keep = (rows < 8 - s) if desc else (rows >= s)
                sr = jnp.where(keep, pltpu.roll(xr, (8 - s) if desc else s, 0), 0.0)
                si = jnp.where(keep, pltpu.roll(xi, (8 - s) if desc else s, 0), 0.0)
                xr, xi = xr + ar * sr - ai * si, xi + ar * si + ai * sr
            cr, ci = car_ref[0:1, cs], car_ref[1:2, cs]
            pr, pi = tab_ref[8:16, cs], tab_ref[16:24, cs]
            xr, xi = xr + pr * cr - pi * ci, xi + pr * ci + pi * cr
            if on_slab is not None:
                on_slab(r0, cs, xr, xi, cr, ci, rows)
            xr_ref[pl.ds(r0, 8), cs] = xr
            xi_ref[pl.ds(r0, 8), cs] = xi
            e = 0 if desc else 7
            car_ref[0:1, cs] = xr[e:e + 1, :]
            car_ref[1:2, cs] = xi[e:e + 1, :]
        return carry

    lax.fori_loop(0, nslab, slab, 0)


def _s5_fwd_call(u, tab, wb_re, wb_im, wc_re, wc_im, *, flip, n_ctx, name):
    L = u.shape[0]
    nb, nc = L // S5_ROWS, n_ctx // S5_ROWS
    mode, desc = (2, True) if flip else (0, False)

    def body(u_ref, tab_ref, wbr_ref, wbi_ref, wcr_ref, wci_ref, y_ref, xr_ref, xi_ref, car_ref):
        @pl.when(pl.program_id(0) == 0)
        def _():
            car_ref[...] = jnp.zeros_like(car_ref)

        for gb in range(S5_GB):
            us, cs = pl.ds(gb * 128, 128), pl.ds(gb * S5_GBW, S5_GBW)
            ub = u_ref[:, us]
            xr_ref[:, cs] = _dot(ub, wbr_ref[gb])
            xi_ref[:, cs] = _dot(ub, wbi_ref[gb])
        _s5_scan(xr_ref, xi_ref, tab_ref, car_ref, desc)
        for gb in range(S5_GB):
            us, cs = pl.ds(gb * 128, 128), pl.ds(gb * S5_GBW, S5_GBW)
            y_ref[:, us] = _dot(xr_ref[:, cs], wcr_ref[gb]) - _dot(xi_ref[:, cs], wci_ref[gb])

    row = lambda w: pl.BlockSpec((S5_ROWS, w), lambda i: (_blk(mode, i, nb, nc), 0))
    full = lambda a: pl.BlockSpec(a.shape, lambda i: (0,) * a.ndim)
    return pl.pallas_call(
        body, name=name, grid=(nb,),
        in_specs=[row(BRANCH_W), full(tab), full(wb_re), full(wb_im), full(wc_re), full(wc_im)],
        out_specs=[row(BRANCH_W), row(S5_LANES), row(S5_LANES)],
        out_shape=[jax.ShapeDtypeStruct((L, BRANCH_W), F32), jax.ShapeDtypeStruct((L, S5_LANES), F32),
                   jax.ShapeDtypeStruct((L, S5_LANES), F32)],
        scratch_shapes=[pltpu.VMEM((8, S5_LANES), F32)],
        compiler_params=_cparams(("arbitrary",)),
    )(u, tab, wb_re, wb_im, wc_re, wc_im)


def _s5_bwd_call(dy, u, xr, xi, tab, wbt_re, wbt_im, wct_re, wct_im, *, flip, n_ctx, name):
    L = u.shape[0]
    nb, nc = L // S5_ROWS, n_ctx // S5_ROWS
    mode, desc = (3, False) if flip else (1, True)

    def body(dy_ref, u_ref, xr_ref, xi_ref, tab_ref, wbtr_ref, wbti_ref, wctr_ref, wcti_ref,
             du_ref, dwbr_ref, dwbi_ref, dwcr_ref, dwci_ref, ga_ref, gr_ref, gi_ref, car_ref):
        @pl.when(pl.program_id(0) == 0)
        def _():
            car_ref[...] = jnp.zeros_like(car_ref)
            ga_ref[...] = jnp.zeros_like(ga_ref)
            for r in (dwbr_ref, dwbi_ref, dwcr_ref, dwci_ref):
                r[...] = jnp.zeros_like(r)

        for gb in range(S5_GB):
            us, cs = pl.ds(gb * 128, 128), pl.ds(gb * S5_GBW, S5_GBW)
            dyb = dy_ref[:, us]
            gr_ref[:, cs] = _dot(dyb, wctr_ref[gb])
            gi_ref[:, cs] = -_dot(dyb, wcti_ref[gb])

        def lam_grad(r0, cs, gr, gi, cr, ci, rows):
            edge = 7 if desc else 0
            pr = jnp.where(rows == edge, cr, pltpu.roll(gr, 7 if desc else 1, 0))
            pi = jnp.where(rows == edge, ci, pltpu.roll(gi, 7 if desc else 1, 0))
            x_r, x_i = xr_ref[pl.ds(r0, 8), cs], xi_ref[pl.ds(r0, 8), cs]
            ga_ref[0:8, cs] += x_r * pr + x_i * pi
            ga_ref[8:16, cs] += x_r * pi - x_i * pr

        _s5_scan(gr_ref, gi_ref, tab_ref, car_ref, desc, on_slab=lam_grad)
        for gb in range(S5_GB):
            us, cs = pl.ds(gb * 128, 128), pl.ds(gb * S5_GBW, S5_GBW)
            g_r, g_i, ub, dyb = gr_ref[:, cs], gi_ref[:, cs], u_ref[:, us], dy_ref[:, us]
            du_ref[:, us] = _dot(g_r, wbtr_ref[gb]) + _dot(g_i, wbti_ref[gb])
            dwbr_ref[gb] += _dot_tn(ub, g_r)
            dwbi_ref[gb] += _dot_tn(ub, g_i)
            dwcr_ref[gb] += _dot_tn(xr_ref[:, cs], dyb)
            dwci_ref[gb] -= _dot_tn(xi_ref[:, cs], dyb)

    row = lambda w: pl.BlockSpec((S5_ROWS, w), lambda i: (_blk(mode, i, nb, nc), 0))
    full = lambda shape: pl.BlockSpec(shape, lambda i: (0,) * len(shape))
    wb_shape, wc_shape = (S5_GB, 128, S5_GBW), (S5_GB, S5_GBW, 128)
    return pl.pallas_call(
        body, name=name, grid=(nb,),
        in_specs=[row(BRANCH_W), row(BRANCH_W), row(S5_LANES), row(S5_LANES), full(tab.shape),
                  full(wc_shape), full(wc_shape), full(wb_shape), full(wb_shape)],
        out_specs=[row(BRANCH_W), full(wb_shape), full(wb_shape), full(wc_shape), full(wc_shape),
                   full((16, S5_LANES))],
        out_shape=[jax.ShapeDtypeStruct((L, BRANCH_W), F32),
                   jax.ShapeDtypeStruct(wb_shape, F32), jax.ShapeDtypeStruct(wb_shape, F32),
                   jax.ShapeDtypeStruct(wc_shape, F32), jax.ShapeDtypeStruct(wc_shape, F32),
                   jax.ShapeDtypeStruct((16, S5_LANES), F32)],
        scratch_shapes=[pltpu.VMEM((S5_ROWS, S5_LANES), F32), pltpu.VMEM((S5_ROWS, S5_LANES), F32),
                        pltpu.VMEM((8, S5_LANES), F32)],
        compiler_params=_cparams(("arbitrary",)),
    )(dy, u, xr, xi, tab, wbt_re, wbt_im, wct_re, wct_im)


def _s5(name, *, flip, n_ctx):
    desc_f = flip
    desc_b = not flip

    def run_fwd(u, lr, li, wbr, wbi, wcr, wci):
        tab = _s5_table(lr, li, desc_f)
        c16 = lambda a: a.astype(BF16)
        return _s5_fwd_call(u, tab, c16(wbr), c16(wbi), c16(wcr), c16(wci), flip=flip, n_ctx=n_ctx, name=name + "_fwd")

    @jax.custom_vjp
    def op(u, lr, li, wbr, wbi, wcr, wci):
        return run_fwd(u, lr, li, wbr, wbi, wcr, wci)[0]

    def fwd(u, lr, li, wbr, wbi, wcr, wci):
        y, xr, xi = run_fwd(u, lr, li, wbr, wbi, wcr, wci)
        return y, (u, xr, xi, lr, li, wbr, wbi, wcr, wci)

    def bwd(res, dy):
        u, xr, xi, lr, li, wbr, wbi, wcr, wci = res
        tab = _s5_table(lr, -li, desc_b)
        t16 = lambda a: jnp.swapaxes(a, 1, 2).astype(BF16)
        du, dwbr, dwbi, dwcr, dwci, ga = _s5_bwd_call(
            dy, u, xr, xi, tab, t16(wbr), t16(wbi), t16(wcr), t16(wci), flip=flip, n_ctx=n_ctx, name=name + "_bwd")
        return (du, jnp.sum(ga[0:8], axis=0, keepdims=True), jnp.sum(ga[8:16], axis=0, keepdims=True),
                dwbr, dwbi, dwcr, dwci)

    op.defvjp(fwd, bwd)
    return op


def _loss_call(h, tgt, g, *, n_ctx, name):
    L, D = h.shape
    tr = 128
    ncx = n_ctx // tr

    def body(h_ref, t_ref, g_ref, dh_ref, dg_ref, loss_ref):
        i = pl.program_id(0)

        @pl.when(i == 0)
        def _():
            dg_ref[...] = jnp.zeros_like(dg_ref)
            loss_ref[...] = jnp.zeros_like(loss_ref)

        @pl.when(i < ncx)
        def _():
            dh_ref[...] = jnp.zeros_like(dh_ref)

        @pl.when(i >= ncx)
        def _():
            t = t_ref[...]

            def f(hv, gv):
                e = _rms(hv, gv) - t
                return 0.5 * jnp.sum(e * e) / D

            l, (dh, dg) = jax.value_and_grad(f, argnums=(0, 1))(h_ref[...], g_ref[...])
            dh_ref[...] = dh
            dg_ref[...] += dg
            loss_ref[...] += l

    return pl.pallas_call(
        body, name=name, grid=(L // tr,),
        in_specs=[pl.BlockSpec((tr, D), lambda i: (i, 0)),
                  pl.BlockSpec((tr, D), lambda i: (jnp.maximum(i - ncx, 0), 0)),
                  pl.BlockSpec((1, D), lambda i: (0, 0))],
        out_specs=[pl.BlockSpec((tr, D), lambda i: (i, 0)), pl.BlockSpec((1, D), lambda i: (0, 0)),
                   pl.BlockSpec((8, 128), lambda i: (0, 0))],
        out_shape=[jax.ShapeDtypeStruct((L, D), F32), jax.ShapeDtypeStruct((1, D), F32),
                   jax.ShapeDtypeStruct((8, 128), F32)],
        compiler_params=_cparams(("arbitrary",)),
    )(h, tgt, g)


def _adamw_call(w, g, m, v, *, name):
    R, C = w.shape
    tr = R if R * C * 4 <= (1 << 20) else _tile(R, tuple(t for t in (512, 256, 128, 64, 32, 16, 8) if t * C * 4 <= (1 << 20)))
    c1 = 1.0 / (1.0 - ADAM_B1 ** ADAM_STEP)
    c2 = 1.0 / (1.0 - ADAM_B2 ** ADAM_STEP)

    def body(w_ref, g_ref, m_ref, v_ref, d_ref, nm_ref, nv_ref):
        gv = g_ref[...]
        nm = ADAM_B1 * m_ref[...] + (1.0 - ADAM_B1) * gv
        nv = ADAM_B2 * v_ref[...] + (1.0 - ADAM_B2) * gv * gv
        d_ref[...] = -ADAM_LR * ((nm * c1) / (jnp.sqrt(nv * c2) + ADAM_EPS) + ADAM_WD * w_ref[...])
        nm_ref[...] = nm
        nv_ref[...] = nv

    spec = pl.BlockSpec((tr, C), lambda i: (i, 0))
    return pl.pallas_call(
        body, name=name, grid=(R // tr,), in_specs=[spec] * 4, out_specs=[spec] * 3,
        out_shape=[jax.ShapeDtypeStruct((R, C), F32)] * 3,
        compiler_params=_cparams(("parallel",)),
    )(w, g, m, v)


def _as2d(a):
    n = a.size
    if a.ndim >= 2 and n >= (1 << 20):
        return a.reshape(-1, a.shape[-1])
    if n % 128 == 0:
        return a.reshape(n // 128, 128)
    return a.reshape(1, n)


def _adamw(w, g, m, v, *, name):
    d, nm, nv = _adamw_call(_as2d(w), _as2d(g.astype(F32)), _as2d(m), _as2d(v), name=name)
    return d.reshape(w.shape), nm.reshape(w.shape), nv.reshape(w.shape)


MESH = pl.DeviceIdType.MESH
ANY = pl.BlockSpec(memory_space=pl.ANY)


def _place():
    return lax.axis_index("x"), lax.axis_index("y"), lax.axis_index("c")


def _all_gather(x, *, name):
    M, N = x.shape

    def body(x_ref, out_ref, send_sems, recv_sems, local_sem):
        x_, y_, c_ = _place()
        me, sibling = (x_, y_, c_), (x_, y_, 1 - c_)
        chips = [(1 - x_, y_), (x_, 1 - y_), (1 - x_, 1 - y_)]

        def slab(px, py, pc):
            return out_ref.at[4 * px + 2 * py + pc]

        def copy(k, block, to, src=None):
            return pltpu.make_async_remote_copy(
                src_ref=slab(*block) if src is None else src, dst_ref=slab(*block),
                send_sem=send_sems.at[k], recv_sem=recv_sems.at[k], device_id=to, device_id_type=MESH)

        mine = pltpu.make_async_copy(x_ref, slab(*me), local_sem)
        mine.start()
        first = [copy(0, me, sibling, src=x_ref)]
        first += [copy(1 + j, me, (*chip, c_), src=x_ref) for j, chip in enumerate(chips)]
        for cp in first:
            cp.start()
        passed = [copy(4 + j, (*chip, c_), sibling) for j, chip in enumerate(chips)]
        for j, chip in enumerate(chips):
            copy(1 + j, (*chip, c_), me).wait_recv()
            passed[j].start()
        copy(0, sibling, me).wait_recv()
        for j, chip in enumerate(chips):
            copy(4 + j, (*chip, 1 - c_), me).wait_recv()
        for cp in first + passed:
            cp.wait_send()
        mine.wait()

    return pl.pallas_call(
        body, name=name, out_shape=jax.ShapeDtypeStruct((N_DEV, M, N), x.dtype),
        in_specs=[ANY], out_specs=ANY,
        scratch_shapes=[pltpu.SemaphoreType.DMA((7,)), pltpu.SemaphoreType.DMA((7,)), pltpu.SemaphoreType.DMA],
    )(x)


def _pair_exchange(g, *, name):
    _, R, C = g.shape

    def body(g_ref, out_ref, send_sems, recv_sems):
        x_, y_, c_ = _place()
        copies = []
        for k in range(4):
            cp = pltpu.make_async_remote_copy(
                src_ref=g_ref.at[2 * k + (1 - c_)], dst_ref=out_ref.at[k], send_sem=send_sems.at[k],
                recv_sem=recv_sems.at[k], device_id=(x_, y_, 1 - c_), device_id_type=MESH)
            cp.start()
            copies.append(cp)
        for cp in copies:
            cp.wait()

    return pl.pallas_call(
        body, name=name, out_shape=jax.ShapeDtypeStruct((4, R, C), g.dtype), in_specs=[ANY], out_specs=ANY,
        scratch_shapes=[pltpu.SemaphoreType.DMA((4,)), pltpu.SemaphoreType.DMA((4,))],
    )(g)


def _chip_exchange(p, *, name):
    _, R, C = p.shape

    def body(p_ref, out_ref, send_sems, recv_sems):
        x_, y_, c_ = _place()
        chips = [(1 - x_, y_), (x_, 1 - y_), (1 - x_, 1 - y_)]
        copies = []
        for j, (cx, cy) in enumerate(chips):
            cp = pltpu.make_async_remote_copy(
                src_ref=p_ref.at[2 * cx + cy], dst_ref=out_ref.at[j], send_sem=send_sems.at[j],
                recv_sem=recv_sems.at[j], device_id=(cx, cy, c_), device_id_type=MESH)
            cp.start()
            copies.append(cp)
        for cp in copies:
            cp.wait()

    return pl.pallas_call(
        body, name=name, out_shape=jax.ShapeDtypeStruct((3, R, C), p.dtype), in_specs=[ANY], out_specs=ANY,
        scratch_shapes=[pltpu.SemaphoreType.DMA((3,)), pltpu.SemaphoreType.DMA((3,))],
    )(p)


def _add_tr(R, C):
    return _tile(R, tuple(t for t in (512, 256, 128, 64, 32, 16, 8) if t * C * 4 <= (2 << 20)))


def _pair_add(g, recv, *, name):
    _, R, C = g.shape
    tr = _add_tr(R, C)
    core = lax.axis_index("c").astype(jnp.int32).reshape(1)

    def body(c_ref, g_ref, r_ref, o_ref):
        o_ref[...] = (g_ref[...] + r_ref[...]).astype(o_ref.dtype)

    return pl.pallas_call(
        body, name=name, out_shape=jax.ShapeDtypeStruct((4, R, C), BF16),
        grid_spec=pltpu.PrefetchScalarGridSpec(
            num_scalar_prefetch=1, grid=(4, R // tr),
            in_specs=[pl.BlockSpec((None, tr, C), lambda k, i, c: (2 * k + c[0], i, 0)),
                      pl.BlockSpec((None, tr, C), lambda k, i, c: (k, i, 0))],
            out_specs=pl.BlockSpec((None, tr, C), lambda k, i, c: (k, i, 0))),
        compiler_params=_cparams(("parallel", "parallel")),
    )(core, g, recv)


def _chip_add(p, recv, *, name):
    _, R, C = p.shape
    tr = _add_tr(R, C)
    chip = (2 * lax.axis_index("x") + lax.axis_index("y")).astype(jnp.int32).reshape(1)

    def body(c_ref, p_ref, r0_ref, r1_ref, r2_ref, o_ref):
        f = lambda r: r[...].astype(F32)
        o_ref[...] = ((f(p_ref) + f(r0_ref)) + f(r1_ref)) + f(r2_ref)

    rspec = lambda j: pl.BlockSpec((None, tr, C), lambda i, c: (j, i, 0))
    return pl.pallas_call(
        body, name=name, out_shape=jax.ShapeDtypeStruct((R, C), F32),
        grid_spec=pltpu.PrefetchScalarGridSpec(
            num_scalar_prefetch=1, grid=(R // tr,),
            in_specs=[pl.BlockSpec((None, tr, C), lambda i, c: (c[0], i, 0)), rspec(0), rspec(1), rspec(2)],
            out_specs=pl.BlockSpec((tr, C), lambda i, c: (i, 0))),
        compiler_params=_cparams(("parallel",)),
    )(chip, p, recv, recv, recv)


def _reduce_scatter(g, *, name):
    p = _pair_add(g, _pair_exchange(g, name=name + "_pair"), name=name + "_pair_add")
    return _chip_add(p, _chip_exchange(p, name=name + "_chip"), name=name + "_chip_add")


def _sum8(a, *, name):
    _, R, C = a.shape
    tr = _tile(R, tuple(t for t in (512, 256, 128, 64, 32, 16, 8) if t * C * 4 * 8 <= (4 << 20)))

    def body(a_ref, o_ref):
        acc = a_ref[0]
        for d in range(1, N_DEV):
            acc = acc + a_ref[d]
        o_ref[...] = acc

    return pl.pallas_call(
        body, name=name, grid=(R // tr,), in_specs=[pl.BlockSpec((N_DEV, tr, C), lambda i: (0, i, 0))],
        out_specs=pl.BlockSpec((tr, C), lambda i: (i, 0)), out_shape=jax.ShapeDtypeStruct((R, C), F32),
        compiler_params=_cparams(("parallel",)),
    )(a)


def _s5_params(lam_re, lam_im, log_dt, b_re, b_im, c_re, c_im):
    lr = jnp.minimum(lam_re, C_MAX_RE)
    li = lam_im
    dt = jnp.exp(log_dt)[:, None]
    mag = jnp.exp(lr * dt)
    ar, ai = mag * jnp.cos(li * dt), mag * jnp.sin(li * dt)
    nr, ni, den = ar - 1.0, ai, lr * lr + li * li
    fr, fi = (nr * lr + ni * li) / den, (ni * lr - nr * li) / den
    bbr = fr[..., None] * b_re - fi[..., None] * b_im
    bbi = fr[..., None] * b_im + fi[..., None] * b_re
    eye = jnp.eye(S5_GB, dtype=F32)

    def dense_b(bb):
        t = bb.reshape(S5_GB, S5_GB, C_STATE, C_GROUP).transpose(0, 1, 3, 2)
        return (t[:, :, :, None, :] * eye[None, :, None, :, None]).reshape(S5_GB, S5_GB * C_GROUP, S5_GBW)

    def dense_c(cc):
        t = cc.reshape(S5_GB, S5_GB, C_GROUP, C_STATE).transpose(0, 1, 3, 2)
        return (t[:, :, :, None, :] * eye[None, :, None, :, None]).reshape(S5_GB, S5_GBW, S5_GB * C_GROUP)

    flat = lambda a: a.reshape(1, S5_LANES)
    return flat(ar), flat(ai), dense_b(bbr), dense_b(bbi), dense_c(c_re), dense_c(c_im)


def _rope_tables(n_ctx, n_lat):
    quarter = HEAD_DK // 4
    freqs = ROPE_BASE ** (-jnp.arange(quarter, dtype=F32) / quarter)
    t = jnp.arange(n_lat)
    r = (t // GRID_W).astype(F32)
    col = (t % GRID_W).astype(F32)
    ang = jnp.concatenate([r[:, None] * freqs, col[:, None] * freqs], axis=-1)
    ang = jnp.concatenate([jnp.zeros((n_ctx, HEAD_DK // 2), F32), ang], axis=0)
    cos, sin = jnp.cos(ang), jnp.sin(ang)
    idx = jnp.arange(HEAD_DK)
    swap = (idx[:, None] == (idx[None, :] + HEAD_DK // 2) % HEAD_DK).astype(F32)
    return jnp.concatenate([cos, cos], axis=1), jnp.concatenate([-sin, sin], axis=1), swap


W_IN = 13344
W_INP = 13440
_COL = dict(a_q=0, a_ff=1024, a_fb=2048, a_i=3072, a_g=4096, b_q=5120, b_k=5632, b_v=6144, b_g=7168,
            c_u=8192, c_g=9216, d_q=10240, d_k=10752, d_v=11264, d_g=12288, b_lr=13312)


_COL_ORDER = (("a_q", 1024), ("a_ff", 1024), ("a_fb", 1024), ("a_i", 1024), ("a_g", 1024), ("b_q", 512), ("b_k", 512),
              ("b_v", 1024), ("b_g", 1024), ("c_u", 1024), ("c_g", 1024), ("d_q", 512), ("d_k", 512), ("d_v", 1024),
              ("d_g", 1024), ("b_lr", 128))
_W_IN_MAP = ((0, 7168, 0), (7200, W_IN, 7168), (7168, 7200, 13312))
W_IN_SHARD = W_IN // N_DEV


def _w_in_from_shards(g):
    parts = []
    for lo, hi, _ in _W_IN_MAP:
        for d in range(N_DEV):
            a, b = max(lo, d * W_IN_SHARD), min(hi, (d + 1) * W_IN_SHARD)
            if a < b:
                parts.append(g[d, ..., a - d * W_IN_SHARD:b - d * W_IN_SHARD])
    parts.append(jnp.zeros(g.shape[1:-1] + (W_INP - W_IN,), g.dtype))
    return jnp.concatenate(parts, axis=-1)


def _w_in_to_shards(w):
    shards = []
    for d in range(N_DEV):
        parts = []
        for lo, hi, here in sorted(_W_IN_MAP):
            a, b = max(lo, d * W_IN_SHARD), min(hi, (d + 1) * W_IN_SHARD)
            if a < b:
                parts.append(w[..., here + a - lo:here + b - lo])
        shards.append(jnp.concatenate(parts, axis=-1))
    return jnp.stack(shards)


@jax.custom_vjp
def _split_cols(proj):
    return tuple(lax.slice_in_dim(proj, _COL[n], _COL[n] + w, axis=1) for n, w in _COL_ORDER)


def _split_cols_fwd(proj):
    return _split_cols(proj), None


def _split_cols_bwd(_, cts):
    return (jnp.concatenate(cts, axis=1),)


_split_cols.defvjp(_split_cols_fwd, _split_cols_bwd)


def _layer_ops(l, n_ctx):
    n = f"l{l}_"
    rw = functools.partial(_rowwise, n_ctx=n_ctx)
    wide = [(BRANCH_W, F32)]
    return dict(
        modnorm=rw(_modnorm_fn, ["row", "const", "seg", "seg"], [(4096, BF16)], ncb=1, tr=128, name=n + "modnorm"),
        resid=rw(_resid_fn, ["row", "row", "seg"], [(4096, F32)], ncb=1, tr=128, name=n + "resid"),
        lin_in=_linear(n + "lin_in"), lin_out=_linear(n + "lin_out"), lin_glu=_linear(n + "lin_glu"),
        hgrn_pro=rw(_hgrn_pro_fn, ["row", "row", "const", "const"], wide * 4, ncb=1, tr=256, name=n + "hgrn_pro"),
        hgrn_f=_gla(n + "hgrn_f", flip=False, n_ctx=n_ctx), hgrn_b=_gla(n + "hgrn_b", flip=True, n_ctx=n_ctx),
        hgrn_epi=rw(_hgrn_epi_fn, ["row", "row", "row", "const"], [(BRANCH_W, BF16)], ncb=8, tr=256, name=n + "hgrn_epi"),
        gla_pro=rw(_gla_pro_fn, ["row", "row", "const", "const", "const", "const"], [(512, F32)] * 3, ncb=1, tr=256,
                   name=n + "gla_pro"),
        gla_f=_gla(n + "gla_f", flip=False, n_ctx=n_ctx), gla_b=_gla(n + "gla_b", flip=True, n_ctx=n_ctx),
        gla_epi=rw(_hgrn_epi_fn, ["row", "row", "row", "const"], [(BRANCH_W, BF16)], ncb=4, tr=256, name=n + "gla_epi"),
        s5_f=_s5(n + "s5_f", flip=False, n_ctx=n_ctx), s5_b=_s5(n + "s5_b", flip=True, n_ctx=n_ctx),
        s5_mid=rw(_s5_mid_fn, ["row", "row", "row", "const"], wide, ncb=1, tr=256, name=n + "s5_mid"),
        s5_epi=rw(_s5_epi_fn, ["row", "row", "row", "const"], [(BRANCH_W, BF16)], ncb=1, tr=256, name=n + "s5_epi"),
        ret_pro=rw(_ret_pro_fn, ["row", "row", "rowc", "rowc", "col", "col", "fixed"], [(512, F32)] * 4, ncb=4, tr=256,
                   name=n + "ret_pro"),
        ret_f=_gla(n + "ret_f", flip=False, n_ctx=n_ctx), ret_b=_gla(n + "ret_b", flip=True, n_ctx=n_ctx),
        ret_epi=rw(_ret_epi_fn, ["row", "row", "row", "const"], [(BRANCH_W, BF16)], ncb=4, tr=256, name=n + "ret_epi"),
    )


def _layer(ops, h, p, w_in, w_out, w_glu, lb, rope):
    cos2, sin2, swap = rope
    (hn,) = ops["modnorm"](h, p["norm_g"], p["sc"], p["sh"])
    proj = ops["lin_in"](hn, w_in, p["shadow_in"])
    cols = dict(zip([n for n, _ in _COL_ORDER], _split_cols(proj)))
    col = lambda name, w: cols[name]

    a_q, a_i = col("a_q", 1024), col("a_i", 1024)
    kf, gf, kb, gb = ops["hgrn_pro"](col("a_ff", 1024), col("a_fb", 1024), lb[0:1], lb[1:2])
    (o_a,) = ops["hgrn_epi"](ops["hgrn_f"](a_q, kf, a_i, gf), ops["hgrn_b"](a_q, kb, a_i, gb), col("a_g", 1024),
                             p["hgrn_norm_g"])

    rank = B_GATE_RANK
    wf = jnp.pad(p["gla_w_gk"][0], ((0, 128 - rank), (0, 0)))
    wb = jnp.pad(p["gla_w_gk"][1], ((rank, 128 - 2 * rank), (0, 0)))
    b_k, b_v = col("b_k", 512), col("b_v", 1024)
    qs, gf, gb = ops["gla_pro"](col("b_q", 512), col("b_lr", 128), wf, wb, p["gla_b_gk"][0:1], p["gla_b_gk"][1:2])
    (o_b,) = ops["gla_epi"](ops["gla_f"](qs, b_k, b_v, gf), ops["gla_b"](qs, b_k, b_v, gb), col("b_g", 1024),
                            p["gla_norm_g"])

    c_u = col("c_u", 1024)
    s5p = lambda d: _s5_params(*[p[k][d] for k in ("s5_lam_re", "s5_lam_im", "s5_log_dt", "s5_b_re", "s5_b_im",
                                                  "s5_c_re", "s5_c_im")])
    (z,) = ops["s5_mid"](ops["s5_f"](c_u, *s5p(0)), ops["s5_b"](c_u, *s5p(1)), c_u, p["s5_d"].reshape(1, BRANCH_W))
    t = ops["lin_glu"](z, w_glu, p["shadow_glu"])
    (o_c,) = ops["s5_epi"](z, t, col("c_g", 1024), p["s5_b_glu"])

    log_gamma = _log_sigmoid(p["ret_decay_logit"])
    lg = lambda d: jnp.broadcast_to(log_gamma[d][:, None, None], (4, 1, HEAD_DK))
    d_v = col("d_v", 1024)
    qs, ks, gf, gb = ops["ret_pro"](col("d_q", 512), col("d_k", 512), cos2, sin2, lg(0), lg(1), swap)
    (o_d,) = ops["ret_epi"](ops["ret_f"](qs, ks, d_v, gf), ops["ret_b"](qs, ks, d_v, gb), col("d_g", 1024),
                            p["ret_norm_g"])

    o = jnp.concatenate([o_a, o_b, o_c, o_d], axis=1)
    y = ops["lin_out"](o, w_out, p["shadow_out"])
    (h,) = ops["resid"](h, y, p["gt"])
    return h


_SMALL = ("norm_g", "hgrn_lb_logits", "hgrn_norm_g", "gla_w_gk", "gla_b_gk", "gla_norm_g", "s5_lam_re", "s5_lam_im",
          "s5_log_dt", "s5_b_re", "s5_b_im", "s5_c_re", "s5_c_im", "s5_d", "s5_b_glu", "ret_decay_logit", "ret_norm_g",
          "final_norm_g")
_WEIGHTS = ("c_ctx", "norm_g", "w_ada", "b_ada", "w_in", "hgrn_lb_logits", "hgrn_norm_g", "gla_w_gk", "gla_b_gk",
            "gla_norm_g", "s5_lam_re", "s5_lam_im", "s5_log_dt", "s5_b_re", "s5_b_im", "s5_c_re", "s5_c_im", "s5_d",
            "s5_w_glu", "s5_b_glu", "ret_decay_logit", "ret_norm_g", "w_out", "final_norm_g")


def _pad_rows(flat, mult=8):
    rows = -(-flat.size // 128)
    rows = -(-rows // mult) * mult
    return jnp.pad(flat, (0, rows * 128 - flat.size)).reshape(rows, 128)


def _unshard_cols(g, inner):
    k = len(inner)
    a = g.reshape((N_DEV,) + tuple(inner))
    return jnp.moveaxis(a, 0, k - 1).reshape(tuple(inner[:-1]) + (N_DEV * inner[-1],))


def kernel(x, c, ctx, c_ctx, norm_g, w_ada, b_ada, w_in, hgrn_lb_logits, hgrn_norm_g, gla_w_gk, gla_b_gk, gla_norm_g, s5_lam_re, s5_lam_im, s5_log_dt, s5_b_re, s5_b_im, s5_c_re, s5_c_im, s5_d, s5_w_glu, s5_b_glu, ret_decay_logit, ret_norm_g, w_out, final_norm_g, loss_target, m_c_ctx, m_norm_g, m_w_ada, m_b_ada, m_w_in, m_hgrn_lb_logits, m_hgrn_norm_g, m_gla_w_gk, m_gla_b_gk, m_gla_norm_g, m_s5_lam_re, m_s5_lam_im, m_s5_log_dt, m_s5_b_re, m_s5_b_im, m_s5_c_re, m_s5_c_im, m_s5_d, m_s5_w_glu, m_s5_b_glu, m_ret_decay_logit, m_ret_norm_g, m_w_out, m_final_norm_g, v_c_ctx, v_norm_g, v_w_ada, v_b_ada, v_w_in, v_hgrn_lb_logits, v_hgrn_norm_g, v_gla_w_gk, v_gla_b_gk, v_gla_norm_g, v_s5_lam_re, v_s5_lam_im, v_s5_log_dt, v_s5_b_re, v_s5_b_im, v_s5_c_re, v_s5_c_im, v_s5_d, v_s5_w_glu, v_s5_b_glu, v_ret_decay_logit, v_ret_norm_g, v_w_out, v_final_norm_g):
    given = dict(locals())
    depth = norm_g.shape[0]
    n_ctx, n_lat, d_model = ctx.shape[1], x.shape[1], x.shape[2]
    me = 4 * lax.axis_index("x") + 2 * lax.axis_index("y") + lax.axis_index("c")
    w_ada_cols = w_ada.shape[2]

    pack = jnp.concatenate([c.reshape(-1), hgrn_lb_logits.reshape(-1), gla_w_gk.reshape(-1), gla_b_gk.reshape(-1)])
    got = _all_gather(_pad_rows(pack), name="gather_small").reshape(N_DEV, -1)
    o0 = d_model
    o1 = o0 + hgrn_lb_logits.size
    o2 = o1 + gla_w_gk.size
    o3 = o2 + gla_b_gk.size
    c_all = got[:, :o0]
    lb_logits = _unshard_cols(got[:, o0:o1], hgrn_lb_logits.shape)
    w_gk = _unshard_cols(got[:, o1:o2], gla_w_gk.shape)
    b_gk = _unshard_cols(got[:, o2:o3], gla_b_gk.shape)

    cond = jnp.concatenate([c_all, c_ctx[None, :], jnp.zeros((7, d_model), F32)], axis=0)
    act = _silu(cond).astype(BF16)
    mod_part = jnp.concatenate([_mm(act, w_ada[l], name=f"ada{l}_fwd") for l in range(depth)], axis=0)
    mod = _all_gather(mod_part, name="gather_mod").reshape(N_DEV, depth, 16, w_ada_cols)
    mod = jnp.moveaxis(mod, 0, 2).reshape(depth, 16, N_DEV * w_ada_cols) + b_ada[:, None, :]
    mod_lat = lax.dynamic_index_in_dim(mod, me, axis=1, keepdims=False)
    mod_ctx = mod[:, 8]
    seg = lambda l, j: jnp.stack([mod_ctx[l, j * d_model:(j + 1) * d_model],
                                  mod_lat[l, j * d_model:(j + 1) * d_model]])[:, None, :]

    def gathered(w, name):
        flat = w.astype(BF16).reshape(-1, w.shape[-1])
        return _all_gather(flat, name=name).reshape((N_DEV,) + w.shape)

    w_in_full = _w_in_from_shards(gathered(w_in, "gather_w_in"))
    w_out_full = jnp.moveaxis(gathered(w_out, "gather_w_out"), 0, 1).reshape(depth, 4 * BRANCH_W, d_model)
    w_glu_full = jnp.moveaxis(gathered(s5_w_glu, "gather_w_glu"), 0, 1).reshape(depth, BRANCH_W, BRANCH_W)

    rope = _rope_tables(n_ctx, n_lat)
    ops = [_layer_ops(l, n_ctx) for l in range(depth)]
    params = dict(
        layers=[dict(
            norm_g=norm_g[l][None, :], sh=seg(l, 0), sc=seg(l, 1), gt=seg(l, 2),
            shadow_in=jnp.zeros((d_model, W_INP), F32), shadow_out=jnp.zeros((4 * BRANCH_W, d_model), F32),
            shadow_glu=jnp.zeros((BRANCH_W, BRANCH_W), F32),
            hgrn_norm_g=hgrn_norm_g[l][None, :], gla_w_gk=w_gk[l], gla_b_gk=b_gk[l], gla_norm_g=gla_norm_g[l][None, :],
            s5_lam_re=s5_lam_re[l], s5_lam_im=s5_lam_im[l], s5_log_dt=s5_log_dt[l], s5_b_re=s5_b_re[l],
            s5_b_im=s5_b_im[l], s5_c_re=s5_c_re[l], s5_c_im=s5_c_im[l], s5_d=s5_d[l], s5_b_glu=s5_b_glu[l][None, :],
            ret_decay_logit=ret_decay_logit[l], ret_norm_g=ret_norm_g[l][None, :]) for l in range(depth)],
        lb_logits=lb_logits,
        h0=jnp.concatenate([ctx[0], x[0]], axis=0))

    def body(p):
        lb_p = jax.nn.softmax(p["lb_logits"], axis=0)
        lower = jnp.cumsum(lb_p, axis=0) - lb_p[0:1]
        h = p["h0"]
        for l in range(depth):
            h = _layer(ops[l], h, p["layers"][l], w_in_full[l], w_out_full[l], w_glu_full[l], lower[l], rope)
        return h

    h_out, pullback = jax.vjp(body, params)
    dh, dg_final, loss_part = _loss_call(h_out, loss_target[0], final_norm_g[None, :], n_ctx=n_ctx, name="loss_head")
    (grads,) = pullback(dh)
    loss = lax.psum(loss_part[0, 0], ("x", "y", "c"))
    gl = grads["layers"]

    dmod = jnp.stack([jnp.concatenate([gl[l][k][s, 0] for k in ("sh", "sc", "gt")])
                      for l in range(depth) for s in (1, 0)])
    dmod = jnp.pad(dmod, ((0, 8 - 2 * depth), (0, 0)))
    dmod_all = _all_gather(dmod, name="gather_dmod")
    dmod_sum = _sum8(dmod_all, name="sum_dmod")
    grad_b_ada = jnp.stack([dmod_sum[2 * l] + dmod_sum[2 * l + 1] for l in range(depth)])
    grad_w_ada, ds_ctx = [], jnp.zeros((d_model,), F32)
    for l in range(depth):
        rows = jnp.concatenate([dmod_all[:, 2 * l], dmod_sum[2 * l + 1][None, :], jnp.zeros((7, 3 * d_model), F32)])
        mine = lax.dynamic_slice_in_dim(rows, me * w_ada_cols, w_ada_cols, axis=1)
        grad_w_ada.append(_mm(act.T, mine, name=f"ada{l}_dw", tk=16))
        ds_ctx = ds_ctx + _mm(mine, w_ada[l], nt=True, name=f"ada{l}_da")[8]
    grad_w_ada = jnp.stack(grad_w_ada)

    def scattered(g, axis, name):
        parts = g.reshape(g.shape[:axis] + (N_DEV, g.shape[axis] // N_DEV) + g.shape[axis + 1:])
        parts = jnp.moveaxis(parts, axis, 0)
        local = parts.shape[1:]
        return _reduce_scatter(parts.reshape(N_DEV, -1, local[-1]), name=name).reshape(local)

    w_in_slabs = _w_in_to_shards(jnp.stack([gl[l]["shadow_in"] for l in range(depth)]))
    grad_w_in = _reduce_scatter(w_in_slabs.reshape(N_DEV, depth * d_model, W_IN_SHARD), name="rs_w_in").reshape(w_in.shape)
    grad_w_out = scattered(jnp.stack([gl[l]["shadow_out"] for l in range(depth)]), 1, "rs_w_out")
    grad_w_glu = scattered(jnp.stack([gl[l]["shadow_glu"] for l in range(depth)]), 1, "rs_w_glu")

    stack = lambda k, shape=None: jnp.stack([gl[l][k] for l in range(depth)]).reshape(shape or given[k].shape)
    small = dict(norm_g=stack("norm_g"), hgrn_lb_logits=grads["lb_logits"], hgrn_norm_g=stack("hgrn_norm_g"),
                 gla_w_gk=stack("gla_w_gk", w_gk.shape), gla_b_gk=stack("gla_b_gk", b_gk.shape),
                 gla_norm_g=stack("gla_norm_g"), s5_d=stack("s5_d"), s5_b_glu=stack("s5_b_glu"),
                 ret_decay_logit=stack("ret_decay_logit"), ret_norm_g=stack("ret_norm_g"), final_norm_g=dg_final[0])
    for k in ("s5_lam_re", "s5_lam_im", "s5_log_dt", "s5_b_re", "s5_b_im", "s5_c_re", "s5_c_im"):
        small[k] = stack(k)
    pieces = [small[k] for k in _SMALL] + [ds_ctx]
    flat = _pad_rows(jnp.concatenate([a.reshape(-1) for a in pieces]))
    total = _sum8(_all_gather(flat, name="gather_small_grads"), name="sum_small_grads").reshape(-1)
    reduced, off = {}, 0
    for k, a in zip(_SMALL + ("ds_ctx",), pieces):
        reduced[k] = total[off:off + a.size].reshape(a.shape)
        off += a.size

    def my_cols(a):
        w = a.shape[-1] // N_DEV
        return lax.dynamic_slice_in_dim(a, me * w, w, axis=a.ndim - 1)

    dsilu = jax.vjp(_silu, c_ctx)[1](reduced.pop("ds_ctx"))[0]
    grad = dict(reduced, c_ctx=dsilu, w_ada=grad_w_ada, b_ada=grad_b_ada, w_in=grad_w_in, w_out=grad_w_out,
                s5_w_glu=grad_w_glu)
    for k in ("hgrn_lb_logits", "gla_w_gk", "gla_b_gk"):
        grad[k] = my_cols(grad[k])

    delta, new_m, new_v = {}, {}, {}
    for k in _WEIGHTS:
        delta[k], new_m[k], new_v[k] = _adamw(given[k], grad[k], given["m_" + k], given["v_" + k], name="adamw_" + k)
    grad_x = grads["h0"][n_ctx:][None]
    return (loss, grad_x, *[grad[k] for k in _WEIGHTS], *[delta[k] for k in _WEIGHTS],
            *[new_m[k] for k in _WEIGHTS], *[new_v[k] for k in _WEIGHTS])
```

```python
import functools
import math

import jax
import jax.numpy as jnp
from jax import lax
from jax.experimental import pallas as pl
from jax.experimental.pallas import tpu as pltpu

F32 = jnp.float32
BF16 = jnp.bfloat16

EPS = 1e-6
BRANCH_W = 1024
CHUNK = 64
HEAD_DK = 128
A_MIN_FORGET = 1e-6
B_GATE_NORM = 16.0
B_GATE_RANK = 16
C_GROUP = 16
C_GROUPS = 64
C_STATE = 64
C_MAX_RE = -1e-4
GRID_W = 64
ROPE_BASE = 10000.0
N_DEV = 8

ADAM_LR = 0.001
ADAM_B1 = 0.9
ADAM_B2 = 0.999
ADAM_EPS = 1e-08
ADAM_WD = 0.01
ADAM_STEP = 10

VMEM_LIMIT = 52 * 1024 * 1024
SCAN_ROWS = 256
S5_ROWS = 128
S5_GB = 8
S5_GBW = S5_GB * C_STATE


def _tile(dim, cands):
    for c in cands:
        if dim % c == 0:
            return c
    return dim


def _cparams(sem):
    return pltpu.CompilerParams(dimension_semantics=sem, vmem_limit_bytes=VMEM_LIMIT)


MM_VMEM_BUDGET = 40 * 1024 * 1024


def _mm_tiles(M, N, K, sa, sb, so):
    tms = [t for t in (1088, 1024, 544, 512) if M % t == 0] or [M]
    tns = [t for t in (1024, 896, 640, 512, 384, 256) if N % t == 0] or [N]
    tks = [t for t in (K, 4352, 4096, 2688, 2176, 2048, 1024, 896, 512, 256, 128) if K % t == 0]
    for tk in tks:
        for tm in tms:
            for tn in tns:
                acc = tm * tn * 4 if tk < K else 0
                if 2 * (tm * tk * sa + tk * tn * sb) + acc + 2 * tm * tn * so <= MM_VMEM_BUDGET:
                    return tm, tn, tk
    return tms[-1], tns[-1], tks[-1]


def _mm(a, b, *, nt=False, out_dtype=F32, name):
    M, K = a.shape
    N = b.shape[0] if nt else b.shape[1]
    tm, tn, tk = _mm_tiles(M, N, K, a.dtype.itemsize, b.dtype.itemsize, jnp.dtype(out_dtype).itemsize)
    nk = K // tk
    dims = (((1,), (1,)), ((), ())) if nt else (((1,), (0,)), ((), ()))

    if nk == 1:
        def body1(a_ref, b_ref, o_ref):
            o_ref[...] = lax.dot_general(a_ref[...].astype(BF16), b_ref[...].astype(BF16), dims,
                                         preferred_element_type=F32).astype(o_ref.dtype)

        b_spec1 = (pl.BlockSpec((tn, K), lambda i, j: (j, 0)) if nt else pl.BlockSpec((K, tn), lambda i, j: (0, j)))
        return pl.pallas_call(
            body1, name=name, grid=(M // tm, N // tn),
            in_specs=[pl.BlockSpec((tm, K), lambda i, j: (i, 0)), b_spec1],
            out_specs=pl.BlockSpec((tm, tn), lambda i, j: (i, j)),
            out_shape=jax.ShapeDtypeStruct((M, N), out_dtype),
            compiler_params=_cparams(("parallel", "arbitrary")),
        )(a, b)

    def body(a_ref, b_ref, o_ref, acc_ref):
        k = pl.program_id(2)

        @pl.when(k == 0)
        def _():
            acc_ref[...] = jnp.zeros_like(acc_ref)

        acc_ref[...] += lax.dot_general(a_ref[...].astype(BF16), b_ref[...].astype(BF16), dims,
                                        preferred_element_type=F32)

        @pl.when(k == nk - 1)
        def _():
            o_ref[...] = acc_ref[...].astype(o_ref.dtype)

    b_spec = (pl.BlockSpec((tn, tk), lambda i, j, k: (j, k)) if nt
              else pl.BlockSpec((tk, tn), lambda i, j, k: (k, j)))
    return pl.pallas_call(
        body, name=name, grid=(M // tm, N // tn, nk),
        in_specs=[pl.BlockSpec((tm, tk), lambda i, j, k: (i, k)), b_spec],
        out_specs=pl.BlockSpec((tm, tn), lambda i, j, k: (i, j)),
        out_shape=jax.ShapeDtypeStruct((M, N), out_dtype),
        scratch_shapes=[pltpu.VMEM((tm, tn), F32)],
        compiler_params=_cparams(("parallel", "parallel", "arbitrary")),
    )(a, b)


def _linear(name):
    @jax.custom_vjp
    def op(a, w, shadow):
        return _mm(a, w, name=name + "_fwd")

    def fwd(a, w, shadow):
        return _mm(a, w, name=name + "_fwd"), (a, w)

    def bwd(res, dy):
        a, w = res
        da = _mm(dy, w, nt=True, out_dtype=a.dtype, name=name + "_da")
        dw = _mm(a.T, dy, name=name + "_dw")
        return da, jnp.zeros_like(w), dw

    op.defvjp(fwd, bwd)
    return op


def _rowwise(fn, specs, out_defs, *, ncb, tr, n_ctx, name):
    ncx = n_ctx // tr
    diff = [k for k, s in enumerate(specs) if s in ("row", "const", "col", "seg")]

    def in_spec(s, arr):
        if s == "row":
            return pl.BlockSpec((tr, arr.shape[1] // ncb), lambda j, i: (i, j))
        if s == "rowc":
            return pl.BlockSpec((tr, arr.shape[1]), lambda j, i: (i, 0))
        if s in ("const", "fixed"):
            return pl.BlockSpec(arr.shape, lambda j, i: (0, 0))
        if s == "col":
            return pl.BlockSpec((None,) + arr.shape[1:], lambda j, i: (j, 0, 0))
        if s == "seg":
            return pl.BlockSpec((None,) + arr.shape[1:], lambda j, i: ((i >= ncx).astype(jnp.int32), 0, 0))
        raise ValueError(s)

    def fwd_call(*arrays):
        L = arrays[specs.index("row")].shape[0]
        n_in = len(arrays)

        def body(*refs):
            outs = fn(*[r[...] for r in refs[:n_in]])
            for o_ref, o in zip(refs[n_in:], outs):
                o_ref[...] = o.astype(o_ref.dtype)

        return pl.pallas_call(
            body, name=name + "_fwd", grid=(ncb, L // tr),
            in_specs=[in_spec(s, a) for s, a in zip(specs, arrays)],
            out_specs=[pl.BlockSpec((tr, c // ncb), lambda j, i: (i, j)) for c, _ in out_defs],
            out_shape=[jax.ShapeDtypeStruct((L, c), dt) for c, dt in out_defs],
            compiler_params=_cparams(("parallel", "arbitrary")),
        )(*arrays)

    def bwd_call(arrays, cts):
        L = arrays[specs.index("row")].shape[0]
        n_in, n_ct = len(arrays), len(cts)

        def body(*refs):
            i = pl.program_id(1)
            vals = [r[...] for r in refs[:n_in]]
            ct_vals = tuple(r[...] for r in refs[n_in:n_in + n_ct])
            g_refs = refs[n_in + n_ct:]

            def f(*dv):
                full = list(vals)
                for k, v in zip(diff, dv):
                    full[k] = v
                return tuple(o.astype(dt) for o, (_, dt) in zip(fn(*full), out_defs))

            _, vjp = jax.vjp(f, *[vals[k] for k in diff])
            grads = vjp(ct_vals)
            for k, g_ref, g in zip(diff, g_refs, grads):
                if specs[k] == "row":
                    g_ref[...] = g.astype(g_ref.dtype)
                else:
                    first = (i == 0) | (i == ncx) if specs[k] == "seg" else (i == 0)

                    @pl.when(first)
                    def _(g_ref=g_ref, g=g):
                        g_ref[...] = g

                    @pl.when(jnp.logical_not(first))
                    def _(g_ref=g_ref, g=g):
                        g_ref[...] += g

        out_specs, out_shape = [], []
        for k in diff:
            s, a = specs[k], arrays[k]
            if s == "row":
                out_specs.append(pl.BlockSpec((tr, a.shape[1] // ncb), lambda j, i: (i, j)))
                out_shape.append(jax.ShapeDtypeStruct(a.shape, a.dtype))
            elif s == "const":
                out_specs.append(pl.BlockSpec((None,) + a.shape, lambda j, i: (j, 0, 0)))
                out_shape.append(jax.ShapeDtypeStruct((ncb,) + a.shape, F32))
            elif s == "col":
                out_specs.append(pl.BlockSpec((None,) + a.shape[1:], lambda j, i: (j, 0, 0)))
                out_shape.append(jax.ShapeDtypeStruct(a.shape, F32))
            else:
                out_specs.append(pl.BlockSpec((None, None) + a.shape[1:],
                                              lambda j, i: (j, (i >= ncx).astype(jnp.int32), 0, 0)))
                out_shape.append(jax.ShapeDtypeStruct((ncb,) + a.shape, F32))
        ct_specs = [pl.BlockSpec((tr, c // ncb), lambda j, i: (i, j)) for c, _ in out_defs]
        grads = pl.pallas_call(
            body, name=name + "_bwd", grid=(ncb, L // tr),
            in_specs=[in_spec(s, a) for s, a in zip(specs, arrays)] + ct_specs,
            out_specs=out_specs, out_shape=out_shape,
            compiler_params=_cparams(("parallel", "arbitrary")),
        )(*arrays, *cts)
        full = [jnp.zeros_like(a) for a in arrays]
        for k, g in zip(diff, grads):
            full[k] = jnp.sum(g, axis=0) if specs[k] in ("const", "seg") else g
        return tuple(full)

    @jax.custom_vjp
    def op(*arrays):
        return tuple(fwd_call(*arrays))

    def op_fwd(*arrays):
        return tuple(fwd_call(*arrays)), arrays

    def op_bwd(arrays, cts):
        return bwd_call(arrays, cts)

    op.defvjp(op_fwd, op_bwd)
    return op


def _sigmoid(x):
    return 1.0 / (1.0 + jnp.exp(-x))


def _log_sigmoid(x):
    return jnp.minimum(x, 0.0) - jnp.log(1.0 + jnp.exp(-jnp.abs(x)))


def _silu(x):
    return x * _sigmoid(x)


def _rms(x, g):
    return x * lax.rsqrt(jnp.mean(x * x, axis=-1, keepdims=True) + EPS) * g


def _modnorm_fn(h, g, sc, sh):
    return ((_rms(h, g) * (1.0 + sc) + sh).astype(BF16),)


def _resid_fn(h, y, gt):
    return (h + gt * y,)


def _hgrn_pro_fn(zf, zb, lbf, lbb):
    def one(z, lb):
        f = lb + (1.0 - lb) * _sigmoid(z)
        return (1.0 - lb) * _sigmoid(-z), jnp.log(jnp.maximum(f, A_MIN_FORGET))

    kf, gf = one(zf, lbf)
    kb, gb = one(zb, lbb)
    return kf, gf, kb, gb


def _hgrn_epi_fn(of, ob, gate, g):
    return ((_rms(of + ob, g) * _silu(gate)).astype(BF16),)


def _gla_pro_fn(q, lr, wf, wb, bf, bb):
    gf = _log_sigmoid(jnp.dot(lr, wf, preferred_element_type=F32) + bf) / B_GATE_NORM
    gb = _log_sigmoid(jnp.dot(lr, wb, preferred_element_type=F32) + bb) / B_GATE_NORM
    return q * (HEAD_DK ** -0.5), gf, gb


def _ret_pro_fn(q, k, cos2, sin2, lgf, lgb, swap):
    def rope(x):
        return x * cos2 + jnp.dot(x, swap, preferred_element_type=F32, precision=lax.Precision.HIGHEST) * sin2

    zero = jnp.zeros_like(q)
    return rope(q) * (HEAD_DK ** -0.5), rope(k), zero + lgf, zero + lgb


def _ret_epi_fn(of, ob, gate, g):
    x = of + ob
    mu = jnp.mean(x, axis=-1, keepdims=True)
    var = jnp.mean(jnp.square(x - mu), axis=-1, keepdims=True)
    return (((x - mu) * lax.rsqrt(var + EPS) * g * _silu(gate)).astype(BF16),)


def _s5_mid_fn(yf, yb, u, d):
    return (jax.nn.gelu(yf + yb + d * u),)


def _s5_epi_fn(z, t, gate, b):
    return ((z * _sigmoid(t + b) * _silu(gate)).astype(BF16),)


def _blk(mode, i, nb, nc):
    if mode == 0:
        return i
    if mode == 1:
        return nb - 1 - i
    if mode == 2:
        return jnp.where(i < nc, nc - 1 - i, nb - 1 - i + nc)
    return jnp.where(i < nb - nc, i + nc, nc - nb + i)


def _split3(x):
    hi = x.astype(BF16)
    r1 = x - hi.astype(F32)
    mid = r1.astype(BF16)
    lo = (r1 - mid.astype(F32)).astype(BF16)
    return hi, mid, lo


def _cum(tri, x):
    hi, mid, lo = _split3(x)
    d = lambda v: jnp.dot(tri, v, preferred_element_type=F32)
    return d(hi) + d(mid) + d(lo)


def _dot(a, b):
    return jnp.dot(a.astype(BF16), b.astype(BF16), preferred_element_type=F32)


def _dot_nt(a, b):
    return lax.dot_general(a.astype(BF16), b.astype(BF16), (((1,), (1,)), ((), ())), preferred_element_type=F32)


def _dot_tn(a, b):
    return jnp.dot(a.astype(F32).T.astype(BF16), b.astype(BF16), preferred_element_type=F32)


def _dot3(a, b):
    ah = a.astype(BF16)
    al = (a - ah.astype(F32)).astype(BF16)
    bh = b.astype(BF16)
    bl = (b - bh.astype(F32)).astype(BF16)
    d = lambda p, q: jnp.dot(p, q, preferred_element_type=F32)
    return d(ah, bh) + d(ah, bl) + d(al, bh)


def _dot3_tn(a, b):
    return _dot3(a.astype(F32).T, b)


def _gla_masks(flip):
    r = lax.broadcasted_iota(jnp.int32, (CHUNK, CHUNK), 0)
    c = lax.broadcasted_iota(jnp.int32, (CHUNK, CHUNK), 1)
    causal = (c >= r) if flip else (c <= r)
    tri = causal.astype(BF16)
    tri_t = jnp.logical_not(causal) | (r == c)
    return causal, tri, tri_t.astype(BF16)


def _gla_chunk_terms(q, k, g, tri, flip):
    b = _cum(tri, g)
    mid = CHUNK // 2
    bm = b[mid:mid + 1, :]
    bl = b[0:1, :] if flip else b[CHUNK - 1:CHUNK, :]
    e_q = jnp.exp(b - bm)
    e_k = jnp.exp(bm - b)
    return b, bl, e_q, e_k, q * e_q, k * e_k


def _gla_fwd_call(q, k, v, g, *, flip, n_ctx, name):
    L = q.shape[0]
    H = q.shape[1] // HEAD_DK
    dv = v.shape[1] // H
    nb, nc, sub = L // SCAN_ROWS, n_ctx // SCAN_ROWS, SCAN_ROWS // CHUNK
    mode = 2 if flip else 0
    order = list(range(sub))[::-1] if flip else list(range(sub))

    def body(q_ref, k_ref, v_ref, g_ref, o_ref, sp_ref, st_ref):
        @pl.when(pl.program_id(1) == 0)
        def _():
            st_ref[...] = jnp.zeros_like(st_ref)

        causal, tri, _ = _gla_masks(flip)
        for s in order:
            sl = pl.ds(s * CHUNK, CHUNK)
            qc, kc, vc, gc = q_ref[sl, :], k_ref[sl, :], v_ref[sl, :], g_ref[sl, :]
            b, bl, _, _, qt, kt = _gla_chunk_terms(qc, kc, gc, tri, flip)
            a = jnp.where(causal, _dot_nt(qt, kt), 0.0)
            sp = st_ref[...]
            sp_ref[s] = sp
            o_ref[sl, :] = _dot(a, vc) + _dot_nt(qc * jnp.exp(b), sp)
            st_ref[...] = sp * jnp.exp(bl) + _dot3_tn(vc, kc * jnp.exp(bl - b))

    row = lambda w: pl.BlockSpec((SCAN_ROWS, w), lambda h, i: (_blk(mode, i, nb, nc), h))
    return pl.pallas_call(
        body, name=name, grid=(H, nb),
        in_specs=[row(HEAD_DK), row(HEAD_DK), row(dv), row(HEAD_DK)],
        out_specs=[row(dv), pl.BlockSpec((None, None, sub, dv, HEAD_DK),
                                         lambda h, i: (h, _blk(mode, i, nb, nc), 0, 0, 0))],
        out_shape=[jax.ShapeDtypeStruct((L, H * dv), F32),
                   jax.ShapeDtypeStruct((H, nb, sub, dv, HEAD_DK), F32)],
        scratch_shapes=[pltpu.VMEM((dv, HEAD_DK), F32)],
        compiler_params=_cparams(("parallel", "arbitrary")),
    )(q, k, v, g)


def _gla_bwd_call(q, k, v, g, sp, do, *, flip, n_ctx, name):
    L = q.shape[0]
    H = q.shape[1] // HEAD_DK
    dv = v.shape[1] // H
    nb, nc, sub = L // SCAN_ROWS, n_ctx // SCAN_ROWS, SCAN_ROWS // CHUNK
    mode = 3 if flip else 1
    order = list(range(sub)) if flip else list(range(sub))[::-1]
    last = 0 if flip else CHUNK - 1

    def body(q_ref, k_ref, v_ref, g_ref, sp_ref, do_ref, dq_ref, dk_ref, dv_ref, dg_ref, ds_ref):
        @pl.when(pl.program_id(1) == 0)
        def _():
            ds_ref[...] = jnp.zeros_like(ds_ref)

        causal, tri, tri_t = _gla_masks(flip)
        is_last = lax.broadcasted_iota(jnp.int32, (CHUNK, HEAD_DK), 0) == last
        for s in order:
            sl = pl.ds(s * CHUNK, CHUNK)
            qc, kc, vc, gc, doc = q_ref[sl, :], k_ref[sl, :], v_ref[sl, :], g_ref[sl, :], do_ref[sl, :]
            b, bl, e_q, e_k, qt, kt = _gla_chunk_terms(qc, kc, gc, tri, flip)
            e_b, e_l = jnp.exp(b), jnp.exp(bl - b)
            a = jnp.where(causal, _dot_nt(qt, kt), 0.0)
            p, kd = qc * e_b, kc * e_l
            spv, ds = sp_ref[s], ds_ref[...]
            dv_ref[sl, :] = _dot_tn(a, doc) + _dot_nt(kd, ds)
            da = jnp.where(causal, _dot_nt(doc, vc), 0.0)
            dkd = _dot3(vc, ds)
            dq = _dot3(da, kt) * e_q + _dot3(doc, spv) * e_b
            dk = _dot3_tn(da, qt) * e_k + dkd * e_l
            dq_ref[sl, :] = dq
            dk_ref[sl, :] = dk
            db_last = (jnp.sum(dkd * kd, axis=0, keepdims=True)
                       + jnp.exp(bl) * jnp.sum(ds * spv, axis=0, keepdims=True))
            db = qc * dq - kc * dk + jnp.where(is_last, db_last, 0.0)
            dg_ref[sl, :] = _cum(tri_t, db)
            ds_ref[...] = ds * jnp.exp(bl) + _dot3_tn(doc, p)

    row = lambda w: pl.BlockSpec((SCAN_ROWS, w), lambda h, i: (_blk(mode, i, nb, nc), h))
    return pl.pallas_call(
        body, name=name, grid=(H, nb),
        in_specs=[row(HEAD_DK), row(HEAD_DK), row(dv), row(HEAD_DK),
                  pl.BlockSpec((None, None, sub, dv, HEAD_DK), lambda h, i: (h, _blk(mode, i, nb, nc), 0, 0, 0)),
                  row(dv)],
        out_specs=[row(HEAD_DK), row(HEAD_DK), row(dv), row(HEAD_DK)],
        out_shape=[jax.ShapeDtypeStruct(q.shape, F32), jax.ShapeDtypeStruct(k.shape, F32),
                   jax.ShapeDtypeStruct(v.shape, F32), jax.ShapeDtypeStruct(g.shape, F32)],
        scratch_shapes=[pltpu.VMEM((dv, HEAD_DK), F32)],
        compiler_params=_cparams(("parallel", "arbitrary")),
    )(q, k, v, g, sp, do)


def _gla(name, *, flip, n_ctx):
    @jax.custom_vjp
    def op(q, k, v, g):
        return _gla_fwd_call(q, k, v, g, flip=flip, n_ctx=n_ctx, name=name + "_fwd")[0]

    def fwd(q, k, v, g):
        o, sp = _gla_fwd_call(q, k, v, g, flip=flip, n_ctx=n_ctx, name=name + "_fwd")
        return o, (q, k, v, g, sp)

    def bwd(res, do):
        return tuple(_gla_bwd_call(*res, do, flip=flip, n_ctx=n_ctx, name=name + "_bwd"))

    op.defvjp(fwd, bwd)
    return op


def _s5_table(ar, ai, desc):
    def mul(p, q):
        return p[0] * q[0] - p[1] * q[1], p[0] * q[1] + p[1] * q[0]

    a1 = (ar.reshape(-1), ai.reshape(-1))
    ar = a1[0]
    a2 = mul(a1, a1)
    a4 = mul(a2, a2)
    pw = [a1]
    for _ in range(7):
        pw.append(mul(pw[-1], a1))
    if desc:
        pw = pw[::-1]
    zero = jnp.zeros_like(ar)
    rows = [a1[0], a1[1], a2[0], a2[1], a4[0], a4[1], zero, zero] + [p[0] for p in pw] + [p[1] for p in pw]
    return jnp.stack(rows, axis=0)


S5_LANES = C_GROUPS * C_STATE
S5_LC = 1024


def _s5_scan(xr_ref, xi_ref, tab_ref, car_ref, desc, on_slab=None):
    nslab = S5_ROWS // 8
    rows = lax.broadcasted_iota(jnp.int32, (8, S5_LC), 0)

    def slab(t, carry):
        r0 = pl.multiple_of(((nslab - 1 - t) if desc else t) * 8, 8)
        for lc in range(S5_LANES // S5_LC):
            cs = pl.ds(lc * S5_LC, S5_LC)
            xr, xi = xr_ref[pl.ds(r0, 8), cs], xi_ref[pl.ds(r0, 8), cs]
            for n, s in enumerate((1, 2, 4)):
                ar, ai = tab_ref[2 * n:2 * n + 1, cs], tab_ref[2 * n + 1:2 * n + 2, cs]
                keep = (rows < 8 - s) if desc else (rows >= s)
                sr = jnp.where(keep, pltpu.roll(xr, (8 - s) if desc else s, 0), 0.0)
                si = jnp.where(keep, pltpu.roll(xi, (8 - s) if desc else s, 0), 0.0)
                xr, xi = xr + ar * sr - ai * si, xi + ar * si + ai * sr
            cr, ci = car_ref[0:1, cs], car_ref[1:2, cs]
            pr, pi = tab_ref[8:16, cs], tab_ref[16:24, cs]
            xr, xi = xr + pr * cr - pi * ci, xi + pr * ci + pi * cr
            if on_slab is not None:
                on_slab(r0, cs, xr, xi, cr, ci, rows)
            xr_ref[pl.ds(r0, 8), cs] = xr
            xi_ref[pl.ds(r0, 8), cs] = xi
            e = 0 if desc else 7
            car_ref[0:1, cs] = xr[e:e + 1, :]
            car_ref[1:2, cs] = xi[e:e + 1, :]
        return carry

    lax.fori_loop(0, nslab, slab, 0)


def _s5_fwd_call(u, tab, wb_re, wb_im, wc_re, wc_im, *, flip, n_ctx, name):
    L = u.shape[0]
    nb, nc = L // S5_ROWS, n_ctx // S5_ROWS
    mode, desc = (2, True) if flip else (0, False)

    def body(u_ref, tab_ref, wbr_ref, wbi_ref, wcr_ref, wci_ref, y_ref, xr_ref, xi_ref, car_ref):
        @pl.when(pl.program_id(0) == 0)
        def _():
            car_ref[...] = jnp.zeros_like(car_ref)

        for gb in range(S5_GB):
            us, cs = pl.ds(gb * 128, 128), pl.ds(gb * S5_GBW, S5_GBW)
            ub = u_ref[:, us]
            xr_ref[:, cs] = _dot(ub, wbr_ref[gb])
            xi_ref[:, cs] = _dot(ub, wbi_ref[gb])
        _s5_scan(xr_ref, xi_ref, tab_ref, car_ref, desc)
        for gb in range(S5_GB):
            us, cs = pl.ds(gb * 128, 128), pl.ds(gb * S5_GBW, S5_GBW)
            y_ref[:, us] = _dot(xr_ref[:, cs], wcr_ref[gb]) - _dot(xi_ref[:, cs], wci_ref[gb])

    row = lambda w: pl.BlockSpec((S5_ROWS, w), lambda i: (_blk(mode, i, nb, nc), 0))
    full = lambda a: pl.BlockSpec(a.shape, lambda i: (0,) * a.ndim)
    return pl.pallas_call(
        body, name=name, grid=(nb,),
        in_specs=[row(BRANCH_W), full(tab), full(wb_re), full(wb_im), full(wc_re), full(wc_im)],
        out_specs=[row(BRANCH_W), row(S5_LANES), row(S5_LANES)],
        out_shape=[jax.ShapeDtypeStruct((L, BRANCH_W), F32), jax.ShapeDtypeStruct((L, S5_LANES), F32),
                   jax.ShapeDtypeStruct((L, S5_LANES), F32)],
        scratch_shapes=[pltpu.VMEM((8, S5_LANES), F32)],
        compiler_params=_cparams(("arbitrary",)),
    )(u, tab, wb_re, wb_im, wc_re, wc_im)


def _s5_bwd_call(dy, u, xr, xi, tab, wbt_re, wbt_im, wct_re, wct_im, *, flip, n_ctx, name):
    L = u.shape[0]
    nb, nc = L // S5_ROWS, n_ctx // S5_ROWS
    mode, desc = (3, False) if flip else (1, True)

    def body(dy_ref, u_ref, xr_ref, xi_ref, tab_ref, wbtr_ref, wbti_ref, wctr_ref, wcti_ref,
             du_ref, dwbr_ref, dwbi_ref, dwcr_ref, dwci_ref, ga_ref, gr_ref, gi_ref, car_ref):
        @pl.when(pl.program_id(0) == 0)
        def _():
            car_ref[...] = jnp.zeros_like(car_ref)
            ga_ref[...] = jnp.zeros_like(ga_ref)
            for r in (dwbr_ref, dwbi_ref, dwcr_ref, dwci_ref):
                r[...] = jnp.zeros_like(r)

        for gb in range(S5_GB):
            us, cs = pl.ds(gb * 128, 128), pl.ds(gb * S5_GBW, S5_GBW)
            dyb = dy_ref[:, us]
            gr_ref[:, cs] = _dot(dyb, wctr_ref[gb])
            gi_ref[:, cs] = -_dot(dyb, wcti_ref[gb])

        def lam_grad(r0, cs, gr, gi, cr, ci, rows):
            edge = 7 if desc else 0
            pr = jnp.where(rows == edge, cr, pltpu.roll(gr, 7 if desc else 1, 0))
            pi = jnp.where(rows == edge, ci, pltpu.roll(gi, 7 if desc else 1, 0))
            x_r, x_i = xr_ref[pl.ds(r0, 8), cs], xi_ref[pl.ds(r0, 8), cs]
            ga_ref[0:8, cs] += x_r * pr + x_i * pi
            ga_ref[8:16, cs] += x_r * pi - x_i * pr

        _s5_scan(gr_ref, gi_ref, tab_ref, car_ref, desc, on_slab=lam_grad)
        for gb in range(S5_GB):
            us, cs = pl.ds(gb * 128, 128), pl.ds(gb * S5_GBW, S5_GBW)
            g_r, g_i, ub, dyb = gr_ref[:, cs], gi_ref[:, cs], u_ref[:, us], dy_ref[:, us]
            du_ref[:, us] = _dot(g_r, wbtr_ref[gb]) + _dot(g_i, wbti_ref[gb])
            dwbr_ref[gb] += _dot_tn(ub, g_r)
            dwbi_ref[gb] += _dot_tn(ub, g_i)
            dwcr_ref[gb] += _dot_tn(xr_ref[:, cs], dyb)
            dwci_ref[gb] -= _dot_tn(xi_ref[:, cs], dyb)

    row = lambda w: pl.BlockSpec((S5_ROWS, w), lambda i: (_blk(mode, i, nb, nc), 0))
    full = lambda shape: pl.BlockSpec(shape, lambda i: (0,) * len(shape))
    wb_shape, wc_shape = (S5_GB, 128, S5_GBW), (S5_GB, S5_GBW, 128)
    return pl.pallas_call(
        body, name=name, grid=(nb,),
        in_specs=[row(BRANCH_W), row(BRANCH_W), row(S5_LANES), row(S5_LANES), full(tab.shape),
                  full(wc_shape), full(wc_shape), full(wb_shape), full(wb_shape)],
        out_specs=[row(BRANCH_W), full(wb_shape), full(wb_shape), full(wc_shape), full(wc_shape),
                   full((16, S5_LANES))],
        out_shape=[jax.ShapeDtypeStruct((L, BRANCH_W), F32),
                   jax.ShapeDtypeStruct(wb_shape, F32), jax.ShapeDtypeStruct(wb_shape, F32),
                   jax.ShapeDtypeStruct(wc_shape, F32), jax.ShapeDtypeStruct(wc_shape, F32),
                   jax.ShapeDtypeStruct((16, S5_LANES), F32)],
        scratch_shapes=[pltpu.VMEM((S5_ROWS, S5_LANES), F32), pltpu.VMEM((S5_ROWS, S5_LANES), F32),
                        pltpu.VMEM((8, S5_LANES), F32)],
        compiler_params=_cparams(("arbitrary",)),
    )(dy, u, xr, xi, tab, wbt_re, wbt_im, wct_re, wct_im)


def _s5(name, *, flip, n_ctx):
    desc_f = flip
    desc_b = not flip

    def run_fwd(u, lr, li, wbr, wbi, wcr, wci):
        tab = _s5_table(lr, li, desc_f)
        c16 = lambda a: a.astype(BF16)
        return _s5_fwd_call(u, tab, c16(wbr), c16(wbi), c16(wcr), c16(wci), flip=flip, n_ctx=n_ctx, name=name + "_fwd")

    @jax.custom_vjp
    def op(u, lr, li, wbr, wbi, wcr, wci):
        return run_fwd(u, lr, li, wbr, wbi, wcr, wci)[0]

    def fwd(u, lr, li, wbr, wbi, wcr, wci):
        y, xr, xi = run_fwd(u, lr, li, wbr, wbi, wcr, wci)
        return y, (u, xr, xi, lr, li, wbr, wbi, wcr, wci)

    def bwd(res, dy):
        u, xr, xi, lr, li, wbr, wbi, wcr, wci = res
        tab = _s5_table(lr, -li, desc_b)
        t16 = lambda a: jnp.swapaxes(a, 1, 2).astype(BF16)
        du, dwbr, dwbi, dwcr, dwci, ga = _s5_bwd_call(
            dy, u, xr, xi, tab, t16(wbr), t16(wbi), t16(wcr), t16(wci), flip=flip, n_ctx=n_ctx, name=name + "_bwd")
        return (du, jnp.sum(ga[0:8], axis=0, keepdims=True), jnp.sum(ga[8:16], axis=0, keepdims=True),
                dwbr, dwbi, dwcr, dwci)

    op.defvjp(fwd, bwd)
    return op


def _loss_call(h, tgt, g, *, n_ctx, name):
    L, D = h.shape
    tr = 128
    ncx = n_ctx // tr

    def body(h_ref, t_ref, g_ref, dh_ref, dg_ref, loss_ref):
        i = pl.program_id(0)

        @pl.when(i == 0)
        def _():
            dg_ref[...] = jnp.zeros_like(dg_ref)
            loss_ref[...] = jnp.zeros_like(loss_ref)

        @pl.when(i < ncx)
        def _():
            dh_ref[...] = jnp.zeros_like(dh_ref)

        @pl.when(i >= ncx)
        def _():
            t = t_ref[...]

            def f(hv, gv):
                e = _rms(hv, gv) - t
                return 0.5 * jnp.sum(e * e) / D

            l, (dh, dg) = jax.value_and_grad(f, argnums=(0, 1))(h_ref[...], g_ref[...])
            dh_ref[...] = dh
            dg_ref[...] += dg
            loss_ref[...] += l

    return pl.pallas_call(
        body, name=name, grid=(L // tr,),
        in_specs=[pl.BlockSpec((tr, D), lambda i: (i, 0)),
                  pl.BlockSpec((tr, D), lambda i: (jnp.maximum(i - ncx, 0), 0)),
                  pl.BlockSpec((1, D), lambda i: (0, 0))],
        out_specs=[pl.BlockSpec((tr, D), lambda i: (i, 0)), pl.BlockSpec((1, D), lambda i: (0, 0)),
                   pl.BlockSpec((8, 128), lambda i: (0, 0))],
        out_shape=[jax.ShapeDtypeStruct((L, D), F32), jax.ShapeDtypeStruct((1, D), F32),
                   jax.ShapeDtypeStruct((8, 128), F32)],
        compiler_params=_cparams(("arbitrary",)),
    )(h, tgt, g)


def _adamw_call(w, g, m, v, *, name):
    R, C = w.shape
    tr = R if R * C * 4 <= (1 << 20) else _tile(R, tuple(t for t in (512, 256, 128, 64, 32, 16, 8) if t * C * 4 <= (1 << 20)))
    c1 = 1.0 / (1.0 - ADAM_B1 ** ADAM_STEP)
    c2 = 1.0 / (1.0 - ADAM_B2 ** ADAM_STEP)

    def body(w_ref, g_ref, m_ref, v_ref, d_ref, nm_ref, nv_ref):
        gv = g_ref[...]
        nm = ADAM_B1 * m_ref[...] + (1.0 - ADAM_B1) * gv
        nv = ADAM_B2 * v_ref[...] + (1.0 - ADAM_B2) * gv * gv
        d_ref[...] = -ADAM_LR * ((nm * c1) / (jnp.sqrt(nv * c2) + ADAM_EPS) + ADAM_WD * w_ref[...])
        nm_ref[...] = nm
        nv_ref[...] = nv

    spec = pl.BlockSpec((tr, C), lambda i: (i, 0))
    return pl.pallas_call(
        body, name=name, grid=(R // tr,), in_specs=[spec] * 4, out_specs=[spec] * 3,
        out_shape=[jax.ShapeDtypeStruct((R, C), F32)] * 3,
        compiler_params=_cparams(("parallel",)),
    )(w, g, m, v)


def _as2d(a):
    n = a.size
    if a.ndim >= 2 and n >= (1 << 20):
        return a.reshape(-1, a.shape[-1])
    if n % 128 == 0:
        return a.reshape(n // 128, 128)
    return a.reshape(1, n)


def _adamw(w, g, m, v, *, name):
    d, nm, nv = _adamw_call(_as2d(w), _as2d(g.astype(F32)), _as2d(m), _as2d(v), name=name)
    return d.reshape(w.shape), nm.reshape(w.shape), nv.reshape(w.shape)


MESH = pl.DeviceIdType.MESH
ANY = pl.BlockSpec(memory_space=pl.ANY)


def _place():
    return lax.axis_index("x"), lax.axis_index("y"), lax.axis_index("c")


def _all_gather(x, *, name):
    M, N = x.shape

    def body(x_ref, out_ref, send_sems, recv_sems, local_sem):
        x_, y_, c_ = _place()
        me, sibling = (x_, y_, c_), (x_, y_, 1 - c_)
        chips = [(1 - x_, y_), (x_, 1 - y_), (1 - x_, 1 - y_)]

        def slab(px, py, pc):
            return out_ref.at[4 * px + 2 * py + pc]

        def copy(k, block, to, src=None):
            return pltpu.make_async_remote_copy(
                src_ref=slab(*block) if src is None else src, dst_ref=slab(*block),
                send_sem=send_sems.at[k], recv_sem=recv_sems.at[k], device_id=to, device_id_type=MESH)

        mine = pltpu.make_async_copy(x_ref, slab(*me), local_sem)
        mine.start()
        first = [copy(0, me, sibling, src=x_ref)]
        first += [copy(1 + j, me, (*chip, c_), src=x_ref) for j, chip in enumerate(chips)]
        for cp in first:
            cp.start()
        passed = [copy(4 + j, (*chip, c_), sibling) for j, chip in enumerate(chips)]
        for j, chip in enumerate(chips):
            copy(1 + j, (*chip, c_), me).wait_recv()
            passed[j].start()
        copy(0, sibling, me).wait_recv()
        for j, chip in enumerate(chips):
            copy(4 + j, (*chip, 1 - c_), me).wait_recv()
        for cp in first + passed:
            cp.wait_send()
        mine.wait()

    return pl.pallas_call(
        body, name=name, out_shape=jax.ShapeDtypeStruct((N_DEV, M, N), x.dtype),
        in_specs=[ANY], out_specs=ANY,
        scratch_shapes=[pltpu.SemaphoreType.DMA((7,)), pltpu.SemaphoreType.DMA((7,)), pltpu.SemaphoreType.DMA],
    )(x)


def _pair_exchange(g, *, name):
    _, R, C = g.shape

    def body(g_ref, out_ref, send_sems, recv_sems):
        x_, y_, c_ = _place()
        copies = []
        for k in range(4):
            cp = pltpu.make_async_remote_copy(
                src_ref=g_ref.at[2 * k + (1 - c_)], dst_ref=out_ref.at[k], send_sem=send_sems.at[k],
                recv_sem=recv_sems.at[k], device_id=(x_, y_, 1 - c_), device_id_type=MESH)
            cp.start()
            copies.append(cp)
        for cp in copies:
            cp.wait()

    return pl.pallas_call(
        body, name=name, out_shape=jax.ShapeDtypeStruct((4, R, C), g.dtype), in_specs=[ANY], out_specs=ANY,
        scratch_shapes=[pltpu.SemaphoreType.DMA((4,)), pltpu.SemaphoreType.DMA((4,))],
    )(g)


def _chip_exchange(p, *, name):
    _, R, C = p.shape

    def body(p_ref, out_ref, send_sems, recv_sems):
        x_, y_, c_ = _place()
        chips = [(1 - x_, y_), (x_, 1 - y_), (1 - x_, 1 - y_)]
        copies = []
        for j, (cx, cy) in enumerate(chips):
            cp = pltpu.make_async_remote_copy(
                src_ref=p_ref.at[2 * cx + cy], dst_ref=out_ref.at[j], send_sem=send_sems.at[j],
                recv_sem=recv_sems.at[j], device_id=(cx, cy, c_), device_id_type=MESH)
            cp.start()
            copies.append(cp)
        for cp in copies:
            cp.wait()

    return pl.pallas_call(
        body, name=name, out_shape=jax.ShapeDtypeStruct((3, R, C), p.dtype), in_specs=[ANY], out_specs=ANY,
        scratch_shapes=[pltpu.SemaphoreType.DMA((3,)), pltpu.SemaphoreType.DMA((3,))],
    )(p)


def _add_tr(R, C):
    return _tile(R, tuple(t for t in (512, 256, 128, 64, 32, 16, 8) if t * C * 4 <= (2 << 20)))


def _pair_add(g, recv, *, name):
    _, R, C = g.shape
    tr = _add_tr(R, C)
    core = lax.axis_index("c").astype(jnp.int32).reshape(1)

    def body(c_ref, g_ref, r_ref, o_ref):
        o_ref[...] = (g_ref[...] + r_ref[...]).astype(o_ref.dtype)

    return pl.pallas_call(
        body, name=name, out_shape=jax.ShapeDtypeStruct((4, R, C), BF16),
        grid_spec=pltpu.PrefetchScalarGridSpec(
            num_scalar_prefetch=1, grid=(4, R // tr),
            in_specs=[pl.BlockSpec((None, tr, C), lambda k, i, c: (2 * k + c[0], i, 0)),
                      pl.BlockSpec((None, tr, C), lambda k, i, c: (k, i, 0))],
            out_specs=pl.BlockSpec((None, tr, C), lambda k, i, c: (k, i, 0))),
        compiler_params=_cparams(("parallel", "parallel")),
    )(core, g, recv)


def _chip_add(p, recv, *, name):
    _, R, C = p.shape
    tr = _add_tr(R, C)
    chip = (2 * lax.axis_index("x") + lax.axis_index("y")).astype(jnp.int32).reshape(1)

    def body(c_ref, p_ref, r0_ref, r1_ref, r2_ref, o_ref):
        f = lambda r: r[...].astype(F32)
        o_ref[...] = ((f(p_ref) + f(r0_ref)) + f(r1_ref)) + f(r2_ref)

    rspec = lambda j: pl.BlockSpec((None, tr, C), lambda i, c: (j, i, 0))
    return pl.pallas_call(
        body, name=name, out_shape=jax.ShapeDtypeStruct((R, C), F32),
        grid_spec=pltpu.PrefetchScalarGridSpec(
            num_scalar_prefetch=1, grid=(R // tr,),
            in_specs=[pl.BlockSpec((None, tr, C), lambda i, c: (c[0], i, 0)), rspec(0), rspec(1), rspec(2)],
            out_specs=pl.BlockSpec((tr, C), lambda i, c: (i, 0))),
        compiler_params=_cparams(("parallel",)),
    )(chip, p, recv, recv, recv)


def _reduce_scatter(g, *, name):
    p = _pair_add(g, _pair_exchange(g, name=name + "_pair"), name=name + "_pair_add")
    return _chip_add(p, _chip_exchange(p, name=name + "_chip"), name=name + "_chip_add")


def _sum8(a, *, name):
    _, R, C = a.shape
    tr = _tile(R, tuple(t for t in (512, 256, 128, 64, 32, 16, 8) if t * C * 4 * 8 <= (4 << 20)))

    def body(a_ref, o_ref):
        acc = a_ref[0]
        for d in range(1, N_DEV):
            acc = acc + a_ref[d]
        o_ref[...] = acc

    return pl.pallas_call(
        body, name=name, grid=(R // tr,), in_specs=[pl.BlockSpec((N_DEV, tr, C), lambda i: (0, i, 0))],
        out_specs=pl.BlockSpec((tr, C), lambda i: (i, 0)), out_shape=jax.ShapeDtypeStruct((R, C), F32),
        compiler_params=_cparams(("parallel",)),
    )(a)


def _s5_params(lam_re, lam_im, log_dt, b_re, b_im, c_re, c_im):
    lr = jnp.minimum(lam_re, C_MAX_RE)
    li = lam_im
    dt = jnp.exp(log_dt)[:, None]
    mag = jnp.exp(lr * dt)
    ar, ai = mag * jnp.cos(li * dt), mag * jnp.sin(li * dt)
    nr, ni, den = ar - 1.0, ai, lr * lr + li * li
    fr, fi = (nr * lr + ni * li) / den, (ni * lr - nr * li) / den
    bbr = fr[..., None] * b_re - fi[..., None] * b_im
    bbi = fr[..., None] * b_im + fi[..., None] * b_re
    eye = jnp.eye(S5_GB, dtype=F32)

    def dense_b(bb):
        t = bb.reshape(S5_GB, S5_GB, C_STATE, C_GROUP).transpose(0, 1, 3, 2)
        return (t[:, :, :, None, :] * eye[None, :, None, :, None]).reshape(S5_GB, S5_GB * C_GROUP, S5_GBW)

    def dense_c(cc):
        t = cc.reshape(S5_GB, S5_GB, C_GROUP, C_STATE).transpose(0, 1, 3, 2)
        return (t[:, :, :, None, :] * eye[None, :, None, :, None]).reshape(S5_GB, S5_GBW, S5_GB * C_GROUP)

    flat = lambda a: a.reshape(1, S5_LANES)
    return flat(ar), flat(ai), dense_b(bbr), dense_b(bbi), dense_c(c_re), dense_c(c_im)


def _rope_tables(n_ctx, n_lat):
    quarter = HEAD_DK // 4
    freqs = ROPE_BASE ** (-jnp.arange(quarter, dtype=F32) / quarter)
    t = jnp.arange(n_lat)
    r = (t // GRID_W).astype(F32)
    col = (t % GRID_W).astype(F32)
    ang = jnp.concatenate([r[:, None] * freqs, col[:, None] * freqs], axis=-1)
    ang = jnp.concatenate([jnp.zeros((n_ctx, HEAD_DK // 2), F32), ang], axis=0)
    cos, sin = jnp.cos(ang), jnp.sin(ang)
    idx = jnp.arange(HEAD_DK)
    swap = (idx[:, None] == (idx[None, :] + HEAD_DK // 2) % HEAD_DK).astype(F32)
    return jnp.concatenate([cos, cos], axis=1), jnp.concatenate([-sin, sin], axis=1), swap


W_IN = 13344
W_INP = 13440
_COL = dict(a_q=0, a_ff=1024, a_fb=2048, a_i=3072, a_g=4096, b_q=5120, b_k=5632, b_v=6144, b_g=7168,
            c_u=8192, c_g=9216, d_q=10240, d_k=10752, d_v=11264, d_g=12288, b_lr=13312)


_COL_ORDER = (("a_q", 1024), ("a_ff", 1024), ("a_fb", 1024), ("a_i", 1024), ("a_g", 1024), ("b_q", 512), ("b_k", 512),
              ("b_v", 1024), ("b_g", 1024), ("c_u", 1024), ("c_g", 1024), ("d_q", 512), ("d_k", 512), ("d_v", 1024),
              ("d_g", 1024), ("b_lr", 128))
_W_IN_MAP = ((0, 7168, 0), (7200, W_IN, 7168), (7168, 7200, 13312))
W_IN_SHARD = W_IN // N_DEV


def _w_in_from_shards(g):
    parts = []
    for lo, hi, _ in _W_IN_MAP:
        for d in range(N_DEV):
            a, b = max(lo, d * W_IN_SHARD), min(hi, (d + 1) * W_IN_SHARD)
            if a < b:
                parts.append(g[d, ..., a - d * W_IN_SHARD:b - d * W_IN_SHARD])
    parts.append(jnp.zeros(g.shape[1:-1] + (W_INP - W_IN,), g.dtype))
    return jnp.concatenate(parts, axis=-1)


def _w_in_to_shards(w):
    shards = []
    for d in range(N_DEV):
        parts = []
        for lo, hi, here in sorted(_W_IN_MAP):
            a, b = max(lo, d * W_IN_SHARD), min(hi, (d + 1) * W_IN_SHARD)
            if a < b:
                parts.append(w[..., here + a - lo:here + b - lo])
        shards.append(jnp.concatenate(parts, axis=-1))
    return jnp.stack(shards)


@jax.custom_vjp
def _split_cols(proj):
    return tuple(lax.slice_in_dim(proj, _COL[n], _COL[n] + w, axis=1) for n, w in _COL_ORDER)


def _split_cols_fwd(proj):
    return _split_cols(proj), None


def _split_cols_bwd(_, cts):
    return (jnp.concatenate(cts, axis=1),)


_split_cols.defvjp(_split_cols_fwd, _split_cols_bwd)


def _layer_ops(l, n_ctx):
    n = f"l{l}_"
    rw = functools.partial(_rowwise, n_ctx=n_ctx)
    wide = [(BRANCH_W, F32)]
    return dict(
        modnorm=rw(_modnorm_fn, ["row", "const", "seg", "seg"], [(4096, BF16)], ncb=1, tr=128, name=n + "modnorm"),
        resid=rw(_resid_fn, ["row", "row", "seg"], [(4096, F32)], ncb=1, tr=128, name=n + "resid"),
        lin_in=_linear(n + "lin_in"), lin_out=_linear(n + "lin_out"), lin_glu=_linear(n + "lin_glu"),
        hgrn_pro=rw(_hgrn_pro_fn, ["row", "row", "const", "const"], wide * 4, ncb=1, tr=256, name=n + "hgrn_pro"),
        hgrn_f=_gla(n + "hgrn_f", flip=False, n_ctx=n_ctx), hgrn_b=_gla(n + "hgrn_b", flip=True, n_ctx=n_ctx),
        hgrn_epi=rw(_hgrn_epi_fn, ["row", "row", "row", "const"], [(BRANCH_W, BF16)], ncb=8, tr=256, name=n + "hgrn_epi"),
        gla_pro=rw(_gla_pro_fn, ["row", "row", "const", "const", "const", "const"], [(512, F32)] * 3, ncb=1, tr=256,
                   name=n + "gla_pro"),
        gla_f=_gla(n + "gla_f", flip=False, n_ctx=n_ctx), gla_b=_gla(n + "gla_b", flip=True, n_ctx=n_ctx),
        gla_epi=rw(_hgrn_epi_fn, ["row", "row", "row", "const"], [(BRANCH_W, BF16)], ncb=4, tr=256, name=n + "gla_epi"),
        s5_f=_s5(n + "s5_f", flip=False, n_ctx=n_ctx), s5_b=_s5(n + "s5_b", flip=True, n_ctx=n_ctx),
        s5_mid=rw(_s5_mid_fn, ["row", "row", "row", "const"], wide, ncb=1, tr=256, name=n + "s5_mid"),
        s5_epi=rw(_s5_epi_fn, ["row", "row", "row", "const"], [(BRANCH_W, BF16)], ncb=1, tr=256, name=n + "s5_epi"),
        ret_pro=rw(_ret_pro_fn, ["row", "row", "rowc", "rowc", "col", "col", "fixed"], [(512, F32)] * 4, ncb=4, tr=256,
                   name=n + "ret_pro"),
        ret_f=_gla(n + "ret_f", flip=False, n_ctx=n_ctx), ret_b=_gla(n + "ret_b", flip=True, n_ctx=n_ctx),
        ret_epi=rw(_ret_epi_fn, ["row", "row", "row", "const"], [(BRANCH_W, BF16)], ncb=4, tr=256, name=n + "ret_epi"),
    )


def _layer(ops, h, p, w_in, w_out, w_glu, lb, rope):
    cos2, sin2, swap = rope
    (hn,) = ops["modnorm"](h, p["norm_g"], p["sc"], p["sh"])
    proj = ops["lin_in"](hn, w_in, p["shadow_in"])
    cols = dict(zip([n for n, _ in _COL_ORDER], _split_cols(proj)))
    col = lambda name, w: cols[name]

    a_q, a_i = col("a_q", 1024), col("a_i", 1024)
    kf, gf, kb, gb = ops["hgrn_pro"](col("a_ff", 1024), col("a_fb", 1024), lb[0:1], lb[1:2])
    (o_a,) = ops["hgrn_epi"](ops["hgrn_f"](a_q, kf, a_i, gf), ops["hgrn_b"](a_q, kb, a_i, gb), col("a_g", 1024),
                             p["hgrn_norm_g"])

    rank = B_GATE_RANK
    wf = jnp.pad(p["gla_w_gk"][0], ((0, 128 - rank), (0, 0)))
    wb = jnp.pad(p["gla_w_gk"][1], ((rank, 128 - 2 * rank), (0, 0)))
    b_k, b_v = col("b_k", 512), col("b_v", 1024)
    qs, gf, gb = ops["gla_pro"](col("b_q", 512), col("b_lr", 128), wf, wb, p["gla_b_gk"][0:1], p["gla_b_gk"][1:2])
    (o_b,) = ops["gla_epi"](ops["gla_f"](qs, b_k, b_v, gf), ops["gla_b"](qs, b_k, b_v, gb), col("b_g", 1024),
                            p["gla_norm_g"])

    c_u = col("c_u", 1024)
    s5p = lambda d: _s5_params(*[p[k][d] for k in ("s5_lam_re", "s5_lam_im", "s5_log_dt", "s5_b_re", "s5_b_im",
                                                  "s5_c_re", "s5_c_im")])
    (z,) = ops["s5_mid"](ops["s5_f"](c_u, *s5p(0)), ops["s5_b"](c_u, *s5p(1)), c_u, p["s5_d"].reshape(1, BRANCH_W))
    t = ops["lin_glu"](z, w_glu, p["shadow_glu"])
    (o_c,) = ops["s5_epi"](z, t, col("c_g", 1024), p["s5_b_glu"])

    log_gamma = _log_sigmoid(p["ret_decay_logit"])
    lg = lambda d: jnp.broadcast_to(log_gamma[d][:, None, None], (4, 1, HEAD_DK))
    d_v = col("d_v", 1024)
    qs, ks, gf, gb = ops["ret_pro"](col("d_q", 512), col("d_k", 512), cos2, sin2, lg(0), lg(1), swap)
    (o_d,) = ops["ret_epi"](ops["ret_f"](qs, ks, d_v, gf), ops["ret_b"](qs, ks, d_v, gb), col("d_g", 1024),
                            p["ret_norm_g"])

    o = jnp.concatenate([o_a, o_b, o_c, o_d], axis=1)
    y = ops["lin_out"](o, w_out, p["shadow_out"])
    (h,) = ops["resid"](h, y, p["gt"])
    return h


_SMALL = ("norm_g", "hgrn_lb_logits", "hgrn_norm_g", "gla_w_gk", "gla_b_gk", "gla_norm_g", "s5_lam_re", "s5_lam_im",
          "s5_log_dt", "s5_b_re", "s5_b_im", "s5_c_re", "s5_c_im", "s5_d", "s5_b_glu", "ret_decay_logit", "ret_norm_g",
          "final_norm_g")
_WEIGHTS = ("c_ctx", "norm_g", "w_ada", "b_ada", "w_in", "hgrn_lb_logits", "hgrn_norm_g", "gla_w_gk", "gla_b_gk",
            "gla_norm_g", "s5_lam_re", "s5_lam_im", "s5_log_dt", "s5_b_re", "s5_b_im", "s5_c_re", "s5_c_im", "s5_d",
            "s5_w_glu", "s5_b_glu", "ret_decay_logit", "ret_norm_g", "w_out", "final_norm_g")


def _pad_rows(flat, mult=8):
    rows = -(-flat.size // 128)
    rows = -(-rows // mult) * mult
    return jnp.pad(flat, (0, rows * 128 - flat.size)).reshape(rows, 128)


def _unshard_cols(g, inner):
    k = len(inner)
    a = g.reshape((N_DEV,) + tuple(inner))
    return jnp.moveaxis(a, 0, k - 1).reshape(tuple(inner[:-1]) + (N_DEV * inner[-1],))


def kernel(x, c, ctx, c_ctx, norm_g, w_ada, b_ada, w_in, hgrn_lb_logits, hgrn_norm_g, gla_w_gk, gla_b_gk, gla_norm_g, s5_lam_re, s5_lam_im, s5_log_dt, s5_b_re, s5_b_im, s5_c_re, s5_c_im, s5_d, s5_w_glu, s5_b_glu, ret_decay_logit, ret_norm_g, w_out, final_norm_g, loss_target, m_c_ctx, m_norm_g, m_w_ada, m_b_ada, m_w_in, m_hgrn_lb_logits, m_hgrn_norm_g, m_gla_w_gk, m_gla_b_gk, m_gla_norm_g, m_s5_lam_re, m_s5_lam_im, m_s5_log_dt, m_s5_b_re, m_s5_b_im, m_s5_c_re, m_s5_c_im, m_s5_d, m_s5_w_glu, m_s5_b_glu, m_ret_decay_logit, m_ret_norm_g, m_w_out, m_final_norm_g, v_c_ctx, v_norm_g, v_w_ada, v_b_ada, v_w_in, v_hgrn_lb_logits, v_hgrn_norm_g, v_gla_w_gk, v_gla_b_gk, v_gla_norm_g, v_s5_lam_re, v_s5_lam_im, v_s5_log_dt, v_s5_b_re, v_s5_b_im, v_s5_c_re, v_s5_c_im, v_s5_d, v_s5_w_glu, v_s5_b_glu, v_ret_decay_logit, v_ret_norm_g, v_w_out, v_final_norm_g):
    given = dict(locals())
    depth = norm_g.shape[0]
    n_ctx, n_lat, d_model = ctx.shape[1], x.shape[1], x.shape[2]
    me = 4 * lax.axis_index("x") + 2 * lax.axis_index("y") + lax.axis_index("c")
    w_ada_cols = w_ada.shape[2]

    pack = jnp.concatenate([c.reshape(-1), hgrn_lb_logits.reshape(-1), gla_w_gk.reshape(-1), gla_b_gk.reshape(-1)])
    got = _all_gather(_pad_rows(pack), name="gather_small").reshape(N_DEV, -1)
    o0 = d_model
    o1 = o0 + hgrn_lb_logits.size
    o2 = o1 + gla_w_gk.size
    o3 = o2 + gla_b_gk.size
    c_all = got[:, :o0]
    lb_logits = _unshard_cols(got[:, o0:o1], hgrn_lb_logits.shape)
    w_gk = _unshard_cols(got[:, o1:o2], gla_w_gk.shape)
    b_gk = _unshard_cols(got[:, o2:o3], gla_b_gk.shape)

    cond = jnp.concatenate([c_all, c_ctx[None, :], jnp.zeros((7, d_model), F32)], axis=0)
    act = _silu(cond).astype(BF16)
    mod_part = jnp.concatenate([_mm(act, w_ada[l], name=f"ada{l}_fwd") for l in range(depth)], axis=0)
    mod = _all_gather(mod_part, name="gather_mod").reshape(N_DEV, depth, 16, w_ada_cols)
    mod = jnp.moveaxis(mod, 0, 2).reshape(depth, 16, N_DEV * w_ada_cols) + b_ada[:, None, :]
    mod_lat = lax.dynamic_index_in_dim(mod, me, axis=1, keepdims=False)
    mod_ctx = mod[:, 8]
    seg = lambda l, j: jnp.stack([mod_ctx[l, j * d_model:(j + 1) * d_model],
                                  mod_lat[l, j * d_model:(j + 1) * d_model]])[:, None, :]

    def gathered(w, name):
        flat = w.astype(BF16).reshape(-1, w.shape[-1])
        return _all_gather(flat, name=name).reshape((N_DEV,) + w.shape)

    w_in_full = _w_in_from_shards(gathered(w_in, "gather_w_in"))
    w_out_full = jnp.moveaxis(gathered(w_out, "gather_w_out"), 0, 1).reshape(depth, 4 * BRANCH_W, d_model)
    w_glu_full = jnp.moveaxis(gathered(s5_w_glu, "gather_w_glu"), 0, 1).reshape(depth, BRANCH_W, BRANCH_W)

    rope = _rope_tables(n_ctx, n_lat)
    ops = [_layer_ops(l, n_ctx) for l in range(depth)]
    params = dict(
        layers=[dict(
            norm_g=norm_g[l][None, :], sh=seg(l, 0), sc=seg(l, 1), gt=seg(l, 2),
            shadow_in=jnp.zeros((d_model, W_INP), F32), shadow_out=jnp.zeros((4 * BRANCH_W, d_model), F32),
            shadow_glu=jnp.zeros((BRANCH_W, BRANCH_W), F32),
            hgrn_norm_g=hgrn_norm_g[l][None, :], gla_w_gk=w_gk[l], gla_b_gk=b_gk[l], gla_norm_g=gla_norm_g[l][None, :],
            s5_lam_re=s5_lam_re[l], s5_lam_im=s5_lam_im[l], s5_log_dt=s5_log_dt[l], s5_b_re=s5_b_re[l],
            s5_b_im=s5_b_im[l], s5_c_re=s5_c_re[l], s5_c_im=s5_c_im[l], s5_d=s5_d[l], s5_b_glu=s5_b_glu[l][None, :],
            ret_decay_logit=ret_decay_logit[l], ret_norm_g=ret_norm_g[l][None, :]) for l in range(depth)],
        lb_logits=lb_logits,
        h0=jnp.concatenate([ctx[0], x[0]], axis=0))

    def body(p):
        lb_p = jax.nn.softmax(p["lb_logits"], axis=0)
        lower = jnp.cumsum(lb_p, axis=0) - lb_p[0:1]
        h = p["h0"]
        for l in range(depth):
            h = _layer(ops[l], h, p["layers"][l], w_in_full[l], w_out_full[l], w_glu_full[l], lower[l], rope)
        return h

    h_out, pullback = jax.vjp(body, params)
    dh, dg_final, loss_part = _loss_call(h_out, loss_target[0], final_norm_g[None, :], n_ctx=n_ctx, name="loss_head")
    (grads,) = pullback(dh)
    loss = lax.psum(loss_part[0, 0], ("x", "y", "c"))
    gl = grads["layers"]

    dmod = jnp.stack([jnp.concatenate([gl[l][k][s, 0] for k in ("sh", "sc", "gt")])
                      for l in range(depth) for s in (1, 0)])
    dmod = jnp.pad(dmod, ((0, 8 - 2 * depth), (0, 0)))
    dmod_all = _all_gather(dmod, name="gather_dmod")
    dmod_sum = _sum8(dmod_all, name="sum_dmod")
    grad_b_ada = jnp.stack([dmod_sum[2 * l] + dmod_sum[2 * l + 1] for l in range(depth)])
    grad_w_ada, ds_ctx = [], jnp.zeros((d_model,), F32)
    for l in range(depth):
        rows = jnp.concatenate([dmod_all[:, 2 * l], dmod_sum[2 * l + 1][None, :], jnp.zeros((7, 3 * d_model), F32)])
        mine = lax.dynamic_slice_in_dim(rows, me * w_ada_cols, w_ada_cols, axis=1)
        grad_w_ada.append(_mm(act.T, mine, name=f"ada{l}_dw"))
        ds_ctx = ds_ctx + _mm(mine, w_ada[l], nt=True, name=f"ada{l}_da")[8]
    grad_w_ada = jnp.stack(grad_w_ada)

    def scattered(g, axis, name):
        parts = g.reshape(g.shape[:axis] + (N_DEV, g.shape[axis] // N_DEV) + g.shape[axis + 1:])
        parts = jnp.moveaxis(parts, axis, 0)
        local = parts.shape[1:]
        return _reduce_scatter(parts.reshape(N_DEV, -1, local[-1]), name=name).reshape(local)

    w_in_slabs = _w_in_to_shards(jnp.stack([gl[l]["shadow_in"] for l in range(depth)]))
    grad_w_in = _reduce_scatter(w_in_slabs.reshape(N_DEV, depth * d_model, W_IN_SHARD), name="rs_w_in").reshape(w_in.shape)
    grad_w_out = scattered(jnp.stack([gl[l]["shadow_out"] for l in range(depth)]), 1, "rs_w_out")
    grad_w_glu = scattered(jnp.stack([gl[l]["shadow_glu"] for l in range(depth)]), 1, "rs_w_glu")

    stack = lambda k, shape=None: jnp.stack([gl[l][k] for l in range(depth)]).reshape(shape or given[k].shape)
    small = dict(norm_g=stack("norm_g"), hgrn_lb_logits=grads["lb_logits"], hgrn_norm_g=stack("hgrn_norm_g"),
                 gla_w_gk=stack("gla_w_gk", w_gk.shape), gla_b_gk=stack("gla_b_gk", b_gk.shape),
                 gla_norm_g=stack("gla_norm_g"), s5_d=stack("s5_d"), s5_b_glu=stack("s5_b_glu"),
                 ret_decay_logit=stack("ret_decay_logit"), ret_norm_g=stack("ret_norm_g"), final_norm_g=dg_final[0])
    for k in ("s5_lam_re", "s5_lam_im", "s5_log_dt", "s5_b_re", "s5_b_im", "s5_c_re", "s5_c_im"):
        small[k] = stack(k)
    pieces = [small[k] for k in _SMALL] + [ds_ctx]
    packed = jnp.concatenate([_pad_rows(a.reshape(-1), mult=1) for a in pieces], axis=0)
    packed = jnp.pad(packed, ((0, -packed.shape[0] % 8), (0, 0)))
    total = _sum8(_all_gather(packed, name="gather_small_grads"), name="sum_small_grads")
    reduced, off = {}, 0
    for k, a in zip(_SMALL + ("ds_ctx",), pieces):
        rows = -(-a.size // 128)
        reduced[k] = total[off:off + rows].reshape(-1)[:a.size].reshape(a.shape)
        off += rows

    def my_cols(a):
        w = a.shape[-1] // N_DEV
        return lax.dynamic_slice_in_dim(a, me * w, w, axis=a.ndim - 1)

    dsilu = jax.vjp(_silu, c_ctx)[1](reduced.pop("ds_ctx"))[0]
    grad = dict(reduced, c_ctx=dsilu, w_ada=grad_w_ada, b_ada=grad_b_ada, w_in=grad_w_in, w_out=grad_w_out,
                s5_w_glu=grad_w_glu)
    for k in ("hgrn_lb_logits", "gla_w_gk", "gla_b_gk"):
        grad[k] = my_cols(grad[k])

    delta, new_m, new_v = {}, {}, {}
    for k in _WEIGHTS:
        delta[k], new_m[k], new_v[k] = _adamw(given[k], grad[k], given["m_" + k], given["v_" + k], name="adamw_" + k)
    grad_x = grads["h0"][n_ctx:][None]
    return (loss, grad_x, *[grad[k] for k in _WEIGHTS], *[delta[k] for k in _WEIGHTS],
            *[new_m[k] for k in _WEIGHTS], *[new_v[k] for k in _WEIGHTS])
```

```python
import functools
import math

import jax
import jax.numpy as jnp
from jax import lax
from jax.experimental import pallas as pl
from jax.experimental.pallas import tpu as pltpu

F32 = jnp.float32
BF16 = jnp.bfloat16

EPS = 1e-6
BRANCH_W = 1024
CHUNK = 64
HEAD_DK = 128
A_MIN_FORGET = 1e-6
B_GATE_NORM = 16.0
B_GATE_RANK = 16
C_GROUP = 16
C_GROUPS = 64
C_STATE = 64
C_MAX_RE = -1e-4
GRID_W = 64
ROPE_BASE = 10000.0
N_DEV = 8

ADAM_LR = 0.001
ADAM_B1 = 0.9
ADAM_B2 = 0.999
ADAM_EPS = 1e-08
ADAM_WD = 0.01
ADAM_STEP = 10

VMEM_LIMIT = 52 * 1024 * 1024
SCAN_ROWS = 256
S5_ROWS = 128
S5_GB = 8
S5_GBW = S5_GB * C_STATE


def _tile(dim, cands):
    for c in cands:
        if dim % c == 0:
            return c
    return dim


def _cparams(sem):
    return pltpu.CompilerParams(dimension_semantics=sem, vmem_limit_bytes=VMEM_LIMIT)


MM_VMEM_BUDGET = 40 * 1024 * 1024


def _mm_tiles(M, N, K, sa, sb, so):
    tms = [t for t in (1088, 1024, 544, 512) if M % t == 0] or [M]
    tns = [t for t in (1024, 896, 640, 512, 384, 256) if N % t == 0] or [N]
    tks = [t for t in (K, 4480, 4352, 4096, 2688, 2176, 2048, 1920, 1024, 896, 640, 512, 384, 256, 128) if K % t == 0]
    fits = lambda tm, tn, tk: (2 * (tm * tk * sa + tk * tn * sb) + (tm * tn * 4 if tk < K else 0)
                               + 2 * tm * tn * so <= MM_VMEM_BUDGET)
    for tm in tms:
        for tn in tns:
            if fits(tm, tn, K):
                return tm, tn, K
    for tm in tms:
        for tn in tns:
            for tk in tks:
                if fits(tm, tn, tk):
                    return tm, tn, tk
    return tms[-1], tns[-1], tks[-1]


def _mm(a, b, *, nt=False, out_dtype=F32, name):
    M, K = a.shape
    N = b.shape[0] if nt else b.shape[1]
    tm, tn, tk = _mm_tiles(M, N, K, a.dtype.itemsize, b.dtype.itemsize, jnp.dtype(out_dtype).itemsize)
    nk = K // tk
    dims = (((1,), (1,)), ((), ())) if nt else (((1,), (0,)), ((), ()))

    if nk == 1:
        def body1(a_ref, b_ref, o_ref):
            o_ref[...] = lax.dot_general(a_ref[...].astype(BF16), b_ref[...].astype(BF16), dims,
                                         preferred_element_type=F32).astype(o_ref.dtype)

        b_spec1 = (pl.BlockSpec((tn, K), lambda i, j: (j, 0)) if nt else pl.BlockSpec((K, tn), lambda i, j: (0, j)))
        return pl.pallas_call(
            body1, name=name, grid=(M // tm, N // tn),
            in_specs=[pl.BlockSpec((tm, K), lambda i, j: (i, 0)), b_spec1],
            out_specs=pl.BlockSpec((tm, tn), lambda i, j: (i, j)),
            out_shape=jax.ShapeDtypeStruct((M, N), out_dtype),
            compiler_params=_cparams(("parallel", "arbitrary")),
        )(a, b)

    def body(a_ref, b_ref, o_ref, acc_ref):
        k = pl.program_id(2)

        @pl.when(k == 0)
        def _():
            acc_ref[...] = jnp.zeros_like(acc_ref)

        acc_ref[...] += lax.dot_general(a_ref[...].astype(BF16), b_ref[...].astype(BF16), dims,
                                        preferred_element_type=F32)

        @pl.when(k == nk - 1)
        def _():
            o_ref[...] = acc_ref[...].astype(o_ref.dtype)

    b_spec = (pl.BlockSpec((tn, tk), lambda i, j, k: (j, k)) if nt
              else pl.BlockSpec((tk, tn), lambda i, j, k: (k, j)))
    return pl.pallas_call(
        body, name=name, grid=(M // tm, N // tn, nk),
        in_specs=[pl.BlockSpec((tm, tk), lambda i, j, k: (i, k)), b_spec],
        out_specs=pl.BlockSpec((tm, tn), lambda i, j, k: (i, j)),
        out_shape=jax.ShapeDtypeStruct((M, N), out_dtype),
        scratch_shapes=[pltpu.VMEM((tm, tn), F32)],
        compiler_params=_cparams(("parallel", "parallel", "arbitrary")),
    )(a, b)


def _linear(name):
    @jax.custom_vjp
    def op(a, w, shadow):
        return _mm(a, w, name=name + "_fwd")

    def fwd(a, w, shadow):
        return _mm(a, w, name=name + "_fwd"), (a, w)

    def bwd(res, dy):
        a, w = res
        dy = dy.astype(BF16)
        da = _mm(dy, w, nt=True, out_dtype=a.dtype, name=name + "_da")
        dw = _mm(a.T, dy, name=name + "_dw")
        return da, jnp.zeros_like(w), dw

    op.defvjp(fwd, bwd)
    return op


def _mm_gather(a, b, side, *, name):
    M, K = a.shape
    N = b.shape[1]
    tm, tn, tk = _mm_tiles(M, N, K, a.dtype.itemsize, b.dtype.itemsize, 4)
    assert tk == K
    gi, gj = M // tm, N // tn
    last = gi * gj - 1
    mid = last // 2

    def body(a_ref, b_ref, side_ref, o_ref, got_ref, send_sems, recv_sems, local_sem):
        step = pl.program_id(0) * gj + pl.program_id(1)
        x_, y_, c_ = _place()
        me, sibling = (x_, y_, c_), (x_, y_, 1 - c_)
        chips = [(1 - x_, y_), (x_, 1 - y_), (1 - x_, 1 - y_)]

        def slab(px, py, pc):
            return got_ref.at[4 * px + 2 * py + pc]

        def copy(k, block, to, src=None):
            return pltpu.make_async_remote_copy(
                src_ref=slab(*block) if src is None else src, dst_ref=slab(*block),
                send_sem=send_sems.at[k], recv_sem=recv_sems.at[k], device_id=to, device_id_type=MESH)

        mine = pltpu.make_async_copy(side_ref, slab(*me), local_sem)
        first = [copy(0, me, sibling, src=side_ref)] + [copy(1 + j, me, (*chip, c_), src=side_ref)
                                                         for j, chip in enumerate(chips)]
        passed = [copy(4 + j, (*chip, c_), sibling) for j, chip in enumerate(chips)]

        @pl.when(step == 0)
        def _():
            mine.start()
            for cp in first:
                cp.start()

        o_ref[...] = jnp.dot(a_ref[...].astype(BF16), b_ref[...].astype(BF16), preferred_element_type=F32)

        @pl.when(step == mid)
        def _():
            for j, chip in enumerate(chips):
                copy(1 + j, (*chip, c_), me).wait_recv()
                passed[j].start()

        @pl.when(step == last)
        def _():
            copy(0, sibling, me).wait_recv()
            for j, chip in enumerate(chips):
                copy(4 + j, (*chip, 1 - c_), me).wait_recv()
            for cp in first + passed:
                cp.wait_send()
            mine.wait()

    return pl.pallas_call(
        body, name=name, grid=(gi, gj),
        in_specs=[pl.BlockSpec((tm, K), lambda i, j: (i, 0)), pl.BlockSpec((K, tn), lambda i, j: (0, j)), ANY],
        out_specs=[pl.BlockSpec((tm, tn), lambda i, j: (i, j)), ANY],
        out_shape=[jax.ShapeDtypeStruct((M, N), F32), jax.ShapeDtypeStruct((N_DEV,) + side.shape, side.dtype)],
        scratch_shapes=[pltpu.SemaphoreType.DMA((7,)), pltpu.SemaphoreType.DMA((7,)), pltpu.SemaphoreType.DMA],
        compiler_params=_cparams(("arbitrary", "arbitrary")),
    )(a, b, side)


def _linear_gather(name):
    @jax.custom_vjp
    def op(a, w, shadow, side):
        return tuple(_mm_gather(a, w, side, name=name + "_fwd"))

    def fwd(a, w, shadow, side):
        return tuple(_mm_gather(a, w, side, name=name + "_fwd")), (a, w, side)

    def bwd(res, cts):
        a, w, side = res
        dy = cts[0].astype(BF16)
        da = _mm(dy, w, nt=True, out_dtype=a.dtype, name=name + "_da")
        dw = _mm(a.T, dy, name=name + "_dw")
        return da, jnp.zeros_like(w), dw, jnp.zeros_like(side)

    op.defvjp(fwd, bwd)
    return op


def _rowwise(fn, specs, out_defs, *, ncb, tr, n_ctx, name):
    ncx = n_ctx // tr
    diff = [k for k, s in enumerate(specs) if s in ("row", "const", "col", "seg")]

    def in_spec(s, arr):
        if s == "row":
            return pl.BlockSpec((tr, arr.shape[1] // ncb), lambda j, i: (i, j))
        if s == "rowc":
            return pl.BlockSpec((tr, arr.shape[1]), lambda j, i: (i, 0))
        if s in ("const", "fixed"):
            return pl.BlockSpec(arr.shape, lambda j, i: (0, 0))
        if s == "col":
            return pl.BlockSpec((None,) + arr.shape[1:], lambda j, i: (j, 0, 0))
        if s == "seg":
            return pl.BlockSpec((None,) + arr.shape[1:], lambda j, i: ((i >= ncx).astype(jnp.int32), 0, 0))
        raise ValueError(s)

    def fwd_call(*arrays):
        L = arrays[specs.index("row")].shape[0]
        n_in = len(arrays)

        def body(*refs):
            outs = fn(*[r[...] for r in refs[:n_in]])
            for o_ref, o in zip(refs[n_in:], outs):
                o_ref[...] = o.astype(o_ref.dtype)

        return pl.pallas_call(
            body, name=name + "_fwd", grid=(ncb, L // tr),
            in_specs=[in_spec(s, a) for s, a in zip(specs, arrays)],
            out_specs=[pl.BlockSpec((tr, c // ncb), lambda j, i: (i, j)) for c, _ in out_defs],
            out_shape=[jax.ShapeDtypeStruct((L, c), dt) for c, dt in out_defs],
            compiler_params=_cparams(("parallel", "arbitrary")),
        )(*arrays)

    def bwd_call(arrays, cts):
        L = arrays[specs.index("row")].shape[0]
        n_in, n_ct = len(arrays), len(cts)

        def body(*refs):
            i = pl.program_id(1)
            vals = [r[...] for r in refs[:n_in]]
            ct_vals = tuple(r[...] for r in refs[n_in:n_in + n_ct])
            g_refs = refs[n_in + n_ct:]

            def f(*dv):
                full = list(vals)
                for k, v in zip(diff, dv):
                    full[k] = v
                return tuple(o.astype(dt) for o, (_, dt) in zip(fn(*full), out_defs))

            _, vjp = jax.vjp(f, *[vals[k] for k in diff])
            grads = vjp(ct_vals)
            for k, g_ref, g in zip(diff, g_refs, grads):
                if specs[k] == "row":
                    g_ref[...] = g.astype(g_ref.dtype)
                else:
                    first = (i == 0) | (i == ncx) if specs[k] == "seg" else (i == 0)

                    @pl.when(first)
                    def _(g_ref=g_ref, g=g):
                        g_ref[...] = g

                    @pl.when(jnp.logical_not(first))
                    def _(g_ref=g_ref, g=g):
                        g_ref[...] += g

        out_specs, out_shape = [], []
        for k in diff:
            s, a = specs[k], arrays[k]
            if s == "row":
                out_specs.append(pl.BlockSpec((tr, a.shape[1] // ncb), lambda j, i: (i, j)))
                out_shape.append(jax.ShapeDtypeStruct(a.shape, a.dtype))
            elif s == "const":
                out_specs.append(pl.BlockSpec((None,) + a.shape, lambda j, i: (j, 0, 0)))
                out_shape.append(jax.ShapeDtypeStruct((ncb,) + a.shape, F32))
            elif s == "col":
                out_specs.append(pl.BlockSpec((None,) + a.shape[1:], lambda j, i: (j, 0, 0)))
                out_shape.append(jax.ShapeDtypeStruct(a.shape, F32))
            else:
                out_specs.append(pl.BlockSpec((None, None) + a.shape[1:],
                                              lambda j, i: (j, (i >= ncx).astype(jnp.int32), 0, 0)))
                out_shape.append(jax.ShapeDtypeStruct((ncb,) + a.shape, F32))
        ct_specs = [pl.BlockSpec((tr, c // ncb), lambda j, i: (i, j)) for c, _ in out_defs]
        grads = pl.pallas_call(
            body, name=name + "_bwd", grid=(ncb, L // tr),
            in_specs=[in_spec(s, a) for s, a in zip(specs, arrays)] + ct_specs,
            out_specs=out_specs, out_shape=out_shape,
            compiler_params=_cparams(("parallel", "arbitrary")),
        )(*arrays, *cts)
        full = [jnp.zeros_like(a) for a in arrays]
        for k, g in zip(diff, grads):
            full[k] = jnp.sum(g, axis=0) if specs[k] in ("const", "seg") else g
        return tuple(full)

    @jax.custom_vjp
    def op(*arrays):
        return tuple(fwd_call(*arrays))

    def op_fwd(*arrays):
        return tuple(fwd_call(*arrays)), arrays

    def op_bwd(arrays, cts):
        return bwd_call(arrays, cts)

    op.defvjp(op_fwd, op_bwd)
    return op


def _sigmoid(x):
    return 1.0 / (1.0 + jnp.exp(-x))


def _log_sigmoid(x):
    return jnp.minimum(x, 0.0) - jnp.log(1.0 + jnp.exp(-jnp.abs(x)))


def _silu(x):
    return x * _sigmoid(x)


def _rms(x, g):
    return x * lax.rsqrt(jnp.mean(x * x, axis=-1, keepdims=True) + EPS) * g


def _modnorm_fn(h, g, sc, sh):
    return ((_rms(h, g) * (1.0 + sc) + sh).astype(BF16),)


def _resid_fn(h, y, gt):
    return (h + gt * y,)


def _hgrn_pro_fn(zf, zb, lbf, lbb):
    def one(z, lb):
        f = lb + (1.0 - lb) * _sigmoid(z)
        return (1.0 - lb) * _sigmoid(-z), jnp.log(jnp.maximum(f, A_MIN_FORGET))

    kf, gf = one(zf, lbf)
    kb, gb = one(zb, lbb)
    return kf, gf, kb, gb


def _hgrn_epi_fn(of, ob, gate, g):
    return ((_rms(of + ob, g) * _silu(gate)).astype(BF16),)


def _gla_pro_fn(q, lr, wf, wb, bf, bb):
    gf = _log_sigmoid(jnp.dot(lr, wf, preferred_element_type=F32) + bf) / B_GATE_NORM
    gb = _log_sigmoid(jnp.dot(lr, wb, preferred_element_type=F32) + bb) / B_GATE_NORM
    return q * (HEAD_DK ** -0.5), gf, gb


def _ret_pro_fn(q, k, cos2, sin2, lgf, lgb, swap):
    def rope(x):
        return x * cos2 + jnp.dot(x, swap, preferred_element_type=F32, precision=lax.Precision.HIGHEST) * sin2

    zero = jnp.zeros_like(q)
    return rope(q) * (HEAD_DK ** -0.5), rope(k), zero + lgf, zero + lgb


def _ret_epi_fn(of, ob, gate, g):
    x = of + ob
    mu = jnp.mean(x, axis=-1, keepdims=True)
    var = jnp.mean(jnp.square(x - mu), axis=-1, keepdims=True)
    return (((x - mu) * lax.rsqrt(var + EPS) * g * _silu(gate)).astype(BF16),)


def _s5_mid_fn(yf, yb, u, d):
    return (jax.nn.gelu(yf + yb + d * u),)


def _s5_epi_fn(z, t, gate, b):
    return ((z * _sigmoid(t + b) * _silu(gate)).astype(BF16),)


def _blk(mode, i, nb, nc):
    if mode == 0:
        return i
    if mode == 1:
        return nb - 1 - i
    if mode == 2:
        return jnp.where(i < nc, nc - 1 - i, nb - 1 - i + nc)
    return jnp.where(i < nb - nc, i + nc, nc - nb + i)


def _split3(x):
    hi = x.astype(BF16)
    r1 = x - hi.astype(F32)
    mid = r1.astype(BF16)
    lo = (r1 - mid.astype(F32)).astype(BF16)
    return hi, mid, lo


def _cum(tri, x):
    hi, mid, lo = _split3(x)
    d = lambda v: jnp.dot(tri, v, preferred_element_type=F32)
    return d(hi) + d(mid) + d(lo)


def _dot(a, b):
    return jnp.dot(a.astype(BF16), b.astype(BF16), preferred_element_type=F32)


def _dot_nt(a, b):
    return lax.dot_general(a.astype(BF16), b.astype(BF16), (((1,), (1,)), ((), ())), preferred_element_type=F32)


def _dot_tn(a, b):
    return jnp.dot(a.astype(F32).T.astype(BF16), b.astype(BF16), preferred_element_type=F32)


def _dot3(a, b):
    ah = a.astype(BF16)
    al = (a - ah.astype(F32)).astype(BF16)
    bh = b.astype(BF16)
    bl = (b - bh.astype(F32)).astype(BF16)
    d = lambda p, q: jnp.dot(p, q, preferred_element_type=F32)
    return d(ah, bh) + d(ah, bl) + d(al, bh)


def _dot3_tn(a, b):
    return _dot3(a.astype(F32).T, b)


def _gla_masks(flip):
    r = lax.broadcasted_iota(jnp.int32, (CHUNK, CHUNK), 0)
    c = lax.broadcasted_iota(jnp.int32, (CHUNK, CHUNK), 1)
    causal = (c >= r) if flip else (c <= r)
    tri = causal.astype(BF16)
    tri_t = jnp.logical_not(causal) | (r == c)
    return causal, tri, tri_t.astype(BF16)


def _gla_chunk_terms(q, k, g, tri, flip):
    b = _cum(tri, g)
    mid = CHUNK // 2
    bm = b[mid:mid + 1, :]
    bl = b[0:1, :] if flip else b[CHUNK - 1:CHUNK, :]
    e_q = jnp.exp(b - bm)
    e_k = jnp.exp(bm - b)
    return b, bl, e_q, e_k, q * e_q, k * e_k


def _gla_fwd_call(q, k, v, g, *, flip, n_ctx, name):
    L = q.shape[0]
    H = q.shape[1] // HEAD_DK
    dv = v.shape[1] // H
    nb, nc, sub = L // SCAN_ROWS, n_ctx // SCAN_ROWS, SCAN_ROWS // CHUNK
    mode = 2 if flip else 0
    order = list(range(sub))[::-1] if flip else list(range(sub))

    def body(q_ref, k_ref, v_ref, g_ref, o_ref, sp_ref, st_ref):
        @pl.when(pl.program_id(1) == 0)
        def _():
            st_ref[...] = jnp.zeros_like(st_ref)

        causal, tri, _ = _gla_masks(flip)
        for s in order:
            sl = pl.ds(s * CHUNK, CHUNK)
            qc, kc, vc, gc = q_ref[sl, :], k_ref[sl, :], v_ref[sl, :], g_ref[sl, :]
            b, bl, _, _, qt, kt = _gla_chunk_terms(qc, kc, gc, tri, flip)
            a = jnp.where(causal, _dot_nt(qt, kt), 0.0)
            sp = st_ref[...]
            sp_ref[s] = sp
            o_ref[sl, :] = _dot(a, vc) + _dot_nt(qc * jnp.exp(b), sp)
            st_ref[...] = sp * jnp.exp(bl) + _dot3_tn(vc, kc * jnp.exp(bl - b))

    row = lambda w: pl.BlockSpec((SCAN_ROWS, w), lambda h, i: (_blk(mode, i, nb, nc), h))
    return pl.pallas_call(
        body, name=name, grid=(H, nb),
        in_specs=[row(HEAD_DK), row(HEAD_DK), row(dv), row(HEAD_DK)],
        out_specs=[row(dv), pl.BlockSpec((None, None, sub, dv, HEAD_DK),
                                         lambda h, i: (h, _blk(mode, i, nb, nc), 0, 0, 0))],
        out_shape=[jax.ShapeDtypeStruct((L, H * dv), F32),
                   jax.ShapeDtypeStruct((H, nb, sub, dv, HEAD_DK), F32)],
        scratch_shapes=[pltpu.VMEM((dv, HEAD_DK), F32)],
        compiler_params=_cparams(("parallel", "arbitrary")),
    )(q, k, v, g)


def _gla_bwd_call(q, k, v, g, sp, do, *, flip, n_ctx, name):
    L = q.shape[0]
    H = q.shape[1] // HEAD_DK
    dv = v.shape[1] // H
    nb, nc, sub = L // SCAN_ROWS, n_ctx // SCAN_ROWS, SCAN_ROWS // CHUNK
    mode = 3 if flip else 1
    order = list(range(sub)) if flip else list(range(sub))[::-1]
    last = 0 if flip else CHUNK - 1

    def body(q_ref, k_ref, v_ref, g_ref, sp_ref, do_ref, dq_ref, dk_ref, dv_ref, dg_ref, ds_ref):
        @pl.when(pl.program_id(1) == 0)
        def _():
            ds_ref[...] = jnp.zeros_like(ds_ref)

        causal, tri, tri_t = _gla_masks(flip)
        is_last = lax.broadcasted_iota(jnp.int32, (CHUNK, HEAD_DK), 0) == last
        for s in order:
            sl = pl.ds(s * CHUNK, CHUNK)
            qc, kc, vc, gc, doc = q_ref[sl, :], k_ref[sl, :], v_ref[sl, :], g_ref[sl, :], do_ref[sl, :]
            b, bl, e_q, e_k, qt, kt = _gla_chunk_terms(qc, kc, gc, tri, flip)
            e_b, e_l = jnp.exp(b), jnp.exp(bl - b)
            a = jnp.where(causal, _dot_nt(qt, kt), 0.0)
            p, kd = qc * e_b, kc * e_l
            spv, ds = sp_ref[s], ds_ref[...]
            dv_ref[sl, :] = _dot_tn(a, doc) + _dot_nt(kd, ds)
            da = jnp.where(causal, _dot_nt(doc, vc), 0.0)
            dkd = _dot3(vc, ds)
            dq = _dot3(da, kt) * e_q + _dot3(doc, spv) * e_b
            dk = _dot3_tn(da, qt) * e_k + dkd * e_l
            dq_ref[sl, :] = dq
            dk_ref[sl, :] = dk
            db_last = (jnp.sum(dkd * kd, axis=0, keepdims=True)
                       + jnp.exp(bl) * jnp.sum(ds * spv, axis=0, keepdims=True))
            db = qc * dq - kc * dk + jnp.where(is_last, db_last, 0.0)
            dg_ref[sl, :] = _cum(tri_t, db)
            ds_ref[...] = ds * jnp.exp(bl) + _dot3_tn(doc, p)

    row = lambda w: pl.BlockSpec((SCAN_ROWS, w), lambda h, i: (_blk(mode, i, nb, nc), h))
    return pl.pallas_call(
        body, name=name, grid=(H, nb),
        in_specs=[row(HEAD_DK), row(HEAD_DK), row(dv), row(HEAD_DK),
                  pl.BlockSpec((None, None, sub, dv, HEAD_DK), lambda h, i: (h, _blk(mode, i, nb, nc), 0, 0, 0)),
                  row(dv)],
        out_specs=[row(HEAD_DK), row(HEAD_DK), row(dv), row(HEAD_DK)],
        out_shape=[jax.ShapeDtypeStruct(q.shape, F32), jax.ShapeDtypeStruct(k.shape, F32),
                   jax.ShapeDtypeStruct(v.shape, F32), jax.ShapeDtypeStruct(g.shape, F32)],
        scratch_shapes=[pltpu.VMEM((dv, HEAD_DK), F32)],
        compiler_params=_cparams(("parallel", "arbitrary")),
    )(q, k, v, g, sp, do)


def _gla(name, *, flip, n_ctx):
    @jax.custom_vjp
    def op(q, k, v, g):
        return _gla_fwd_call(q, k, v, g, flip=flip, n_ctx=n_ctx, name=name + "_fwd")[0]

    def fwd(q, k, v, g):
        o, sp = _gla_fwd_call(q, k, v, g, flip=flip, n_ctx=n_ctx, name=name + "_fwd")
        return o, (q, k, v, g, sp)

    def bwd(res, do):
        return tuple(_gla_bwd_call(*res, do, flip=flip, n_ctx=n_ctx, name=name + "_bwd"))

    op.defvjp(fwd, bwd)
    return op


def _s5_table(ar, ai, desc):
    def mul(p, q):
        return p[0] * q[0] - p[1] * q[1], p[0] * q[1] + p[1] * q[0]

    a1 = (ar.reshape(-1), ai.reshape(-1))
    ar = a1[0]
    a2 = mul(a1, a1)
    a4 = mul(a2, a2)
    pw = [a1]
    for _ in range(7):
        pw.append(mul(pw[-1], a1))
    if desc:
        pw = pw[::-1]
    zero = jnp.zeros_like(ar)
    rows = [a1[0], a1[1], a2[0], a2[1], a4[0], a4[1], zero, zero] + [p[0] for p in pw] + [p[1] for p in pw]
    return jnp.stack(rows, axis=0)


S5_LANES = C_GROUPS * C_STATE
S5_LC = 1024


def _s5_scan(xr_ref, xi_ref, tab_ref, car_ref, desc, on_slab=None):
    nslab = S5_ROWS // 8
    rows = lax.broadcasted_iota(jnp.int32, (8, S5_LC), 0)

    def slab(t, carry):
        r0 = pl.multiple_of(((nslab - 1 - t) if desc else t) * 8, 8)
        for lc in range(S5_LANES // S5_LC):
            cs = pl.ds(lc * S5_LC, S5_LC)
            xr, xi = xr_ref[pl.ds(r0, 8), cs], xi_ref[pl.ds(r0, 8), cs]
            for n, s in enumerate((1, 2, 4)):
                ar, ai = tab_ref[2 * n:2 * n + 1, cs], tab_ref[2 * n + 1:2 * n + 2, cs]
                keep = (rows < 8 - s) if desc else (rows >= s)
                sr = jnp.where(keep, pltpu.roll(xr, (8 - s) if desc else s, 0), 0.0)
                si = jnp.where(keep, pltpu.roll(xi, (8 - s) if desc else s, 0), 0.0)
                xr, xi = xr + ar * sr - ai * si, xi + ar * si + ai * sr
            cr, ci = car_ref[0:1, cs], car_ref[1:2, cs]
            pr, pi = tab_ref[8:16, cs], tab_ref[16:24, cs]
            xr, xi = xr + pr * cr - pi * ci, xi + pr * ci + pi * cr
            if on_slab is not None:
                on_slab(r0, cs, xr, xi, cr, ci, rows)
            xr_ref[pl.ds(r0, 8), cs] = xr
            xi_ref[pl.ds(r0, 8), cs] = xi
            e = 0 if desc else 7
            car_ref[0:1, cs] = xr[e:e + 1, :]
            car_ref[1:2, cs] = xi[e:e + 1, :]
        return carry

    lax.fori_loop(0, nslab, slab, 0)


def _s5_fwd_call(u, tab, wb_re, wb_im, wc_re, wc_im, *, flip, n_ctx, name):
    L = u.shape[0]
    nb, nc = L // S5_ROWS, n_ctx // S5_ROWS
    mode, desc = (2, True) if flip else (0, False)

    def body(u_ref, tab_ref, wbr_ref, wbi_ref, wcr_ref, wci_ref, y_ref, xr_ref, xi_ref, car_ref):
        @pl.when(pl.program_id(0) == 0)
        def _():
            car_ref[...] = jnp.zeros_like(car_ref)

        for gb in range(S5_GB):
            us, cs = pl.ds(gb * 128, 128), pl.ds(gb * S5_GBW, S5_GBW)
            ub = u_ref[:, us]
            xr_ref[:, cs] = _dot(ub, wbr_ref[gb])
            xi_ref[:, cs] = _dot(ub, wbi_ref[gb])
        _s5_scan(xr_ref, xi_ref, tab_ref, car_ref, desc)
        for gb in range(S5_GB):
            us, cs = pl.ds(gb * 128, 128), pl.ds(gb * S5_GBW, S5_GBW)
            y_ref[:, us] = _dot(xr_ref[:, cs], wcr_ref[gb]) - _dot(xi_ref[:, cs], wci_ref[gb])

    row = lambda w: pl.BlockSpec((S5_ROWS, w), lambda i: (_blk(mode, i, nb, nc), 0))
    full = lambda a: pl.BlockSpec(a.shape, lambda i: (0,) * a.ndim)
    return pl.pallas_call(
        body, name=name, grid=(nb,),
        in_specs=[row(BRANCH_W), full(tab), full(wb_re), full(wb_im), full(wc_re), full(wc_im)],
        out_specs=[row(BRANCH_W), row(S5_LANES), row(S5_LANES)],
        out_shape=[jax.ShapeDtypeStruct((L, BRANCH_W), F32), jax.ShapeDtypeStruct((L, S5_LANES), F32),
                   jax.ShapeDtypeStruct((L, S5_LANES), F32)],
        scratch_shapes=[pltpu.VMEM((8, S5_LANES), F32)],
        compiler_params=_cparams(("arbitrary",)),
    )(u, tab, wb_re, wb_im, wc_re, wc_im)


def _s5_bwd_call(dy, u, xr, xi, tab, wbt_re, wbt_im, wct_re, wct_im, *, flip, n_ctx, name):
    L = u.shape[0]
    nb, nc = L // S5_ROWS, n_ctx // S5_ROWS
    mode, desc = (3, False) if flip else (1, True)

    def body(dy_ref, u_ref, xr_ref, xi_ref, tab_ref, wbtr_ref, wbti_ref, wctr_ref, wcti_ref,
             du_ref, dwbr_ref, dwbi_ref, dwcr_ref, dwci_ref, ga_ref, gr_ref, gi_ref, car_ref):
        @pl.when(pl.program_id(0) == 0)
        def _():
            car_ref[...] = jnp.zeros_like(car_ref)
            ga_ref[...] = jnp.zeros_like(ga_ref)
            for r in (dwbr_ref, dwbi_ref, dwcr_ref, dwci_ref):
                r[...] = jnp.zeros_like(r)

        for gb in range(S5_GB):
            us, cs = pl.ds(gb * 128, 128), pl.ds(gb * S5_GBW, S5_GBW)
            dyb = dy_ref[:, us]
            gr_ref[:, cs] = _dot(dyb, wctr_ref[gb])
            gi_ref[:, cs] = -_dot(dyb, wcti_ref[gb])

        def lam_grad(r0, cs, gr, gi, cr, ci, rows):
            edge = 7 if desc else 0
            pr = jnp.where(rows == edge, cr, pltpu.roll(gr, 7 if desc else 1, 0))
            pi = jnp.where(rows == edge, ci, pltpu.roll(gi, 7 if desc else 1, 0))
            x_r, x_i = xr_ref[pl.ds(r0, 8), cs], xi_ref[pl.ds(r0, 8), cs]
            ga_ref[0:8, cs] += x_r * pr + x_i * pi
            ga_ref[8:16, cs] += x_r * pi - x_i * pr

        _s5_scan(gr_ref, gi_ref, tab_ref, car_ref, desc, on_slab=lam_grad)
        for gb in range(S5_GB):
            us, cs = pl.ds(gb * 128, 128), pl.ds(gb * S5_GBW, S5_GBW)
            g_r, g_i, ub, dyb = gr_ref[:, cs], gi_ref[:, cs], u_ref[:, us], dy_ref[:, us]
            du_ref[:, us] = _dot(g_r, wbtr_ref[gb]) + _dot(g_i, wbti_ref[gb])
            dwbr_ref[gb] += _dot_tn(ub, g_r)
            dwbi_ref[gb] += _dot_tn(ub, g_i)
            dwcr_ref[gb] += _dot_tn(xr_ref[:, cs], dyb)
            dwci_ref[gb] -= _dot_tn(xi_ref[:, cs], dyb)

    row = lambda w: pl.BlockSpec((S5_ROWS, w), lambda i: (_blk(mode, i, nb, nc), 0))
    full = lambda shape: pl.BlockSpec(shape, lambda i: (0,) * len(shape))
    wb_shape, wc_shape = (S5_GB, 128, S5_GBW), (S5_GB, S5_GBW, 128)
    return pl.pallas_call(
        body, name=name, grid=(nb,),
        in_specs=[row(BRANCH_W), row(BRANCH_W), row(S5_LANES), row(S5_LANES), full(tab.shape),
                  full(wc_shape), full(wc_shape), full(wb_shape), full(wb_shape)],
        out_specs=[row(BRANCH_W), full(wb_shape), full(wb_shape), full(wc_shape), full(wc_shape),
                   full((16, S5_LANES))],
        out_shape=[jax.ShapeDtypeStruct((L, BRANCH_W), F32),
                   jax.ShapeDtypeStruct(wb_shape, F32), jax.ShapeDtypeStruct(wb_shape, F32),
                   jax.ShapeDtypeStruct(wc_shape, F32), jax.ShapeDtypeStruct(wc_shape, F32),
                   jax.ShapeDtypeStruct((16, S5_LANES), F32)],
        scratch_shapes=[pltpu.VMEM((S5_ROWS, S5_LANES), F32), pltpu.VMEM((S5_ROWS, S5_LANES), F32),
                        pltpu.VMEM((8, S5_LANES), F32)],
        compiler_params=_cparams(("arbitrary",)),
    )(dy, u, xr, xi, tab, wbt_re, wbt_im, wct_re, wct_im)


def _s5(name, *, flip, n_ctx):
    desc_f = flip
    desc_b = not flip

    def run_fwd(u, lr, li, wbr, wbi, wcr, wci):
        tab = _s5_table(lr, li, desc_f)
        c16 = lambda a: a.astype(BF16)
        return _s5_fwd_call(u, tab, c16(wbr), c16(wbi), c16(wcr), c16(wci), flip=flip, n_ctx=n_ctx, name=name + "_fwd")

    @jax.custom_vjp
    def op(u, lr, li, wbr, wbi, wcr, wci):
        return run_fwd(u, lr, li, wbr, wbi, wcr, wci)[0]

    def fwd(u, lr, li, wbr, wbi, wcr, wci):
        y, xr, xi = run_fwd(u, lr, li, wbr, wbi, wcr, wci)
        return y, (u, xr, xi, lr, li, wbr, wbi, wcr, wci)

    def bwd(res, dy):
        u, xr, xi, lr, li, wbr, wbi, wcr, wci = res
        tab = _s5_table(lr, -li, desc_b)
        t16 = lambda a: jnp.swapaxes(a, 1, 2).astype(BF16)
        du, dwbr, dwbi, dwcr, dwci, ga = _s5_bwd_call(
            dy, u, xr, xi, tab, t16(wbr), t16(wbi), t16(wcr), t16(wci), flip=flip, n_ctx=n_ctx, name=name + "_bwd")
        return (du, jnp.sum(ga[0:8], axis=0, keepdims=True), jnp.sum(ga[8:16], axis=0, keepdims=True),
                dwbr, dwbi, dwcr, dwci)

    op.defvjp(fwd, bwd)
    return op


def _loss_call(h, tgt, g, *, n_ctx, name):
    L, D = h.shape
    tr = 128
    ncx = n_ctx // tr

    def body(h_ref, t_ref, g_ref, dh_ref, dg_ref, loss_ref):
        i = pl.program_id(0)

        @pl.when(i == 0)
        def _():
            dg_ref[...] = jnp.zeros_like(dg_ref)
            loss_ref[...] = jnp.zeros_like(loss_ref)

        @pl.when(i < ncx)
        def _():
            dh_ref[...] = jnp.zeros_like(dh_ref)

        @pl.when(i >= ncx)
        def _():
            t = t_ref[...]

            def f(hv, gv):
                e = _rms(hv, gv) - t
                return 0.5 * jnp.sum(e * e) / D

            l, (dh, dg) = jax.value_and_grad(f, argnums=(0, 1))(h_ref[...], g_ref[...])
            dh_ref[...] = dh
            dg_ref[...] += dg
            loss_ref[...] += l

    return pl.pallas_call(
        body, name=name, grid=(L // tr,),
        in_specs=[pl.BlockSpec((tr, D), lambda i: (i, 0)),
                  pl.BlockSpec((tr, D), lambda i: (jnp.maximum(i - ncx, 0), 0)),
                  pl.BlockSpec((1, D), lambda i: (0, 0))],
        out_specs=[pl.BlockSpec((tr, D), lambda i: (i, 0)), pl.BlockSpec((1, D), lambda i: (0, 0)),
                   pl.BlockSpec((8, 128), lambda i: (0, 0))],
        out_shape=[jax.ShapeDtypeStruct((L, D), F32), jax.ShapeDtypeStruct((1, D), F32),
                   jax.ShapeDtypeStruct((8, 128), F32)],
        compiler_params=_cparams(("arbitrary",)),
    )(h, tgt, g)


def _adamw_call(w, g, m, v, *, name):
    R, C = w.shape
    tr = R if R * C * 4 <= (1 << 20) else _tile(R, tuple(t for t in (512, 256, 128, 64, 32, 16, 8) if t * C * 4 <= (1 << 20)))
    c1 = 1.0 / (1.0 - ADAM_B1 ** ADAM_STEP)
    c2 = 1.0 / (1.0 - ADAM_B2 ** ADAM_STEP)

    def body(w_ref, g_ref, m_ref, v_ref, d_ref, nm_ref, nv_ref):
        gv = g_ref[...]
        nm = ADAM_B1 * m_ref[...] + (1.0 - ADAM_B1) * gv
        nv = ADAM_B2 * v_ref[...] + (1.0 - ADAM_B2) * gv * gv
        d_ref[...] = -ADAM_LR * ((nm * c1) / (jnp.sqrt(nv * c2) + ADAM_EPS) + ADAM_WD * w_ref[...])
        nm_ref[...] = nm
        nv_ref[...] = nv

    spec = pl.BlockSpec((tr, C), lambda i: (i, 0))
    return pl.pallas_call(
        body, name=name, grid=(R // tr,), in_specs=[spec] * 4, out_specs=[spec] * 3,
        out_shape=[jax.ShapeDtypeStruct((R, C), F32)] * 3,
        compiler_params=_cparams(("parallel",)),
    )(w, g, m, v)


def _as2d(a):
    n = a.size
    if a.ndim >= 2 and n >= (1 << 20):
        return a.reshape(-1, a.shape[-1])
    if n % 128 == 0:
        return a.reshape(n // 128, 128)
    return a.reshape(1, n)


def _adamw(w, g, m, v, *, name):
    d, nm, nv = _adamw_call(_as2d(w), _as2d(g.astype(F32)), _as2d(m), _as2d(v), name=name)
    return d.reshape(w.shape), nm.reshape(w.shape), nv.reshape(w.shape)


MESH = pl.DeviceIdType.MESH
ANY = pl.BlockSpec(memory_space=pl.ANY)


def _place():
    return lax.axis_index("x"), lax.axis_index("y"), lax.axis_index("c")


def _all_gather(x, *, name):
    M, N = x.shape

    def body(x_ref, out_ref, send_sems, recv_sems, local_sem):
        x_, y_, c_ = _place()
        me, sibling = (x_, y_, c_), (x_, y_, 1 - c_)
        chips = [(1 - x_, y_), (x_, 1 - y_), (1 - x_, 1 - y_)]

        def slab(px, py, pc):
            return out_ref.at[4 * px + 2 * py + pc]

        def copy(k, block, to, src=None):
            return pltpu.make_async_remote_copy(
                src_ref=slab(*block) if src is None else src, dst_ref=slab(*block),
                send_sem=send_sems.at[k], recv_sem=recv_sems.at[k], device_id=to, device_id_type=MESH)

        mine = pltpu.make_async_copy(x_ref, slab(*me), local_sem)
        mine.start()
        first = [copy(0, me, sibling, src=x_ref)]
        first += [copy(1 + j, me, (*chip, c_), src=x_ref) for j, chip in enumerate(chips)]
        for cp in first:
            cp.start()
        passed = [copy(4 + j, (*chip, c_), sibling) for j, chip in enumerate(chips)]
        for j, chip in enumerate(chips):
            copy(1 + j, (*chip, c_), me).wait_recv()
            passed[j].start()
        copy(0, sibling, me).wait_recv()
        for j, chip in enumerate(chips):
            copy(4 + j, (*chip, 1 - c_), me).wait_recv()
        for cp in first + passed:
            cp.wait_send()
        mine.wait()

    return pl.pallas_call(
        body, name=name, out_shape=jax.ShapeDtypeStruct((N_DEV, M, N), x.dtype),
        in_specs=[ANY], out_specs=ANY,
        scratch_shapes=[pltpu.SemaphoreType.DMA((7,)), pltpu.SemaphoreType.DMA((7,)), pltpu.SemaphoreType.DMA],
    )(x)


def _pair_exchange(g, *, name):
    _, R, C = g.shape

    def body(g_ref, out_ref, send_sems, recv_sems):
        x_, y_, c_ = _place()
        copies = []
        for k in range(4):
            cp = pltpu.make_async_remote_copy(
                src_ref=g_ref.at[2 * k + (1 - c_)], dst_ref=out_ref.at[k], send_sem=send_sems.at[k],
                recv_sem=recv_sems.at[k], device_id=(x_, y_, 1 - c_), device_id_type=MESH)
            cp.start()
            copies.append(cp)
        for cp in copies:
            cp.wait()

    return pl.pallas_call(
        body, name=name, out_shape=jax.ShapeDtypeStruct((4, R, C), g.dtype), in_specs=[ANY], out_specs=ANY,
        scratch_shapes=[pltpu.SemaphoreType.DMA((4,)), pltpu.SemaphoreType.DMA((4,))],
    )(g)


def _chip_exchange(p, *, name):
    _, R, C = p.shape

    def body(p_ref, out_ref, send_sems, recv_sems):
        x_, y_, c_ = _place()
        chips = [(1 - x_, y_), (x_, 1 - y_), (1 - x_, 1 - y_)]
        copies = []
        for j, (cx, cy) in enumerate(chips):
            cp = pltpu.make_async_remote_copy(
                src_ref=p_ref.at[2 * cx + cy], dst_ref=out_ref.at[j], send_sem=send_sems.at[j],
                recv_sem=recv_sems.at[j], device_id=(cx, cy, c_), device_id_type=MESH)
            cp.start()
            copies.append(cp)
        for cp in copies:
            cp.wait()

    return pl.pallas_call(
        body, name=name, out_shape=jax.ShapeDtypeStruct((3, R, C), p.dtype), in_specs=[ANY], out_specs=ANY,
        scratch_shapes=[pltpu.SemaphoreType.DMA((3,)), pltpu.SemaphoreType.DMA((3,))],
    )(p)


def _add_tr(R, C):
    return _tile(R, tuple(t for t in (512, 256, 128, 64, 32, 16, 8) if t * C * 4 <= (2 << 20)))


def _pair_add(g, recv, *, name):
    _, R, C = g.shape
    tr = _add_tr(R, C)
    core = lax.axis_index("c").astype(jnp.int32).reshape(1)

    def body(c_ref, g_ref, r_ref, o_ref):
        o_ref[...] = (g_ref[...] + r_ref[...]).astype(o_ref.dtype)

    return pl.pallas_call(
        body, name=name, out_shape=jax.ShapeDtypeStruct((4, R, C), BF16),
        grid_spec=pltpu.PrefetchScalarGridSpec(
            num_scalar_prefetch=1, grid=(4, R // tr),
            in_specs=[pl.BlockSpec((None, tr, C), lambda k, i, c: (2 * k + c[0], i, 0)),
                      pl.BlockSpec((None, tr, C), lambda k, i, c: (k, i, 0))],
            out_specs=pl.BlockSpec((None, tr, C), lambda k, i, c: (k, i, 0))),
        compiler_params=_cparams(("parallel", "parallel")),
    )(core, g, recv)


def _chip_add(p, recv, *, name):
    _, R, C = p.shape
    tr = _add_tr(R, C)
    chip = (2 * lax.axis_index("x") + lax.axis_index("y")).astype(jnp.int32).reshape(1)

    def body(c_ref, p_ref, r0_ref, r1_ref, r2_ref, o_ref):
        f = lambda r: r[...].astype(F32)
        o_ref[...] = ((f(p_ref) + f(r0_ref)) + f(r1_ref)) + f(r2_ref)

    rspec = lambda j: pl.BlockSpec((None, tr, C), lambda i, c: (j, i, 0))
    return pl.pallas_call(
        body, name=name, out_shape=jax.ShapeDtypeStruct((R, C), F32),
        grid_spec=pltpu.PrefetchScalarGridSpec(
            num_scalar_prefetch=1, grid=(R // tr,),
            in_specs=[pl.BlockSpec((None, tr, C), lambda i, c: (c[0], i, 0)), rspec(0), rspec(1), rspec(2)],
            out_specs=pl.BlockSpec((tr, C), lambda i, c: (i, 0))),
        compiler_params=_cparams(("parallel",)),
    )(chip, p, recv, recv, recv)


def _reduce_scatter(g, *, name):
    p = _pair_add(g, _pair_exchange(g, name=name + "_pair"), name=name + "_pair_add")
    return _chip_add(p, _chip_exchange(p, name=name + "_chip"), name=name + "_chip_add")


def _sum8(a, *, name):
    _, R, C = a.shape
    tr = _tile(R, tuple(t for t in (512, 256, 128, 64, 32, 16, 8) if t * C * 4 * 8 <= (4 << 20)))

    def body(a_ref, o_ref):
        acc = a_ref[0]
        for d in range(1, N_DEV):
            acc = acc + a_ref[d]
        o_ref[...] = acc

    return pl.pallas_call(
        body, name=name, grid=(R // tr,), in_specs=[pl.BlockSpec((N_DEV, tr, C), lambda i: (0, i, 0))],
        out_specs=pl.BlockSpec((tr, C), lambda i: (i, 0)), out_shape=jax.ShapeDtypeStruct((R, C), F32),
        compiler_params=_cparams(("parallel",)),
    )(a)


def _s5_params(lam_re, lam_im, log_dt, b_re, b_im, c_re, c_im):
    lr = jnp.minimum(lam_re, C_MAX_RE)
    li = lam_im
    dt = jnp.exp(log_dt)[:, None]
    mag = jnp.exp(lr * dt)
    ar, ai = mag * jnp.cos(li * dt), mag * jnp.sin(li * dt)
    nr, ni, den = ar - 1.0, ai, lr * lr + li * li
    fr, fi = (nr * lr + ni * li) / den, (ni * lr - nr * li) / den
    bbr = fr[..., None] * b_re - fi[..., None] * b_im
    bbi = fr[..., None] * b_im + fi[..., None] * b_re
    eye = jnp.eye(S5_GB, dtype=F32)

    def dense_b(bb):
        t = bb.reshape(S5_GB, S5_GB, C_STATE, C_GROUP).transpose(0, 1, 3, 2)
        return (t[:, :, :, None, :] * eye[None, :, None, :, None]).reshape(S5_GB, S5_GB * C_GROUP, S5_GBW)

    def dense_c(cc):
        t = cc.reshape(S5_GB, S5_GB, C_GROUP, C_STATE).transpose(0, 1, 3, 2)
        return (t[:, :, :, None, :] * eye[None, :, None, :, None]).reshape(S5_GB, S5_GBW, S5_GB * C_GROUP)

    flat = lambda a: a.reshape(1, S5_LANES)
    return flat(ar), flat(ai), dense_b(bbr), dense_b(bbi), dense_c(c_re), dense_c(c_im)


def _rope_tables(n_ctx, n_lat):
    quarter = HEAD_DK // 4
    freqs = ROPE_BASE ** (-jnp.arange(quarter, dtype=F32) / quarter)
    t = jnp.arange(n_lat)
    r = (t // GRID_W).astype(F32)
    col = (t % GRID_W).astype(F32)
    ang = jnp.concatenate([r[:, None] * freqs, col[:, None] * freqs], axis=-1)
    ang = jnp.concatenate([jnp.zeros((n_ctx, HEAD_DK // 2), F32), ang], axis=0)
    cos, sin = jnp.cos(ang), jnp.sin(ang)
    idx = jnp.arange(HEAD_DK)
    swap = (idx[:, None] == (idx[None, :] + HEAD_DK // 2) % HEAD_DK).astype(F32)
    return jnp.concatenate([cos, cos], axis=1), jnp.concatenate([-sin, sin], axis=1), swap


W_IN = 13344
W_INP = 13440
_COL = dict(a_q=0, a_ff=1024, a_fb=2048, a_i=3072, a_g=4096, b_q=5120, b_k=5632, b_v=6144, b_g=7168,
            c_u=8192, c_g=9216, d_q=10240, d_k=10752, d_v=11264, d_g=12288, b_lr=13312)


_COL_ORDER = (("a_q", 1024), ("a_ff", 1024), ("a_fb", 1024), ("a_i", 1024), ("a_g", 1024), ("b_q", 512), ("b_k", 512),
              ("b_v", 1024), ("b_g", 1024), ("c_u", 1024), ("c_g", 1024), ("d_q", 512), ("d_k", 512), ("d_v", 1024),
              ("d_g", 1024), ("b_lr", 128))
_W_IN_MAP = ((0, 7168, 0), (7200, W_IN, 7168), (7168, 7200, 13312))
W_IN_SHARD = W_IN // N_DEV


def _w_in_from_shards(g):
    parts = []
    for lo, hi, _ in _W_IN_MAP:
        for d in range(N_DEV):
            a, b = max(lo, d * W_IN_SHARD), min(hi, (d + 1) * W_IN_SHARD)
            if a < b:
                parts.append(g[d, ..., a - d * W_IN_SHARD:b - d * W_IN_SHARD])
    parts.append(jnp.zeros(g.shape[1:-1] + (W_INP - W_IN,), g.dtype))
    return jnp.concatenate(parts, axis=-1)


def _w_in_to_shards(w):
    shards = []
    for d in range(N_DEV):
        parts = []
        for lo, hi, here in sorted(_W_IN_MAP):
            a, b = max(lo, d * W_IN_SHARD), min(hi, (d + 1) * W_IN_SHARD)
            if a < b:
                parts.append(w[..., here + a - lo:here + b - lo])
        shards.append(jnp.concatenate(parts, axis=-1))
    return jnp.stack(shards)


@jax.custom_vjp
def _split_cols(proj):
    return tuple(lax.slice_in_dim(proj, _COL[n], _COL[n] + w, axis=1) for n, w in _COL_ORDER)


def _split_cols_fwd(proj):
    return _split_cols(proj), None


def _split_cols_bwd(_, cts):
    return (jnp.concatenate(cts, axis=1),)


_split_cols.defvjp(_split_cols_fwd, _split_cols_bwd)


def _layer_ops(l, n_ctx):
    n = f"l{l}_"
    rw = functools.partial(_rowwise, n_ctx=n_ctx)
    wide = [(BRANCH_W, F32)]
    return dict(
        modnorm=rw(_modnorm_fn, ["row", "const", "seg", "seg"], [(4096, BF16)], ncb=1, tr=128, name=n + "modnorm"),
        resid=rw(_resid_fn, ["row", "row", "seg"], [(4096, F32)], ncb=1, tr=128, name=n + "resid"),
        lin_in=_linear(n + "lin_in"), lin_in_gather=_linear_gather(n + "lin_in"),
        lin_out=_linear(n + "lin_out"), lin_glu=_linear(n + "lin_glu"),
        hgrn_pro=rw(_hgrn_pro_fn, ["row", "row", "const", "const"], wide * 4, ncb=1, tr=256, name=n + "hgrn_pro"),
        hgrn_f=_gla(n + "hgrn_f", flip=False, n_ctx=n_ctx), hgrn_b=_gla(n + "hgrn_b", flip=True, n_ctx=n_ctx),
        hgrn_epi=rw(_hgrn_epi_fn, ["row", "row", "row", "const"], [(BRANCH_W, BF16)], ncb=8, tr=256, name=n + "hgrn_epi"),
        gla_pro=rw(_gla_pro_fn, ["row", "row", "const", "const", "const", "const"], [(512, F32)] * 3, ncb=1, tr=256,
                   name=n + "gla_pro"),
        gla_f=_gla(n + "gla_f", flip=False, n_ctx=n_ctx), gla_b=_gla(n + "gla_b", flip=True, n_ctx=n_ctx),
        gla_epi=rw(_hgrn_epi_fn, ["row", "row", "row", "const"], [(BRANCH_W, BF16)], ncb=4, tr=256, name=n + "gla_epi"),
        s5_f=_s5(n + "s5_f", flip=False, n_ctx=n_ctx), s5_b=_s5(n + "s5_b", flip=True, n_ctx=n_ctx),
        s5_mid=rw(_s5_mid_fn, ["row", "row", "row", "const"], wide, ncb=1, tr=256, name=n + "s5_mid"),
        s5_epi=rw(_s5_epi_fn, ["row", "row", "row", "const"], [(BRANCH_W, BF16)], ncb=1, tr=256, name=n + "s5_epi"),
        ret_pro=rw(_ret_pro_fn, ["row", "row", "rowc", "rowc", "col", "col", "fixed"], [(512, F32)] * 4, ncb=4, tr=256,
                   name=n + "ret_pro"),
        ret_f=_gla(n + "ret_f", flip=False, n_ctx=n_ctx), ret_b=_gla(n + "ret_b", flip=True, n_ctx=n_ctx),
        ret_epi=rw(_ret_epi_fn, ["row", "row", "row", "const"], [(BRANCH_W, BF16)], ncb=4, tr=256, name=n + "ret_epi"),
    )


def _layer(ops, h, p, w_in, w_out, w_glu, lb, rope, next_shard=None):
    cos2, sin2, swap = rope
    (hn,) = ops["modnorm"](h, p["norm_g"], p["sc"], p["sh"])
    if next_shard is None:
        proj, got = ops["lin_in"](hn, w_in, p["shadow_in"]), None
    else:
        proj, got = ops["lin_in_gather"](hn, w_in, p["shadow_in"], next_shard)
    cols = dict(zip([n for n, _ in _COL_ORDER], _split_cols(proj)))
    col = lambda name, w: cols[name]

    a_q, a_i = col("a_q", 1024), col("a_i", 1024)
    kf, gf, kb, gb = ops["hgrn_pro"](col("a_ff", 1024), col("a_fb", 1024), lb[0:1], lb[1:2])
    (o_a,) = ops["hgrn_epi"](ops["hgrn_f"](a_q, kf, a_i, gf), ops["hgrn_b"](a_q, kb, a_i, gb), col("a_g", 1024),
                             p["hgrn_norm_g"])

    rank = B_GATE_RANK
    wf = jnp.pad(p["gla_w_gk"][0], ((0, 128 - rank), (0, 0)))
    wb = jnp.pad(p["gla_w_gk"][1], ((rank, 128 - 2 * rank), (0, 0)))
    b_k, b_v = col("b_k", 512), col("b_v", 1024)
    qs, gf, gb = ops["gla_pro"](col("b_q", 512), col("b_lr", 128), wf, wb, p["gla_b_gk"][0:1], p["gla_b_gk"][1:2])
    (o_b,) = ops["gla_epi"](ops["gla_f"](qs, b_k, b_v, gf), ops["gla_b"](qs, b_k, b_v, gb), col("b_g", 1024),
                            p["gla_norm_g"])

    c_u = col("c_u", 1024)
    s5p = lambda d: _s5_params(*[p[k][d] for k in ("s5_lam_re", "s5_lam_im", "s5_log_dt", "s5_b_re", "s5_b_im",
                                                  "s5_c_re", "s5_c_im")])
    (z,) = ops["s5_mid"](ops["s5_f"](c_u, *s5p(0)), ops["s5_b"](c_u, *s5p(1)), c_u, p["s5_d"].reshape(1, BRANCH_W))
    t = ops["lin_glu"](z, w_glu, p["shadow_glu"])
    (o_c,) = ops["s5_epi"](z, t, col("c_g", 1024), p["s5_b_glu"])

    log_gamma = _log_sigmoid(p["ret_decay_logit"])
    lg = lambda d: jnp.broadcast_to(log_gamma[d][:, None, None], (4, 1, HEAD_DK))
    d_v = col("d_v", 1024)
    qs, ks, gf, gb = ops["ret_pro"](col("d_q", 512), col("d_k", 512), cos2, sin2, lg(0), lg(1), swap)
    (o_d,) = ops["ret_epi"](ops["ret_f"](qs, ks, d_v, gf), ops["ret_b"](qs, ks, d_v, gb), col("d_g", 1024),
                            p["ret_norm_g"])

    o = jnp.concatenate([o_a, o_b, o_c, o_d], axis=1)
    y = ops["lin_out"](o, w_out, p["shadow_out"])
    (h,) = ops["resid"](h, y, p["gt"])
    return h, got


_SMALL = ("norm_g", "hgrn_lb_logits", "hgrn_norm_g", "gla_w_gk", "gla_b_gk", "gla_norm_g", "s5_lam_re", "s5_lam_im",
          "s5_log_dt", "s5_b_re", "s5_b_im", "s5_c_re", "s5_c_im", "s5_d", "s5_b_glu", "ret_decay_logit", "ret_norm_g",
          "final_norm_g")
_WEIGHTS = ("c_ctx", "norm_g", "w_ada", "b_ada", "w_in", "hgrn_lb_logits", "hgrn_norm_g", "gla_w_gk", "gla_b_gk",
            "gla_norm_g", "s5_lam_re", "s5_lam_im", "s5_log_dt", "s5_b_re", "s5_b_im", "s5_c_re", "s5_c_im", "s5_d",
            "s5_w_glu", "s5_b_glu", "ret_decay_logit", "ret_norm_g", "w_out", "final_norm_g")


def _pad_rows(flat, mult=8):
    rows = -(-flat.size // 128)
    rows = -(-rows // mult) * mult
    return jnp.pad(flat, (0, rows * 128 - flat.size)).reshape(rows, 128)


def _unshard_cols(g, inner):
    k = len(inner)
    a = g.reshape((N_DEV,) + tuple(inner))
    return jnp.moveaxis(a, 0, k - 1).reshape(tuple(inner[:-1]) + (N_DEV * inner[-1],))


def kernel(x, c, ctx, c_ctx, norm_g, w_ada, b_ada, w_in, hgrn_lb_logits, hgrn_norm_g, gla_w_gk, gla_b_gk, gla_norm_g, s5_lam_re, s5_lam_im, s5_log_dt, s5_b_re, s5_b_im, s5_c_re, s5_c_im, s5_d, s5_w_glu, s5_b_glu, ret_decay_logit, ret_norm_g, w_out, final_norm_g, loss_target, m_c_ctx, m_norm_g, m_w_ada, m_b_ada, m_w_in, m_hgrn_lb_logits, m_hgrn_norm_g, m_gla_w_gk, m_gla_b_gk, m_gla_norm_g, m_s5_lam_re, m_s5_lam_im, m_s5_log_dt, m_s5_b_re, m_s5_b_im, m_s5_c_re, m_s5_c_im, m_s5_d, m_s5_w_glu, m_s5_b_glu, m_ret_decay_logit, m_ret_norm_g, m_w_out, m_final_norm_g, v_c_ctx, v_norm_g, v_w_ada, v_b_ada, v_w_in, v_hgrn_lb_logits, v_hgrn_norm_g, v_gla_w_gk, v_gla_b_gk, v_gla_norm_g, v_s5_lam_re, v_s5_lam_im, v_s5_log_dt, v_s5_b_re, v_s5_b_im, v_s5_c_re, v_s5_c_im, v_s5_d, v_s5_w_glu, v_s5_b_glu, v_ret_decay_logit, v_ret_norm_g, v_w_out, v_final_norm_g):
    given = dict(locals())
    depth = norm_g.shape[0]
    n_ctx, n_lat, d_model = ctx.shape[1], x.shape[1], x.shape[2]
    me = 4 * lax.axis_index("x") + 2 * lax.axis_index("y") + lax.axis_index("c")
    w_ada_cols = w_ada.shape[2]

    pack = jnp.concatenate([c.reshape(-1), hgrn_lb_logits.reshape(-1), gla_w_gk.reshape(-1), gla_b_gk.reshape(-1)])
    got = _all_gather(_pad_rows(pack), name="gather_small").reshape(N_DEV, -1)
    o0 = d_model
    o1 = o0 + hgrn_lb_logits.size
    o2 = o1 + gla_w_gk.size
    o3 = o2 + gla_b_gk.size
    c_all = got[:, :o0]
    lb_logits = _unshard_cols(got[:, o0:o1], hgrn_lb_logits.shape)
    w_gk = _unshard_cols(got[:, o1:o2], gla_w_gk.shape)
    b_gk = _unshard_cols(got[:, o2:o3], gla_b_gk.shape)

    cond = jnp.concatenate([c_all, c_ctx[None, :], jnp.zeros((7, d_model), F32)], axis=0)
    act = _silu(cond).astype(BF16)
    mod_part = jnp.concatenate([_mm(act, w_ada[l], name=f"ada{l}_fwd") for l in range(depth)], axis=0)
    mod = _all_gather(mod_part, name="gather_mod").reshape(N_DEV, depth, 16, w_ada_cols)
    mod = jnp.moveaxis(mod, 0, 2).reshape(depth, 16, N_DEV * w_ada_cols) + b_ada[:, None, :]
    mod_lat = lax.dynamic_index_in_dim(mod, me, axis=1, keepdims=False)
    mod_ctx = mod[:, 8]
    seg = lambda l, j: jnp.stack([mod_ctx[l, j * d_model:(j + 1) * d_model],
                                  mod_lat[l, j * d_model:(j + 1) * d_model]])[:, None, :]

    def gathered(w, name):
        flat = w.astype(BF16).reshape(-1, w.shape[-1])
        return _all_gather(flat, name=name).reshape((N_DEV,) + w.shape)

    w_in_shards = [w_in[l].astype(BF16) for l in range(depth)]
    w_in_first = _w_in_from_shards(_all_gather(w_in_shards[0], name="gather_w_in"))
    w_out_full = jnp.moveaxis(gathered(w_out, "gather_w_out"), 0, 1).reshape(depth, 4 * BRANCH_W, d_model)
    w_glu_full = jnp.moveaxis(gathered(s5_w_glu, "gather_w_glu"), 0, 1).reshape(depth, BRANCH_W, BRANCH_W)

    rope = _rope_tables(n_ctx, n_lat)
    ops = [_layer_ops(l, n_ctx) for l in range(depth)]
    params = dict(
        layers=[dict(
            norm_g=norm_g[l][None, :], sh=seg(l, 0), sc=seg(l, 1), gt=seg(l, 2),
            shadow_in=jnp.zeros((d_model, W_INP), F32), shadow_out=jnp.zeros((4 * BRANCH_W, d_model), F32),
            shadow_glu=jnp.zeros((BRANCH_W, BRANCH_W), F32),
            hgrn_norm_g=hgrn_norm_g[l][None, :], gla_w_gk=w_gk[l], gla_b_gk=b_gk[l], gla_norm_g=gla_norm_g[l][None, :],
            s5_lam_re=s5_lam_re[l], s5_lam_im=s5_lam_im[l], s5_log_dt=s5_log_dt[l], s5_b_re=s5_b_re[l],
            s5_b_im=s5_b_im[l], s5_c_re=s5_c_re[l], s5_c_im=s5_c_im[l], s5_d=s5_d[l], s5_b_glu=s5_b_glu[l][None, :],
            ret_decay_logit=ret_decay_logit[l], ret_norm_g=ret_norm_g[l][None, :]) for l in range(depth)],
        lb_logits=lb_logits,
        h0=jnp.concatenate([ctx[0], x[0]], axis=0))

    def body(p):
        lb_p = jax.nn.softmax(p["lb_logits"], axis=0)
        lower = jnp.cumsum(lb_p, axis=0) - lb_p[0:1]
        h = p["h0"]
        w_in_l = w_in_first
        for l in range(depth):
            h, got = _layer(ops[l], h, p["layers"][l], w_in_l, w_out_full[l], w_glu_full[l], lower[l], rope,
                            next_shard=w_in_shards[l + 1] if l + 1 < depth else None)
            if got is not None:
                w_in_l = _w_in_from_shards(lax.stop_gradient(got))
        return h

    h_out, pullback = jax.vjp(body, params)
    dh, dg_final, loss_part = _loss_call(h_out, loss_target[0], final_norm_g[None, :], n_ctx=n_ctx, name="loss_head")
    (grads,) = pullback(dh)
    loss = lax.psum(loss_part[0, 0], ("x", "y", "c"))
    gl = grads["layers"]

    dmod = jnp.stack([jnp.concatenate([gl[l][k][s, 0] for k in ("sh", "sc", "gt")])
                      for l in range(depth) for s in (1, 0)])
    dmod = jnp.pad(dmod, ((0, 8 - 2 * depth), (0, 0)))
    dmod_all = _all_gather(dmod, name="gather_dmod")
    dmod_sum = _sum8(dmod_all, name="sum_dmod")
    grad_b_ada = jnp.stack([dmod_sum[2 * l] + dmod_sum[2 * l + 1] for l in range(depth)])
    grad_w_ada, ds_ctx = [], jnp.zeros((d_model,), F32)
    for l in range(depth):
        rows = jnp.concatenate([dmod_all[:, 2 * l], dmod_sum[2 * l + 1][None, :], jnp.zeros((7, 3 * d_model), F32)])
        mine = lax.dynamic_slice_in_dim(rows, me * w_ada_cols, w_ada_cols, axis=1)
        grad_w_ada.append(_mm(act.T, mine, name=f"ada{l}_dw"))
        ds_ctx = ds_ctx + _mm(mine, w_ada[l], nt=True, name=f"ada{l}_da")[8]
    grad_w_ada = jnp.stack(grad_w_ada)

    def scattered(g, axis, name):
        parts = g.reshape(g.shape[:axis] + (N_DEV, g.shape[axis] // N_DEV) + g.shape[axis + 1:])
        parts = jnp.moveaxis(parts, axis, 0)
        local = parts.shape[1:]
        return _reduce_scatter(parts.reshape(N_DEV, -1, local[-1]), name=name).reshape(local)

    w_in_slabs = _w_in_to_shards(jnp.stack([gl[l]["shadow_in"] for l in range(depth)]))
    grad_w_in = _reduce_scatter(w_in_slabs.reshape(N_DEV, depth * d_model, W_IN_SHARD), name="rs_w_in").reshape(w_in.shape)
    grad_w_out = scattered(jnp.stack([gl[l]["shadow_out"] for l in range(depth)]), 1, "rs_w_out")
    grad_w_glu = scattered(jnp.stack([gl[l]["shadow_glu"] for l in range(depth)]), 1, "rs_w_glu")

    stack = lambda k, shape=None: jnp.stack([gl[l][k] for l in range(depth)]).reshape(shape or given[k].shape)
    small = dict(norm_g=stack("norm_g"), hgrn_lb_logits=grads["lb_logits"], hgrn_norm_g=stack("hgrn_norm_g"),
                 gla_w_gk=stack("gla_w_gk", w_gk.shape), gla_b_gk=stack("gla_b_gk", b_gk.shape),
                 gla_norm_g=stack("gla_norm_g"), s5_d=stack("s5_d"), s5_b_glu=stack("s5_b_glu"),
                 ret_decay_logit=stack("ret_decay_logit"), ret_norm_g=stack("ret_norm_g"), final_norm_g=dg_final[0])
    for k in ("s5_lam_re", "s5_lam_im", "s5_log_dt", "s5_b_re", "s5_b_im", "s5_c_re", "s5_c_im"):
        small[k] = stack(k)
    pieces = [small[k] for k in _SMALL] + [ds_ctx]
    packed = jnp.concatenate([_pad_rows(a.reshape(-1), mult=1) for a in pieces], axis=0)
    packed = jnp.pad(packed, ((0, -packed.shape[0] % 8), (0, 0)))
    total = _sum8(_all_gather(packed, name="gather_small_grads"), name="sum_small_grads")
    reduced, off = {}, 0
    for k, a in zip(_SMALL + ("ds_ctx",), pieces):
        rows = -(-a.size // 128)
        reduced[k] = total[off:off + rows].reshape(-1)[:a.size].reshape(a.shape)
        off += rows

    def my_cols(a):
        w = a.shape[-1] // N_DEV
        return lax.dynamic_slice_in_dim(a, me * w, w, axis=a.ndim - 1)

    dsilu = jax.vjp(_silu, c_ctx)[1](reduced.pop("ds_ctx"))[0]
    grad = dict(reduced, c_ctx=dsilu, w_ada=grad_w_ada, b_ada=grad_b_ada, w_in=grad_w_in, w_out=grad_w_out,
                s5_w_glu=grad_w_glu)
    for k in ("hgrn_lb_logits", "gla_w_gk", "gla_b_gk"):
        grad[k] = my_cols(grad[k])

    delta, new_m, new_v = {}, {}, {}
    for k in _WEIGHTS:
        delta[k], new_m[k], new_v[k] = _adamw(given[k], grad[k], given["m_" + k], given["v_" + k], name="adamw_" + k)
    grad_x = grads["h0"][n_ctx:][None]
    return (loss, grad_x, *[grad[k] for k in _WEIGHTS], *[delta[k] for k in _WEIGHTS],
            *[new_m[k] for k in _WEIGHTS], *[new_v[k] for k in _WEIGHTS])
```

```python
import functools
import math

import jax
import jax.numpy as jnp
from jax import lax
from jax.experimental import pallas as pl
from jax.experimental.pallas import tpu as pltpu

F32 = jnp.float32
BF16 = jnp.bfloat16

EPS = 1e-6
BRANCH_W = 1024
CHUNK = 64
HEAD_DK = 128
A_MIN_FORGET = 1e-6
B_GATE_NORM = 16.0
B_GATE_RANK = 16
C_GROUP = 16
C_GROUPS = 64
C_STATE = 64
C_MAX_RE = -1e-4
GRID_W = 64
ROPE_BASE = 10000.0
N_DEV = 8

ADAM_LR = 0.001
ADAM_B1 = 0.9
ADAM_B2 = 0.999
ADAM_EPS = 1e-08
ADAM_WD = 0.01
ADAM_STEP = 10

VMEM_LIMIT = 52 * 1024 * 1024
SCAN_ROWS = 256
S5_ROWS = 128
S5_GB = 8
S5_GBW = S5_GB * C_STATE


def _tile(dim, cands):
    for c in cands:
        if dim % c == 0:
            return c
    return dim


def _cparams(sem):
    return pltpu.CompilerParams(dimension_semantics=sem, vmem_limit_bytes=VMEM_LIMIT)


MM_VMEM_BUDGET = 40 * 1024 * 1024


def _mm_tiles(M, N, K, sa, sb, so):
    tms = [t for t in (1088, 1024, 544, 512) if M % t == 0] or [M]
    tns = [t for t in (1024, 896, 640, 512, 384, 256) if N % t == 0] or [N]
    tks = [t for t in (K, 4480, 4352, 4096, 2688, 2176, 2048, 1920, 1024, 896, 640, 512, 384, 256, 128) if K % t == 0]
    fits = lambda tm, tn, tk: (2 * (tm * tk * sa + tk * tn * sb) + (tm * tn * 4 if tk < K else 0)
                               + 2 * tm * tn * so <= MM_VMEM_BUDGET)
    for tm in tms:
        for tn in tns:
            if fits(tm, tn, K):
                return tm, tn, K
    for tm in tms:
        for tn in tns:
            for tk in tks:
                if fits(tm, tn, tk):
                    return tm, tn, tk
    return tms[-1], tns[-1], tks[-1]


def _mm(a, b, *, nt=False, out_dtype=F32, name):
    M, K = a.shape
    N = b.shape[0] if nt else b.shape[1]
    tm, tn, tk = _mm_tiles(M, N, K, a.dtype.itemsize, b.dtype.itemsize, jnp.dtype(out_dtype).itemsize)
    nk = K // tk
    dims = (((1,), (1,)), ((), ())) if nt else (((1,), (0,)), ((), ()))

    if nk == 1:
        def body1(a_ref, b_ref, o_ref):
            o_ref[...] = lax.dot_general(a_ref[...].astype(BF16), b_ref[...].astype(BF16), dims,
                                         preferred_element_type=F32).astype(o_ref.dtype)

        b_spec1 = (pl.BlockSpec((tn, K), lambda i, j: (j, 0)) if nt else pl.BlockSpec((K, tn), lambda i, j: (0, j)))
        return pl.pallas_call(
            body1, name=name, grid=(M // tm, N // tn),
            in_specs=[pl.BlockSpec((tm, K), lambda i, j: (i, 0)), b_spec1],
            out_specs=pl.BlockSpec((tm, tn), lambda i, j: (i, j)),
            out_shape=jax.ShapeDtypeStruct((M, N), out_dtype),
            compiler_params=_cparams(("parallel", "arbitrary")),
        )(a, b)

    def body(a_ref, b_ref, o_ref, acc_ref):
        k = pl.program_id(2)

        @pl.when(k == 0)
        def _():
            acc_ref[...] = jnp.zeros_like(acc_ref)

        acc_ref[...] += lax.dot_general(a_ref[...].astype(BF16), b_ref[...].astype(BF16), dims,
                                        preferred_element_type=F32)

        @pl.when(k == nk - 1)
        def _():
            o_ref[...] = acc_ref[...].astype(o_ref.dtype)

    b_spec = (pl.BlockSpec((tn, tk), lambda i, j, k: (j, k)) if nt
              else pl.BlockSpec((tk, tn), lambda i, j, k: (k, j)))
    return pl.pallas_call(
        body, name=name, grid=(M // tm, N // tn, nk),
        in_specs=[pl.BlockSpec((tm, tk), lambda i, j, k: (i, k)), b_spec],
        out_specs=pl.BlockSpec((tm, tn), lambda i, j, k: (i, j)),
        out_shape=jax.ShapeDtypeStruct((M, N), out_dtype),
        scratch_shapes=[pltpu.VMEM((tm, tn), F32)],
        compiler_params=_cparams(("parallel", "parallel", "arbitrary")),
    )(a, b)


def _linear(name):
    @jax.custom_vjp
    def op(a, w, shadow):
        return _mm(a, w, name=name + "_fwd")

    def fwd(a, w, shadow):
        return _mm(a, w, name=name + "_fwd"), (a, w)

    def bwd(res, dy):
        a, w = res
        dy = dy.astype(BF16)
        da = _mm(dy, w, nt=True, out_dtype=a.dtype, name=name + "_da")
        dw = _mm(a.T, dy, name=name + "_dw")
        return da, jnp.zeros_like(w), dw

    op.defvjp(fwd, bwd)
    return op


def _mm_gather(a, b, sides, *, name):
    M, K = a.shape
    N = b.shape[1]
    tm, tn, tk = _mm_tiles(M, N, K, a.dtype.itemsize, b.dtype.itemsize, 4)
    assert tk == K
    gi, gj = M // tm, N // tn
    last = gi * gj - 1
    mid = last // 2
    ns = len(sides)

    def body(a_ref, b_ref, *refs):
        side_refs, o_ref, got_refs = refs[:ns], refs[ns], refs[ns + 1:2 * ns + 1]
        send_sems, recv_sems, local_sems = refs[2 * ns + 1:]
        step = pl.program_id(0) * gj + pl.program_id(1)
        x_, y_, c_ = _place()
        me, sibling = (x_, y_, c_), (x_, y_, 1 - c_)
        chips = [(1 - x_, y_), (x_, 1 - y_), (1 - x_, 1 - y_)]

        def gather(s):
            side_ref, got_ref = side_refs[s], got_refs[s]

            def slab(px, py, pc):
                return got_ref.at[4 * px + 2 * py + pc]

            def copy(k, block, to, src=None):
                return pltpu.make_async_remote_copy(
                    src_ref=slab(*block) if src is None else src, dst_ref=slab(*block), send_sem=send_sems.at[7 * s + k],
                    recv_sem=recv_sems.at[7 * s + k], device_id=to, device_id_type=MESH)

            mine = pltpu.make_async_copy(side_ref, slab(*me), local_sems.at[s])
            first = [copy(0, me, sibling, src=side_ref)] + [copy(1 + j, me, (*chip, c_), src=side_ref)
                                                             for j, chip in enumerate(chips)]
            passed = [copy(4 + j, (*chip, c_), sibling) for j, chip in enumerate(chips)]
            return copy, mine, first, passed

        parts = [gather(s) for s in range(ns)]

        @pl.when(step == 0)
        def _():
            for copy, mine, first, passed in parts:
                mine.start()
                for cp in first:
                    cp.start()

        o_ref[...] = jnp.dot(a_ref[...].astype(BF16), b_ref[...].astype(BF16), preferred_element_type=F32)

        @pl.when(step == mid)
        def _():
            for copy, mine, first, passed in parts:
                for j, chip in enumerate(chips):
                    copy(1 + j, (*chip, c_), me).wait_recv()
                    passed[j].start()

        @pl.when(step == last)
        def _():
            for copy, mine, first, passed in parts:
                copy(0, sibling, me).wait_recv()
                for j, chip in enumerate(chips):
                    copy(4 + j, (*chip, 1 - c_), me).wait_recv()
                for cp in first + passed:
                    cp.wait_send()
                mine.wait()

    return pl.pallas_call(
        body, name=name, grid=(gi, gj),
        in_specs=[pl.BlockSpec((tm, K), lambda i, j: (i, 0)), pl.BlockSpec((K, tn), lambda i, j: (0, j))] + [ANY] * ns,
        out_specs=[pl.BlockSpec((tm, tn), lambda i, j: (i, j))] + [ANY] * ns,
        out_shape=[jax.ShapeDtypeStruct((M, N), F32)]
        + [jax.ShapeDtypeStruct((N_DEV,) + s.shape, s.dtype) for s in sides],
        scratch_shapes=[pltpu.SemaphoreType.DMA((7 * ns,)), pltpu.SemaphoreType.DMA((7 * ns,)),
                        pltpu.SemaphoreType.DMA((ns,))],
        compiler_params=_cparams(("arbitrary", "arbitrary")),
    )(a, b, *sides)


def _linear_gather(name):
    @jax.custom_vjp
    def op(a, w, shadow, sides):
        return tuple(_mm_gather(a, w, sides, name=name + "_fwd"))

    def fwd(a, w, shadow, sides):
        return tuple(_mm_gather(a, w, sides, name=name + "_fwd")), (a, w, sides)

    def bwd(res, cts):
        a, w, sides = res
        dy = cts[0].astype(BF16)
        da = _mm(dy, w, nt=True, out_dtype=a.dtype, name=name + "_da")
        dw = _mm(a.T, dy, name=name + "_dw")
        return da, jnp.zeros_like(w), dw, tuple(jnp.zeros_like(s) for s in sides)

    op.defvjp(fwd, bwd)
    return op


def _rowwise(fn, specs, out_defs, *, ncb, tr, n_ctx, name):
    ncx = n_ctx // tr
    diff = [k for k, s in enumerate(specs) if s in ("row", "const", "col", "seg")]

    def in_spec(s, arr):
        if s == "row":
            return pl.BlockSpec((tr, arr.shape[1] // ncb), lambda j, i: (i, j))
        if s == "rowc":
            return pl.BlockSpec((tr, arr.shape[1]), lambda j, i: (i, 0))
        if s in ("const", "fixed"):
            return pl.BlockSpec(arr.shape, lambda j, i: (0, 0))
        if s == "col":
            return pl.BlockSpec((None,) + arr.shape[1:], lambda j, i: (j, 0, 0))
        if s == "seg":
            return pl.BlockSpec((None,) + arr.shape[1:], lambda j, i: ((i >= ncx).astype(jnp.int32), 0, 0))
        raise ValueError(s)

    def fwd_call(*arrays):
        L = arrays[specs.index("row")].shape[0]
        n_in = len(arrays)

        def body(*refs):
            outs = fn(*[r[...] for r in refs[:n_in]])
            for o_ref, o in zip(refs[n_in:], outs):
                o_ref[...] = o.astype(o_ref.dtype)

        return pl.pallas_call(
            body, name=name + "_fwd", grid=(ncb, L // tr),
            in_specs=[in_spec(s, a) for s, a in zip(specs, arrays)],
            out_specs=[pl.BlockSpec((tr, c // ncb), lambda j, i: (i, j)) for c, _ in out_defs],
            out_shape=[jax.ShapeDtypeStruct((L, c), dt) for c, dt in out_defs],
            compiler_params=_cparams(("parallel", "arbitrary")),
        )(*arrays)

    def bwd_call(arrays, cts):
        L = arrays[specs.index("row")].shape[0]
        n_in, n_ct = len(arrays), len(cts)

        def body(*refs):
            i = pl.program_id(1)
            vals = [r[...] for r in refs[:n_in]]
            ct_vals = tuple(r[...] for r in refs[n_in:n_in + n_ct])
            g_refs = refs[n_in + n_ct:]

            def f(*dv):
                full = list(vals)
                for k, v in zip(diff, dv):
                    full[k] = v
                return tuple(o.astype(dt) for o, (_, dt) in zip(fn(*full), out_defs))

            _, vjp = jax.vjp(f, *[vals[k] for k in diff])
            grads = vjp(ct_vals)
            for k, g_ref, g in zip(diff, g_refs, grads):
                if specs[k] == "row":
                    g_ref[...] = g.astype(g_ref.dtype)
                else:
                    first = (i == 0) | (i == ncx) if specs[k] == "seg" else (i == 0)

                    @pl.when(first)
                    def _(g_ref=g_ref, g=g):
                        g_ref[...] = g

                    @pl.when(jnp.logical_not(first))
                    def _(g_ref=g_ref, g=g):
                        g_ref[...] += g

        out_specs, out_shape = [], []
        for k in diff:
            s, a = specs[k], arrays[k]
            if s == "row":
                out_specs.append(pl.BlockSpec((tr, a.shape[1] // ncb), lambda j, i: (i, j)))
                out_shape.append(jax.ShapeDtypeStruct(a.shape, a.dtype))
            elif s == "const":
                out_specs.append(pl.BlockSpec((None,) + a.shape, lambda j, i: (j, 0, 0)))
                out_shape.append(jax.ShapeDtypeStruct((ncb,) + a.shape, F32))
            elif s == "col":
                out_specs.append(pl.BlockSpec((None,) + a.shape[1:], lambda j, i: (j, 0, 0)))
                out_shape.append(jax.ShapeDtypeStruct(a.shape, F32))
            else:
                out_specs.append(pl.BlockSpec((None, None) + a.shape[1:],
                                              lambda j, i: (j, (i >= ncx).astype(jnp.int32), 0, 0)))
                out_shape.append(jax.ShapeDtypeStruct((ncb,) + a.shape, F32))
        ct_specs = [pl.BlockSpec((tr, c // ncb), lambda j, i: (i, j)) for c, _ in out_defs]
        grads = pl.pallas_call(
            body, name=name + "_bwd", grid=(ncb, L // tr),
            in_specs=[in_spec(s, a) for s, a in zip(specs, arrays)] + ct_specs,
            out_specs=out_specs, out_shape=out_shape,
            compiler_params=_cparams(("parallel", "arbitrary")),
        )(*arrays, *cts)
        full = [jnp.zeros_like(a) for a in arrays]
        for k, g in zip(diff, grads):
            full[k] = jnp.sum(g, axis=0) if specs[k] in ("const", "seg") else g
        return tuple(full)

    @jax.custom_vjp
    def op(*arrays):
        return tuple(fwd_call(*arrays))

    def op_fwd(*arrays):
        return tuple(fwd_call(*arrays)), arrays

    def op_bwd(arrays, cts):
        return bwd_call(arrays, cts)

    op.defvjp(op_fwd, op_bwd)
    return op


def _sigmoid(x):
    return 1.0 / (1.0 + jnp.exp(-x))


def _log_sigmoid(x):
    return jnp.minimum(x, 0.0) - jnp.log(1.0 + jnp.exp(-jnp.abs(x)))


def _silu(x):
    return x * _sigmoid(x)


def _rms(x, g):
    return x * lax.rsqrt(jnp.mean(x * x, axis=-1, keepdims=True) + EPS) * g


def _modnorm_fn(h, g, sc, sh):
    return ((_rms(h, g) * (1.0 + sc) + sh).astype(BF16),)


def _resid_fn(h, y, gt):
    return (h + gt * y,)


def _hgrn_pro_fn(zf, zb, lbf, lbb):
    def one(z, lb):
        f = lb + (1.0 - lb) * _sigmoid(z)
        return (1.0 - lb) * _sigmoid(-z), jnp.log(jnp.maximum(f, A_MIN_FORGET))

    kf, gf = one(zf, lbf)
    kb, gb = one(zb, lbb)
    return kf, gf, kb, gb


def _hgrn_epi_fn(of, ob, gate, g):
    return ((_rms(of + ob, g) * _silu(gate)).astype(BF16),)


def _gla_pro_fn(q, lr, wf, wb, bf, bb):
    gf = _log_sigmoid(jnp.dot(lr, wf, preferred_element_type=F32) + bf) / B_GATE_NORM
    gb = _log_sigmoid(jnp.dot(lr, wb, preferred_element_type=F32) + bb) / B_GATE_NORM
    return q * (HEAD_DK ** -0.5), gf, gb


def _ret_pro_fn(q, k, cos2, sin2, lgf, lgb, swap):
    def rope(x):
        return x * cos2 + jnp.dot(x, swap, preferred_element_type=F32, precision=lax.Precision.HIGHEST) * sin2

    zero = jnp.zeros_like(q)
    return rope(q) * (HEAD_DK ** -0.5), rope(k), zero + lgf, zero + lgb


def _ret_epi_fn(of, ob, gate, g):
    x = of + ob
    mu = jnp.mean(x, axis=-1, keepdims=True)
    var = jnp.mean(jnp.square(x - mu), axis=-1, keepdims=True)
    return (((x - mu) * lax.rsqrt(var + EPS) * g * _silu(gate)).astype(BF16),)


def _s5_mid_fn(yf, yb, u, d):
    return (jax.nn.gelu(yf + yb + d * u),)


def _s5_epi_fn(z, t, gate, b):
    return ((z * _sigmoid(t + b) * _silu(gate)).astype(BF16),)


def _blk(mode, i, nb, nc):
    if mode == 0:
        return i
    if mode == 1:
        return nb - 1 - i
    if mode == 2:
        return jnp.where(i < nc, nc - 1 - i, nb - 1 - i + nc)
    return jnp.where(i < nb - nc, i + nc, nc - nb + i)


def _split3(x):
    hi = x.astype(BF16)
    r1 = x - hi.astype(F32)
    mid = r1.astype(BF16)
    lo = (r1 - mid.astype(F32)).astype(BF16)
    return hi, mid, lo


def _cum(tri, x):
    hi, mid, lo = _split3(x)
    d = lambda v: jnp.dot(tri, v, preferred_element_type=F32)
    return d(hi) + d(mid) + d(lo)


def _dot(a, b):
    return jnp.dot(a.astype(BF16), b.astype(BF16), preferred_element_type=F32)


def _dot_nt(a, b):
    return lax.dot_general(a.astype(BF16), b.astype(BF16), (((1,), (1,)), ((), ())), preferred_element_type=F32)


def _dot_tn(a, b):
    return jnp.dot(a.astype(F32).T.astype(BF16), b.astype(BF16), preferred_element_type=F32)


def _dot3(a, b):
    ah = a.astype(BF16)
    al = (a - ah.astype(F32)).astype(BF16)
    bh = b.astype(BF16)
    bl = (b - bh.astype(F32)).astype(BF16)
    d = lambda p, q: jnp.dot(p, q, preferred_element_type=F32)
    return d(ah, bh) + d(ah, bl) + d(al, bh)


def _dot3_tn(a, b):
    return _dot3(a.astype(F32).T, b)


def _gla_masks(flip):
    r = lax.broadcasted_iota(jnp.int32, (CHUNK, CHUNK), 0)
    c = lax.broadcasted_iota(jnp.int32, (CHUNK, CHUNK), 1)
    causal = (c >= r) if flip else (c <= r)
    tri = causal.astype(BF16)
    tri_t = jnp.logical_not(causal) | (r == c)
    return causal, tri, tri_t.astype(BF16)


def _gla_chunk_terms(q, k, g, tri, flip):
    b = _cum(tri, g)
    mid = CHUNK // 2
    bm = b[mid:mid + 1, :]
    bl = b[0:1, :] if flip else b[CHUNK - 1:CHUNK, :]
    e_q = jnp.exp(b - bm)
    e_k = jnp.exp(bm - b)
    return b, bl, e_q, e_k, q * e_q, k * e_k


def _gla_fwd_call(q, k, v, g, *, flip, n_ctx, name):
    L = q.shape[0]
    H = q.shape[1] // HEAD_DK
    dv = v.shape[1] // H
    nb, nc, sub = L // SCAN_ROWS, n_ctx // SCAN_ROWS, SCAN_ROWS // CHUNK
    mode = 2 if flip else 0
    order = list(range(sub))[::-1] if flip else list(range(sub))

    def body(q_ref, k_ref, v_ref, g_ref, o_ref, sp_ref, st_ref):
        @pl.when(pl.program_id(1) == 0)
        def _():
            st_ref[...] = jnp.zeros_like(st_ref)

        causal, tri, _ = _gla_masks(flip)
        for s in order:
            sl = pl.ds(s * CHUNK, CHUNK)
            qc, kc, vc, gc = q_ref[sl, :], k_ref[sl, :], v_ref[sl, :], g_ref[sl, :]
            b, bl, _, _, qt, kt = _gla_chunk_terms(qc, kc, gc, tri, flip)
            a = jnp.where(causal, _dot_nt(qt, kt), 0.0)
            sp = st_ref[...]
            sp_ref[s] = sp
            o_ref[sl, :] = _dot(a, vc) + _dot_nt(qc * jnp.exp(b), sp)
            st_ref[...] = sp * jnp.exp(bl) + _dot3_tn(vc, kc * jnp.exp(bl - b))

    row = lambda w: pl.BlockSpec((SCAN_ROWS, w), lambda h, i: (_blk(mode, i, nb, nc), h))
    return pl.pallas_call(
        body, name=name, grid=(H, nb),
        in_specs=[row(HEAD_DK), row(HEAD_DK), row(dv), row(HEAD_DK)],
        out_specs=[row(dv), pl.BlockSpec((None, None, sub, dv, HEAD_DK),
                                         lambda h, i: (h, _blk(mode, i, nb, nc), 0, 0, 0))],
        out_shape=[jax.ShapeDtypeStruct((L, H * dv), F32),
                   jax.ShapeDtypeStruct((H, nb, sub, dv, HEAD_DK), F32)],
        scratch_shapes=[pltpu.VMEM((dv, HEAD_DK), F32)],
        compiler_params=_cparams(("parallel", "arbitrary")),
    )(q, k, v, g)


def _gla_bwd_call(q, k, v, g, sp, do, *, flip, n_ctx, name):
    L = q.shape[0]
    H = q.shape[1] // HEAD_DK
    dv = v.shape[1] // H
    nb, nc, sub = L // SCAN_ROWS, n_ctx // SCAN_ROWS, SCAN_ROWS // CHUNK
    mode = 3 if flip else 1
    order = list(range(sub)) if flip else list(range(sub))[::-1]
    last = 0 if flip else CHUNK - 1

    def body(q_ref, k_ref, v_ref, g_ref, sp_ref, do_ref, dq_ref, dk_ref, dv_ref, dg_ref, ds_ref):
        @pl.when(pl.program_id(1) == 0)
        def _():
            ds_ref[...] = jnp.zeros_like(ds_ref)

        causal, tri, tri_t = _gla_masks(flip)
        is_last = lax.broadcasted_iota(jnp.int32, (CHUNK, HEAD_DK), 0) == last
        for s in order:
            sl = pl.ds(s * CHUNK, CHUNK)
            qc, kc, vc, gc, doc = q_ref[sl, :], k_ref[sl, :], v_ref[sl, :], g_ref[sl, :], do_ref[sl, :]
            b, bl, e_q, e_k, qt, kt = _gla_chunk_terms(qc, kc, gc, tri, flip)
            e_b, e_l = jnp.exp(b), jnp.exp(bl - b)
            a = jnp.where(causal, _dot_nt(qt, kt), 0.0)
            p, kd = qc * e_b, kc * e_l
            spv, ds = sp_ref[s], ds_ref[...]
            dv_ref[sl, :] = _dot_tn(a, doc) + _dot_nt(kd, ds)
            da = jnp.where(causal, _dot_nt(doc, vc), 0.0)
            dkd = _dot3(vc, ds)
            dq = _dot3(da, kt) * e_q + _dot3(doc, spv) * e_b
            dk = _dot3_tn(da, qt) * e_k + dkd * e_l
            dq_ref[sl, :] = dq
            dk_ref[sl, :] = dk
            db_last = (jnp.sum(dkd * kd, axis=0, keepdims=True)
                       + jnp.exp(bl) * jnp.sum(ds * spv, axis=0, keepdims=True))
            db = qc * dq - kc * dk + jnp.where(is_last, db_last, 0.0)
            dg_ref[sl, :] = _cum(tri_t, db)
            ds_ref[...] = ds * jnp.exp(bl) + _dot3_tn(doc, p)

    row = lambda w: pl.BlockSpec((SCAN_ROWS, w), lambda h, i: (_blk(mode, i, nb, nc), h))
    return pl.pallas_call(
        body, name=name, grid=(H, nb),
        in_specs=[row(HEAD_DK), row(HEAD_DK), row(dv), row(HEAD_DK),
                  pl.BlockSpec((None, None, sub, dv, HEAD_DK), lambda h, i: (h, _blk(mode, i, nb, nc), 0, 0, 0)),
                  row(dv)],
        out_specs=[row(HEAD_DK), row(HEAD_DK), row(dv), row(HEAD_DK)],
        out_shape=[jax.ShapeDtypeStruct(q.shape, F32), jax.ShapeDtypeStruct(k.shape, F32),
                   jax.ShapeDtypeStruct(v.shape, F32), jax.ShapeDtypeStruct(g.shape, F32)],
        scratch_shapes=[pltpu.VMEM((dv, HEAD_DK), F32)],
        compiler_params=_cparams(("parallel", "arbitrary")),
    )(q, k, v, g, sp, do)


def _gla(name, *, flip, n_ctx):
    @jax.custom_vjp
    def op(q, k, v, g):
        return _gla_fwd_call(q, k, v, g, flip=flip, n_ctx=n_ctx, name=name + "_fwd")[0]

    def fwd(q, k, v, g):
        o, sp = _gla_fwd_call(q, k, v, g, flip=flip, n_ctx=n_ctx, name=name + "_fwd")
        return o, (q, k, v, g, sp)

    def bwd(res, do):
        return tuple(_gla_bwd_call(*res, do, flip=flip, n_ctx=n_ctx, name=name + "_bwd"))

    op.defvjp(fwd, bwd)
    return op


def _s5_table(ar, ai, desc):
    def mul(p, q):
        return p[0] * q[0] - p[1] * q[1], p[0] * q[1] + p[1] * q[0]

    a1 = (ar.reshape(-1), ai.reshape(-1))
    ar = a1[0]
    a2 = mul(a1, a1)
    a4 = mul(a2, a2)
    pw = [a1]
    for _ in range(7):
        pw.append(mul(pw[-1], a1))
    if desc:
        pw = pw[::-1]
    zero = jnp.zeros_like(ar)
    rows = [a1[0], a1[1], a2[0], a2[1], a4[0], a4[1], zero, zero] + [p[0] for p in pw] + [p[1] for p in pw]
    return jnp.stack(rows, axis=0)


S5_LANES = C_GROUPS * C_STATE
S5_LC = 1024


def _s5_scan(xr_ref, xi_ref, tab_ref, car_ref, desc, on_slab=None):
    nslab = S5_ROWS // 8
    rows = lax.broadcasted_iota(jnp.int32, (8, S5_LC), 0)

    def slab(t, carry):
        r0 = pl.multiple_of(((nslab - 1 - t) if desc else t) * 8, 8)
        for lc in range(S5_LANES // S5_LC):
            cs = pl.ds(lc * S5_LC, S5_LC)
            xr, xi = xr_ref[pl.ds(r0, 8), cs], xi_ref[pl.ds(r0, 8), cs]
            for n, s in enumerate((1, 2, 4)):
                ar, ai = tab_ref[2 * n:2 * n + 1, cs], tab_ref[2 * n + 1:2 * n + 2, cs]
                keep = (rows < 8 - s) if desc else (rows >= s)
                sr = jnp.where(keep, pltpu.roll(xr, (8 - s) if desc else s, 0), 0.0)
                si = jnp.where(keep, pltpu.roll(xi, (8 - s) if desc else s, 0), 0.0)
                xr, xi = xr + ar * sr - ai * si, xi + ar * si + ai * sr
            cr, ci = car_ref[0:1, cs], car_ref[1:2, cs]
            pr, pi = tab_ref[8:16, cs], tab_ref[16:24, cs]
            xr, xi = xr + pr * cr - pi * ci, xi + pr * ci + pi * cr
            if on_slab is not None:
                on_slab(r0, cs, xr, xi, cr, ci, rows)
            xr_ref[pl.ds(r0, 8), cs] = xr
            xi_ref[pl.ds(r0, 8), cs] = xi
            e = 0 if desc else 7
            car_ref[0:1, cs] = xr[e:e + 1, :]
            car_ref[1:2, cs] = xi[e:e + 1, :]
        return carry

    lax.fori_loop(0, nslab, slab, 0)


def _s5_fwd_call(u, tab, wb_re, wb_im, wc_re, wc_im, *, flip, n_ctx, name):
    L = u.shape[0]
    nb, nc = L // S5_ROWS, n_ctx // S5_ROWS
    mode, desc = (2, True) if flip else (0, False)

    def body(u_ref, tab_ref, wbr_ref, wbi_ref, wcr_ref, wci_ref, y_ref, xr_ref, xi_ref, car_ref):
        @pl.when(pl.program_id(0) == 0)
        def _():
            car_ref[...] = jnp.zeros_like(car_ref)

        for gb in range(S5_GB):
            us, cs = pl.ds(gb * 128, 128), pl.ds(gb * S5_GBW, S5_GBW)
            ub = u_ref[:, us]
            xr_ref[:, cs] = _dot(ub, wbr_ref[gb])
            xi_ref[:, cs] = _dot(ub, wbi_ref[gb])
        _s5_scan(xr_ref, xi_ref, tab_ref, car_ref, desc)
        for gb in range(S5_GB):
            us, cs = pl.ds(gb * 128, 128), pl.ds(gb * S5_GBW, S5_GBW)
            y_ref[:, us] = _dot(xr_ref[:, cs], wcr_ref[gb]) - _dot(xi_ref[:, cs], wci_ref[gb])

    row = lambda w: pl.BlockSpec((S5_ROWS, w), lambda i: (_blk(mode, i, nb, nc), 0))
    full = lambda a: pl.BlockSpec(a.shape, lambda i: (0,) * a.ndim)
    return pl.pallas_call(
        body, name=name, grid=(nb,),
        in_specs=[row(BRANCH_W), full(tab), full(wb_re), full(wb_im), full(wc_re), full(wc_im)],
        out_specs=[row(BRANCH_W), row(S5_LANES), row(S5_LANES)],
        out_shape=[jax.ShapeDtypeStruct((L, BRANCH_W), F32), jax.ShapeDtypeStruct((L, S5_LANES), F32),
                   jax.ShapeDtypeStruct((L, S5_LANES), F32)],
        scratch_shapes=[pltpu.VMEM((8, S5_LANES), F32)],
        compiler_params=_cparams(("arbitrary",)),
    )(u, tab, wb_re, wb_im, wc_re, wc_im)


def _s5_bwd_call(dy, u, xr, xi, tab, wbt_re, wbt_im, wct_re, wct_im, *, flip, n_ctx, name):
    L = u.shape[0]
    nb, nc = L // S5_ROWS, n_ctx // S5_ROWS
    mode, desc = (3, False) if flip else (1, True)

    def body(dy_ref, u_ref, xr_ref, xi_ref, tab_ref, wbtr_ref, wbti_ref, wctr_ref, wcti_ref,
             du_ref, dwbr_ref, dwbi_ref, dwcr_ref, dwci_ref, ga_ref, gr_ref, gi_ref, car_ref):
        @pl.when(pl.program_id(0) == 0)
        def _():
            car_ref[...] = jnp.zeros_like(car_ref)
            ga_ref[...] = jnp.zeros_like(ga_ref)
            for r in (dwbr_ref, dwbi_ref, dwcr_ref, dwci_ref):
                r[...] = jnp.zeros_like(r)

        for gb in range(S5_GB):
            us, cs = pl.ds(gb * 128, 128), pl.ds(gb * S5_GBW, S5_GBW)
            dyb = dy_ref[:, us]
            gr_ref[:, cs] = _dot(dyb, wctr_ref[gb])
            gi_ref[:, cs] = -_dot(dyb, wcti_ref[gb])

        def lam_grad(r0, cs, gr, gi, cr, ci, rows):
            edge = 7 if desc else 0
            pr = jnp.where(rows == edge, cr, pltpu.roll(gr, 7 if desc else 1, 0))
            pi = jnp.where(rows == edge, ci, pltpu.roll(gi, 7 if desc else 1, 0))
            x_r, x_i = xr_ref[pl.ds(r0, 8), cs], xi_ref[pl.ds(r0, 8), cs]
            ga_ref[0:8, cs] += x_r * pr + x_i * pi
            ga_ref[8:16, cs] += x_r * pi - x_i * pr

        _s5_scan(gr_ref, gi_ref, tab_ref, car_ref, desc, on_slab=lam_grad)
        for gb in range(S5_GB):
            us, cs = pl.ds(gb * 128, 128), pl.ds(gb * S5_GBW, S5_GBW)
            g_r, g_i, ub, dyb = gr_ref[:, cs], gi_ref[:, cs], u_ref[:, us], dy_ref[:, us]
            du_ref[:, us] = _dot(g_r, wbtr_ref[gb]) + _dot(g_i, wbti_ref[gb])
            dwbr_ref[gb] += _dot_tn(ub, g_r)
            dwbi_ref[gb] += _dot_tn(ub, g_i)
            dwcr_ref[gb] += _dot_tn(xr_ref[:, cs], dyb)
            dwci_ref[gb] -= _dot_tn(xi_ref[:, cs], dyb)

    row = lambda w: pl.BlockSpec((S5_ROWS, w), lambda i: (_blk(mode, i, nb, nc), 0))
    full = lambda shape: pl.BlockSpec(shape, lambda i: (0,) * len(shape))
    wb_shape, wc_shape = (S5_GB, 128, S5_GBW), (S5_GB, S5_GBW, 128)
    return pl.pallas_call(
        body, name=name, grid=(nb,),
        in_specs=[row(BRANCH_W), row(BRANCH_W), row(S5_LANES), row(S5_LANES), full(tab.shape),
                  full(wc_shape), full(wc_shape), full(wb_shape), full(wb_shape)],
        out_specs=[row(BRANCH_W), full(wb_shape), full(wb_shape), full(wc_shape), full(wc_shape),
                   full((16, S5_LANES))],
        out_shape=[jax.ShapeDtypeStruct((L, BRANCH_W), F32),
                   jax.ShapeDtypeStruct(wb_shape, F32), jax.ShapeDtypeStruct(wb_shape, F32),
                   jax.ShapeDtypeStruct(wc_shape, F32), jax.ShapeDtypeStruct(wc_shape, F32),
                   jax.ShapeDtypeStruct((16, S5_LANES), F32)],
        scratch_shapes=[pltpu.VMEM((S5_ROWS, S5_LANES), F32), pltpu.VMEM((S5_ROWS, S5_LANES), F32),
                        pltpu.VMEM((8, S5_LANES), F32)],
        compiler_params=_cparams(("arbitrary",)),
    )(dy, u, xr, xi, tab, wbt_re, wbt_im, wct_re, wct_im)


def _s5(name, *, flip, n_ctx):
    desc_f = flip
    desc_b = not flip

    def run_fwd(u, lr, li, wbr, wbi, wcr, wci):
        tab = _s5_table(lr, li, desc_f)
        c16 = lambda a: a.astype(BF16)
        return _s5_fwd_call(u, tab, c16(wbr), c16(wbi), c16(wcr), c16(wci), flip=flip, n_ctx=n_ctx, name=name + "_fwd")

    @jax.custom_vjp
    def op(u, lr, li, wbr, wbi, wcr, wci):
        return run_fwd(u, lr, li, wbr, wbi, wcr, wci)[0]

    def fwd(u, lr, li, wbr, wbi, wcr, wci):
        y, xr, xi = run_fwd(u, lr, li, wbr, wbi, wcr, wci)
        return y, (u, xr, xi, lr, li, wbr, wbi, wcr, wci)

    def bwd(res, dy):
        u, xr, xi, lr, li, wbr, wbi, wcr, wci = res
        tab = _s5_table(lr, -li, desc_b)
        t16 = lambda a: jnp.swapaxes(a, 1, 2).astype(BF16)
        du, dwbr, dwbi, dwcr, dwci, ga = _s5_bwd_call(
            dy, u, xr, xi, tab, t16(wbr), t16(wbi), t16(wcr), t16(wci), flip=flip, n_ctx=n_ctx, name=name + "_bwd")
        return (du, jnp.sum(ga[0:8], axis=0, keepdims=True), jnp.sum(ga[8:16], axis=0, keepdims=True),
                dwbr, dwbi, dwcr, dwci)

    op.defvjp(fwd, bwd)
    return op


def _loss_call(h, tgt, g, *, n_ctx, name):
    L, D = h.shape
    tr = 128
    ncx = n_ctx // tr

    def body(h_ref, t_ref, g_ref, dh_ref, dg_ref, loss_ref):
        i = pl.program_id(0)

        @pl.when(i == 0)
        def _():
            dg_ref[...] = jnp.zeros_like(dg_ref)
            loss_ref[...] = jnp.zeros_like(loss_ref)

        @pl.when(i < ncx)
        def _():
            dh_ref[...] = jnp.zeros_like(dh_ref)

        @pl.when(i >= ncx)
        def _():
            t = t_ref[...]

            def f(hv, gv):
                e = _rms(hv, gv) - t
                return 0.5 * jnp.sum(e * e) / D

            l, (dh, dg) = jax.value_and_grad(f, argnums=(0, 1))(h_ref[...], g_ref[...])
            dh_ref[...] = dh
            dg_ref[...] += dg
            loss_ref[...] += l

    return pl.pallas_call(
        body, name=name, grid=(L // tr,),
        in_specs=[pl.BlockSpec((tr, D), lambda i: (i, 0)),
                  pl.BlockSpec((tr, D), lambda i: (jnp.maximum(i - ncx, 0), 0)),
                  pl.BlockSpec((1, D), lambda i: (0, 0))],
        out_specs=[pl.BlockSpec((tr, D), lambda i: (i, 0)), pl.BlockSpec((1, D), lambda i: (0, 0)),
                   pl.BlockSpec((8, 128), lambda i: (0, 0))],
        out_shape=[jax.ShapeDtypeStruct((L, D), F32), jax.ShapeDtypeStruct((1, D), F32),
                   jax.ShapeDtypeStruct((8, 128), F32)],
        compiler_params=_cparams(("arbitrary",)),
    )(h, tgt, g)


def _adamw_call(w, g, m, v, *, name):
    R, C = w.shape
    tr = R if R * C * 4 <= (1 << 20) else _tile(R, tuple(t for t in (512, 256, 128, 64, 32, 16, 8) if t * C * 4 <= (1 << 20)))
    c1 = 1.0 / (1.0 - ADAM_B1 ** ADAM_STEP)
    c2 = 1.0 / (1.0 - ADAM_B2 ** ADAM_STEP)

    def body(w_ref, g_ref, m_ref, v_ref, d_ref, nm_ref, nv_ref):
        gv = g_ref[...]
        nm = ADAM_B1 * m_ref[...] + (1.0 - ADAM_B1) * gv
        nv = ADAM_B2 * v_ref[...] + (1.0 - ADAM_B2) * gv * gv
        d_ref[...] = -ADAM_LR * ((nm * c1) / (jnp.sqrt(nv * c2) + ADAM_EPS) + ADAM_WD * w_ref[...])
        nm_ref[...] = nm
        nv_ref[...] = nv

    spec = pl.BlockSpec((tr, C), lambda i: (i, 0))
    return pl.pallas_call(
        body, name=name, grid=(R // tr,), in_specs=[spec] * 4, out_specs=[spec] * 3,
        out_shape=[jax.ShapeDtypeStruct((R, C), F32)] * 3,
        compiler_params=_cparams(("parallel",)),
    )(w, g, m, v)


def _as2d(a):
    n = a.size
    if a.ndim >= 2 and n >= (1 << 20):
        return a.reshape(-1, a.shape[-1])
    if n % 128 == 0:
        return a.reshape(n // 128, 128)
    return a.reshape(1, n)


def _adamw(w, g, m, v, *, name):
    d, nm, nv = _adamw_call(_as2d(w), _as2d(g.astype(F32)), _as2d(m), _as2d(v), name=name)
    return d.reshape(w.shape), nm.reshape(w.shape), nv.reshape(w.shape)


MESH = pl.DeviceIdType.MESH
ANY = pl.BlockSpec(memory_space=pl.ANY)


def _place():
    return lax.axis_index("x"), lax.axis_index("y"), lax.axis_index("c")


def _all_gather(x, *, name):
    M, N = x.shape

    def body(x_ref, out_ref, send_sems, recv_sems, local_sem):
        x_, y_, c_ = _place()
        me, sibling = (x_, y_, c_), (x_, y_, 1 - c_)
        chips = [(1 - x_, y_), (x_, 1 - y_), (1 - x_, 1 - y_)]

        def slab(px, py, pc):
            return out_ref.at[4 * px + 2 * py + pc]

        def copy(k, block, to, src=None):
            return pltpu.make_async_remote_copy(
                src_ref=slab(*block) if src is None else src, dst_ref=slab(*block),
                send_sem=send_sems.at[k], recv_sem=recv_sems.at[k], device_id=to, device_id_type=MESH)

        mine = pltpu.make_async_copy(x_ref, slab(*me), local_sem)
        mine.start()
        first = [copy(0, me, sibling, src=x_ref)]
        first += [copy(1 + j, me, (*chip, c_), src=x_ref) for j, chip in enumerate(chips)]
        for cp in first:
            cp.start()
        passed = [copy(4 + j, (*chip, c_), sibling) for j, chip in enumerate(chips)]
        for j, chip in enumerate(chips):
            copy(1 + j, (*chip, c_), me).wait_recv()
            passed[j].start()
        copy(0, sibling, me).wait_recv()
        for j, chip in enumerate(chips):
            copy(4 + j, (*chip, 1 - c_), me).wait_recv()
        for cp in first + passed:
            cp.wait_send()
        mine.wait()

    return pl.pallas_call(
        body, name=name, out_shape=jax.ShapeDtypeStruct((N_DEV, M, N), x.dtype),
        in_specs=[ANY], out_specs=ANY,
        scratch_shapes=[pltpu.SemaphoreType.DMA((7,)), pltpu.SemaphoreType.DMA((7,)), pltpu.SemaphoreType.DMA],
    )(x)


def _pair_exchange(g, *, name):
    _, R, C = g.shape

    def body(g_ref, out_ref, send_sems, recv_sems):
        x_, y_, c_ = _place()
        copies = []
        for k in range(4):
            cp = pltpu.make_async_remote_copy(
                src_ref=g_ref.at[2 * k + (1 - c_)], dst_ref=out_ref.at[k], send_sem=send_sems.at[k],
                recv_sem=recv_sems.at[k], device_id=(x_, y_, 1 - c_), device_id_type=MESH)
            cp.start()
            copies.append(cp)
        for cp in copies:
            cp.wait()

    return pl.pallas_call(
        body, name=name, out_shape=jax.ShapeDtypeStruct((4, R, C), g.dtype), in_specs=[ANY], out_specs=ANY,
        scratch_shapes=[pltpu.SemaphoreType.DMA((4,)), pltpu.SemaphoreType.DMA((4,))],
    )(g)


def _chip_exchange(p, *, name):
    _, R, C = p.shape

    def body(p_ref, out_ref, send_sems, recv_sems):
        x_, y_, c_ = _place()
        chips = [(1 - x_, y_), (x_, 1 - y_), (1 - x_, 1 - y_)]
        copies = []
        for j, (cx, cy) in enumerate(chips):
            cp = pltpu.make_async_remote_copy(
                src_ref=p_ref.at[2 * cx + cy], dst_ref=out_ref.at[j], send_sem=send_sems.at[j],
                recv_sem=recv_sems.at[j], device_id=(cx, cy, c_), device_id_type=MESH)
            cp.start()
            copies.append(cp)
        for cp in copies:
            cp.wait()

    return pl.pallas_call(
        body, name=name, out_shape=jax.ShapeDtypeStruct((3, R, C), p.dtype), in_specs=[ANY], out_specs=ANY,
        scratch_shapes=[pltpu.SemaphoreType.DMA((3,)), pltpu.SemaphoreType.DMA((3,))],
    )(p)


def _add_tr(R, C):
    return _tile(R, tuple(t for t in (512, 256, 128, 64, 32, 16, 8) if t * C * 4 <= (2 << 20)))


def _pair_add(g, recv, *, name):
    _, R, C = g.shape
    tr = _add_tr(R, C)
    core = lax.axis_index("c").astype(jnp.int32).reshape(1)

    def body(c_ref, g_ref, r_ref, o_ref):
        o_ref[...] = (g_ref[...] + r_ref[...]).astype(o_ref.dtype)

    return pl.pallas_call(
        body, name=name, out_shape=jax.ShapeDtypeStruct((4, R, C), BF16),
        grid_spec=pltpu.PrefetchScalarGridSpec(
            num_scalar_prefetch=1, grid=(4, R // tr),
            in_specs=[pl.BlockSpec((None, tr, C), lambda k, i, c: (2 * k + c[0], i, 0)),
                      pl.BlockSpec((None, tr, C), lambda k, i, c: (k, i, 0))],
            out_specs=pl.BlockSpec((None, tr, C), lambda k, i, c: (k, i, 0))),
        compiler_params=_cparams(("parallel", "parallel")),
    )(core, g, recv)


def _chip_add(p, recv, *, name):
    _, R, C = p.shape
    tr = _add_tr(R, C)
    chip = (2 * lax.axis_index("x") + lax.axis_index("y")).astype(jnp.int32).reshape(1)

    def body(c_ref, p_ref, r0_ref, r1_ref, r2_ref, o_ref):
        f = lambda r: r[...].astype(F32)
        o_ref[...] = ((f(p_ref) + f(r0_ref)) + f(r1_ref)) + f(r2_ref)

    rspec = lambda j: pl.BlockSpec((None, tr, C), lambda i, c: (j, i, 0))
    return pl.pallas_call(
        body, name=name, out_shape=jax.ShapeDtypeStruct((R, C), F32),
        grid_spec=pltpu.PrefetchScalarGridSpec(
            num_scalar_prefetch=1, grid=(R // tr,),
            in_specs=[pl.BlockSpec((None, tr, C), lambda i, c: (c[0], i, 0)), rspec(0), rspec(1), rspec(2)],
            out_specs=pl.BlockSpec((tr, C), lambda i, c: (i, 0))),
        compiler_params=_cparams(("parallel",)),
    )(chip, p, recv, recv, recv)


def _reduce_scatter(g, *, name):
    p = _pair_add(g, _pair_exchange(g, name=name + "_pair"), name=name + "_pair_add")
    return _chip_add(p, _chip_exchange(p, name=name + "_chip"), name=name + "_chip_add")


def _sum8(a, *, name):
    _, R, C = a.shape
    tr = _tile(R, tuple(t for t in (512, 256, 128, 64, 32, 16, 8) if t * C * 4 * 8 <= (4 << 20)))

    def body(a_ref, o_ref):
        acc = a_ref[0]
        for d in range(1, N_DEV):
            acc = acc + a_ref[d]
        o_ref[...] = acc

    return pl.pallas_call(
        body, name=name, grid=(R // tr,), in_specs=[pl.BlockSpec((N_DEV, tr, C), lambda i: (0, i, 0))],
        out_specs=pl.BlockSpec((tr, C), lambda i: (i, 0)), out_shape=jax.ShapeDtypeStruct((R, C), F32),
        compiler_params=_cparams(("parallel",)),
    )(a)


def _s5_params(lam_re, lam_im, log_dt, b_re, b_im, c_re, c_im):
    lr = jnp.minimum(lam_re, C_MAX_RE)
    li = lam_im
    dt = jnp.exp(log_dt)[:, None]
    mag = jnp.exp(lr * dt)
    ar, ai = mag * jnp.cos(li * dt), mag * jnp.sin(li * dt)
    nr, ni, den = ar - 1.0, ai, lr * lr + li * li
    fr, fi = (nr * lr + ni * li) / den, (ni * lr - nr * li) / den
    bbr = fr[..., None] * b_re - fi[..., None] * b_im
    bbi = fr[..., None] * b_im + fi[..., None] * b_re
    eye = jnp.eye(S5_GB, dtype=F32)

    def dense_b(bb):
        t = bb.reshape(S5_GB, S5_GB, C_STATE, C_GROUP).transpose(0, 1, 3, 2)
        return (t[:, :, :, None, :] * eye[None, :, None, :, None]).reshape(S5_GB, S5_GB * C_GROUP, S5_GBW)

    def dense_c(cc):
        t = cc.reshape(S5_GB, S5_GB, C_GROUP, C_STATE).transpose(0, 1, 3, 2)
        return (t[:, :, :, None, :] * eye[None, :, None, :, None]).reshape(S5_GB, S5_GBW, S5_GB * C_GROUP)

    flat = lambda a: a.reshape(1, S5_LANES)
    return flat(ar), flat(ai), dense_b(bbr), dense_b(bbi), dense_c(c_re), dense_c(c_im)


def _rope_tables(n_ctx, n_lat):
    quarter = HEAD_DK // 4
    freqs = ROPE_BASE ** (-jnp.arange(quarter, dtype=F32) / quarter)
    t = jnp.arange(n_lat)
    r = (t // GRID_W).astype(F32)
    col = (t % GRID_W).astype(F32)
    ang = jnp.concatenate([r[:, None] * freqs, col[:, None] * freqs], axis=-1)
    ang = jnp.concatenate([jnp.zeros((n_ctx, HEAD_DK // 2), F32), ang], axis=0)
    cos, sin = jnp.cos(ang), jnp.sin(ang)
    idx = jnp.arange(HEAD_DK)
    swap = (idx[:, None] == (idx[None, :] + HEAD_DK // 2) % HEAD_DK).astype(F32)
    return jnp.concatenate([cos, cos], axis=1), jnp.concatenate([-sin, sin], axis=1), swap


W_IN = 13344
W_INP = 13440
_COL = dict(a_q=0, a_ff=1024, a_fb=2048, a_i=3072, a_g=4096, b_q=5120, b_k=5632, b_v=6144, b_g=7168,
            c_u=8192, c_g=9216, d_q=10240, d_k=10752, d_v=11264, d_g=12288, b_lr=13312)


_COL_ORDER = (("a_q", 1024), ("a_ff", 1024), ("a_fb", 1024), ("a_i", 1024), ("a_g", 1024), ("b_q", 512), ("b_k", 512),
              ("b_v", 1024), ("b_g", 1024), ("c_u", 1024), ("c_g", 1024), ("d_q", 512), ("d_k", 512), ("d_v", 1024),
              ("d_g", 1024), ("b_lr", 128))
_W_IN_MAP = ((0, 7168, 0), (7200, W_IN, 7168), (7168, 7200, 13312))
W_IN_SHARD = W_IN // N_DEV


def _w_in_from_shards(g):
    parts = []
    for lo, hi, _ in _W_IN_MAP:
        for d in range(N_DEV):
            a, b = max(lo, d * W_IN_SHARD), min(hi, (d + 1) * W_IN_SHARD)
            if a < b:
                parts.append(g[d, ..., a - d * W_IN_SHARD:b - d * W_IN_SHARD])
    parts.append(jnp.zeros(g.shape[1:-1] + (W_INP - W_IN,), g.dtype))
    return jnp.concatenate(parts, axis=-1)


def _w_in_to_shards(w):
    shards = []
    for d in range(N_DEV):
        parts = []
        for lo, hi, here in sorted(_W_IN_MAP):
            a, b = max(lo, d * W_IN_SHARD), min(hi, (d + 1) * W_IN_SHARD)
            if a < b:
                parts.append(w[..., here + a - lo:here + b - lo])
        shards.append(jnp.concatenate(parts, axis=-1))
    return jnp.stack(shards)


@jax.custom_vjp
def _split_cols(proj):
    return tuple(lax.slice_in_dim(proj, _COL[n], _COL[n] + w, axis=1) for n, w in _COL_ORDER)


def _split_cols_fwd(proj):
    return _split_cols(proj), None


def _split_cols_bwd(_, cts):
    return (jnp.concatenate(cts, axis=1),)


_split_cols.defvjp(_split_cols_fwd, _split_cols_bwd)


def _layer_ops(l, n_ctx):
    n = f"l{l}_"
    rw = functools.partial(_rowwise, n_ctx=n_ctx)
    wide = [(BRANCH_W, F32)]
    return dict(
        modnorm=rw(_modnorm_fn, ["row", "const", "seg", "seg"], [(4096, BF16)], ncb=1, tr=128, name=n + "modnorm"),
        resid=rw(_resid_fn, ["row", "row", "seg"], [(4096, F32)], ncb=1, tr=128, name=n + "resid"),
        lin_in=_linear(n + "lin_in"), lin_in_gather=_linear_gather(n + "lin_in"),
        lin_out=_linear(n + "lin_out"), lin_glu=_linear(n + "lin_glu"),
        hgrn_pro=rw(_hgrn_pro_fn, ["row", "row", "const", "const"], wide * 4, ncb=1, tr=256, name=n + "hgrn_pro"),
        hgrn_f=_gla(n + "hgrn_f", flip=False, n_ctx=n_ctx), hgrn_b=_gla(n + "hgrn_b", flip=True, n_ctx=n_ctx),
        hgrn_epi=rw(_hgrn_epi_fn, ["row", "row", "row", "const"], [(BRANCH_W, BF16)], ncb=8, tr=256, name=n + "hgrn_epi"),
        gla_pro=rw(_gla_pro_fn, ["row", "row", "const", "const", "const", "const"], [(512, F32)] * 3, ncb=1, tr=256,
                   name=n + "gla_pro"),
        gla_f=_gla(n + "gla_f", flip=False, n_ctx=n_ctx), gla_b=_gla(n + "gla_b", flip=True, n_ctx=n_ctx),
        gla_epi=rw(_hgrn_epi_fn, ["row", "row", "row", "const"], [(BRANCH_W, BF16)], ncb=4, tr=256, name=n + "gla_epi"),
        s5_f=_s5(n + "s5_f", flip=False, n_ctx=n_ctx), s5_b=_s5(n + "s5_b", flip=True, n_ctx=n_ctx),
        s5_mid=rw(_s5_mid_fn, ["row", "row", "row", "const"], wide, ncb=1, tr=256, name=n + "s5_mid"),
        s5_epi=rw(_s5_epi_fn, ["row", "row", "row", "const"], [(BRANCH_W, BF16)], ncb=1, tr=256, name=n + "s5_epi"),
        ret_pro=rw(_ret_pro_fn, ["row", "row", "rowc", "rowc", "col", "col", "fixed"], [(512, F32)] * 4, ncb=4, tr=256,
                   name=n + "ret_pro"),
        ret_f=_gla(n + "ret_f", flip=False, n_ctx=n_ctx), ret_b=_gla(n + "ret_b", flip=True, n_ctx=n_ctx),
        ret_epi=rw(_ret_epi_fn, ["row", "row", "row", "const"], [(BRANCH_W, BF16)], ncb=4, tr=256, name=n + "ret_epi"),
    )


def _layer(ops, h, p, w_in, w_out, w_glu, lb, rope, sides=()):
    cos2, sin2, swap = rope
    (hn,) = ops["modnorm"](h, p["norm_g"], p["sc"], p["sh"])
    if not sides:
        proj, got = ops["lin_in"](hn, w_in, p["shadow_in"]), ()
    else:
        proj, *got = ops["lin_in_gather"](hn, w_in, p["shadow_in"], tuple(sides))
        got = [lax.stop_gradient(g) for g in got]
    w_out = w_out(got) if callable(w_out) else w_out
    w_glu = w_glu(got) if callable(w_glu) else w_glu
    cols = dict(zip([n for n, _ in _COL_ORDER], _split_cols(proj)))
    col = lambda name, w: cols[name]

    a_q, a_i = col("a_q", 1024), col("a_i", 1024)
    kf, gf, kb, gb = ops["hgrn_pro"](col("a_ff", 1024), col("a_fb", 1024), lb[0:1], lb[1:2])
    (o_a,) = ops["hgrn_epi"](ops["hgrn_f"](a_q, kf, a_i, gf), ops["hgrn_b"](a_q, kb, a_i, gb), col("a_g", 1024),
                             p["hgrn_norm_g"])

    rank = B_GATE_RANK
    wf = jnp.pad(p["gla_w_gk"][0], ((0, 128 - rank), (0, 0)))
    wb = jnp.pad(p["gla_w_gk"][1], ((rank, 128 - 2 * rank), (0, 0)))
    b_k, b_v = col("b_k", 512), col("b_v", 1024)
    qs, gf, gb = ops["gla_pro"](col("b_q", 512), col("b_lr", 128), wf, wb, p["gla_b_gk"][0:1], p["gla_b_gk"][1:2])
    (o_b,) = ops["gla_epi"](ops["gla_f"](qs, b_k, b_v, gf), ops["gla_b"](qs, b_k, b_v, gb), col("b_g", 1024),
                            p["gla_norm_g"])

    c_u = col("c_u", 1024)
    s5p = lambda d: _s5_params(*[p[k][d] for k in ("s5_lam_re", "s5_lam_im", "s5_log_dt", "s5_b_re", "s5_b_im",
                                                  "s5_c_re", "s5_c_im")])
    (z,) = ops["s5_mid"](ops["s5_f"](c_u, *s5p(0)), ops["s5_b"](c_u, *s5p(1)), c_u, p["s5_d"].reshape(1, BRANCH_W))
    t = ops["lin_glu"](z, w_glu, p["shadow_glu"])
    (o_c,) = ops["s5_epi"](z, t, col("c_g", 1024), p["s5_b_glu"])

    log_gamma = _log_sigmoid(p["ret_decay_logit"])
    lg = lambda d: jnp.broadcast_to(log_gamma[d][:, None, None], (4, 1, HEAD_DK))
    d_v = col("d_v", 1024)
    qs, ks, gf, gb = ops["ret_pro"](col("d_q", 512), col("d_k", 512), cos2, sin2, lg(0), lg(1), swap)
    (o_d,) = ops["ret_epi"](ops["ret_f"](qs, ks, d_v, gf), ops["ret_b"](qs, ks, d_v, gb), col("d_g", 1024),
                            p["ret_norm_g"])

    o = jnp.concatenate([o_a, o_b, o_c, o_d], axis=1)
    y = ops["lin_out"](o, w_out, p["shadow_out"])
    (h,) = ops["resid"](h, y, p["gt"])
    return h, got


_SMALL = ("norm_g", "hgrn_lb_logits", "hgrn_norm_g", "gla_w_gk", "gla_b_gk", "gla_norm_g", "s5_lam_re", "s5_lam_im",
          "s5_log_dt", "s5_b_re", "s5_b_im", "s5_c_re", "s5_c_im", "s5_d", "s5_b_glu", "ret_decay_logit", "ret_norm_g",
          "final_norm_g")
_WEIGHTS = ("c_ctx", "norm_g", "w_ada", "b_ada", "w_in", "hgrn_lb_logits", "hgrn_norm_g", "gla_w_gk", "gla_b_gk",
            "gla_norm_g", "s5_lam_re", "s5_lam_im", "s5_log_dt", "s5_b_re", "s5_b_im", "s5_c_re", "s5_c_im", "s5_d",
            "s5_w_glu", "s5_b_glu", "ret_decay_logit", "ret_norm_g", "w_out", "final_norm_g")


def _pad_rows(flat, mult=8):
    rows = -(-flat.size // 128)
    rows = -(-rows // mult) * mult
    return jnp.pad(flat, (0, rows * 128 - flat.size)).reshape(rows, 128)


def _unshard_cols(g, inner):
    k = len(inner)
    a = g.reshape((N_DEV,) + tuple(inner))
    return jnp.moveaxis(a, 0, k - 1).reshape(tuple(inner[:-1]) + (N_DEV * inner[-1],))


def kernel(x, c, ctx, c_ctx, norm_g, w_ada, b_ada, w_in, hgrn_lb_logits, hgrn_norm_g, gla_w_gk, gla_b_gk, gla_norm_g, s5_lam_re, s5_lam_im, s5_log_dt, s5_b_re, s5_b_im, s5_c_re, s5_c_im, s5_d, s5_w_glu, s5_b_glu, ret_decay_logit, ret_norm_g, w_out, final_norm_g, loss_target, m_c_ctx, m_norm_g, m_w_ada, m_b_ada, m_w_in, m_hgrn_lb_logits, m_hgrn_norm_g, m_gla_w_gk, m_gla_b_gk, m_gla_norm_g, m_s5_lam_re, m_s5_lam_im, m_s5_log_dt, m_s5_b_re, m_s5_b_im, m_s5_c_re, m_s5_c_im, m_s5_d, m_s5_w_glu, m_s5_b_glu, m_ret_decay_logit, m_ret_norm_g, m_w_out, m_final_norm_g, v_c_ctx, v_norm_g, v_w_ada, v_b_ada, v_w_in, v_hgrn_lb_logits, v_hgrn_norm_g, v_gla_w_gk, v_gla_b_gk, v_gla_norm_g, v_s5_lam_re, v_s5_lam_im, v_s5_log_dt, v_s5_b_re, v_s5_b_im, v_s5_c_re, v_s5_c_im, v_s5_d, v_s5_w_glu, v_s5_b_glu, v_ret_decay_logit, v_ret_norm_g, v_w_out, v_final_norm_g):
    given = dict(locals())
    depth = norm_g.shape[0]
    n_ctx, n_lat, d_model = ctx.shape[1], x.shape[1], x.shape[2]
    me = 4 * lax.axis_index("x") + 2 * lax.axis_index("y") + lax.axis_index("c")
    w_ada_cols = w_ada.shape[2]

    pack = jnp.concatenate([c.reshape(-1), hgrn_lb_logits.reshape(-1), gla_w_gk.reshape(-1), gla_b_gk.reshape(-1)])
    got = _all_gather(_pad_rows(pack), name="gather_small").reshape(N_DEV, -1)
    o0 = d_model
    o1 = o0 + hgrn_lb_logits.size
    o2 = o1 + gla_w_gk.size
    o3 = o2 + gla_b_gk.size
    c_all = got[:, :o0]
    lb_logits = _unshard_cols(got[:, o0:o1], hgrn_lb_logits.shape)
    w_gk = _unshard_cols(got[:, o1:o2], gla_w_gk.shape)
    b_gk = _unshard_cols(got[:, o2:o3], gla_b_gk.shape)

    cond = jnp.concatenate([c_all, c_ctx[None, :], jnp.zeros((7, d_model), F32)], axis=0)
    act = _silu(cond).astype(BF16)
    mod_part = jnp.concatenate([_mm(act, w_ada[l], name=f"ada{l}_fwd") for l in range(depth)], axis=0)
    mod = _all_gather(mod_part, name="gather_mod").reshape(N_DEV, depth, 16, w_ada_cols)
    mod = jnp.moveaxis(mod, 0, 2).reshape(depth, 16, N_DEV * w_ada_cols) + b_ada[:, None, :]
    mod_lat = lax.dynamic_index_in_dim(mod, me, axis=1, keepdims=False)
    mod_ctx = mod[:, 8]
    seg = lambda l, j: jnp.stack([mod_ctx[l, j * d_model:(j + 1) * d_model],
                                  mod_lat[l, j * d_model:(j + 1) * d_model]])[:, None, :]

    w_in_shards = [w_in[l].astype(BF16) for l in range(depth)]
    w_out_shards = [w_out[l].astype(BF16) for l in range(depth)]
    w_glu_shard = s5_w_glu.astype(BF16).reshape(-1, BRANCH_W)
    w_in_first = _w_in_from_shards(_all_gather(w_in_shards[0], name="gather_w_in"))

    rope = _rope_tables(n_ctx, n_lat)
    ops = [_layer_ops(l, n_ctx) for l in range(depth)]
    params = dict(
        layers=[dict(
            norm_g=norm_g[l][None, :], sh=seg(l, 0), sc=seg(l, 1), gt=seg(l, 2),
            shadow_in=jnp.zeros((d_model, W_INP), F32), shadow_out=jnp.zeros((4 * BRANCH_W, d_model), F32),
            shadow_glu=jnp.zeros((BRANCH_W, BRANCH_W), F32),
            hgrn_norm_g=hgrn_norm_g[l][None, :], gla_w_gk=w_gk[l], gla_b_gk=b_gk[l], gla_norm_g=gla_norm_g[l][None, :],
            s5_lam_re=s5_lam_re[l], s5_lam_im=s5_lam_im[l], s5_log_dt=s5_log_dt[l], s5_b_re=s5_b_re[l],
            s5_b_im=s5_b_im[l], s5_c_re=s5_c_re[l], s5_c_im=s5_c_im[l], s5_d=s5_d[l], s5_b_glu=s5_b_glu[l][None, :],
            ret_decay_logit=ret_decay_logit[l], ret_norm_g=ret_norm_g[l][None, :]) for l in range(depth)],
        lb_logits=lb_logits,
        h0=jnp.concatenate([ctx[0], x[0]], axis=0))

    def body(p):
        lb_p = jax.nn.softmax(p["lb_logits"], axis=0)
        lower = jnp.cumsum(lb_p, axis=0) - lb_p[0:1]
        h = p["h0"]
        w_in_l, w_glu_all = w_in_first, None
        glu_of = lambda g: jnp.moveaxis(g.reshape(N_DEV, depth, -1, BRANCH_W), 0, 1).reshape(depth, BRANCH_W, BRANCH_W)
        for l in range(depth):
            more = l + 1 < depth
            sides = ([w_in_shards[l + 1]] if more else []) + [w_out_shards[l]] + ([w_glu_shard] if l == 0 else [])
            k_out = 1 if more else 0
            pick_out = lambda got, k=k_out: got[k].reshape(4 * BRANCH_W, d_model)
            pick_glu = (lambda got, k=k_out + 1: glu_of(got[k])[0]) if l == 0 else w_glu_all[l]
            h, got = _layer(ops[l], h, p["layers"][l], w_in_l, pick_out, pick_glu, lower[l], rope, sides=sides)
            if l == 0:
                w_glu_all = glu_of(got[k_out + 1])
            if more:
                w_in_l = _w_in_from_shards(got[0])
        return h

    h_out, pullback = jax.vjp(body, params)
    dh, dg_final, loss_part = _loss_call(h_out, loss_target[0], final_norm_g[None, :], n_ctx=n_ctx, name="loss_head")
    (grads,) = pullback(dh)
    loss = lax.psum(loss_part[0, 0], ("x", "y", "c"))
    gl = grads["layers"]

    dmod = jnp.stack([jnp.concatenate([gl[l][k][s, 0] for k in ("sh", "sc", "gt")])
                      for l in range(depth) for s in (1, 0)])
    dmod = jnp.pad(dmod, ((0, 8 - 2 * depth), (0, 0)))
    dmod_all = _all_gather(dmod, name="gather_dmod")
    dmod_sum = _sum8(dmod_all, name="sum_dmod")
    grad_b_ada = jnp.stack([dmod_sum[2 * l] + dmod_sum[2 * l + 1] for l in range(depth)])
    grad_w_ada, ds_ctx = [], jnp.zeros((d_model,), F32)
    for l in range(depth):
        rows = jnp.concatenate([dmod_all[:, 2 * l], dmod_sum[2 * l + 1][None, :], jnp.zeros((7, 3 * d_model), F32)])
        mine = lax.dynamic_slice_in_dim(rows, me * w_ada_cols, w_ada_cols, axis=1)
        grad_w_ada.append(_mm(act.T, mine, name=f"ada{l}_dw"))
        ds_ctx = ds_ctx + _mm(mine, w_ada[l], nt=True, name=f"ada{l}_da")[8]
    grad_w_ada = jnp.stack(grad_w_ada)

    def scattered(g, axis, name):
        parts = g.reshape(g.shape[:axis] + (N_DEV, g.shape[axis] // N_DEV) + g.shape[axis + 1:])
        parts = jnp.moveaxis(parts, axis, 0)
        local = parts.shape[1:]
        return _reduce_scatter(parts.reshape(N_DEV, -1, local[-1]), name=name).reshape(local)

    w_in_slabs = _w_in_to_shards(jnp.stack([gl[l]["shadow_in"] for l in range(depth)]))
    grad_w_in = _reduce_scatter(w_in_slabs.reshape(N_DEV, depth * d_model, W_IN_SHARD), name="rs_w_in").reshape(w_in.shape)
    grad_w_out = scattered(jnp.stack([gl[l]["shadow_out"] for l in range(depth)]), 1, "rs_w_out")
    grad_w_glu = scattered(jnp.stack([gl[l]["shadow_glu"] for l in range(depth)]), 1, "rs_w_glu")

    stack = lambda k, shape=None: jnp.stack([gl[l][k] for l in range(depth)]).reshape(shape or given[k].shape)
    small = dict(norm_g=stack("norm_g"), hgrn_lb_logits=grads["lb_logits"], hgrn_norm_g=stack("hgrn_norm_g"),
                 gla_w_gk=stack("gla_w_gk", w_gk.shape), gla_b_gk=stack("gla_b_gk", b_gk.shape),
                 gla_norm_g=stack("gla_norm_g"), s5_d=stack("s5_d"), s5_b_glu=stack("s5_b_glu"),
                 ret_decay_logit=stack("ret_decay_logit"), ret_norm_g=stack("ret_norm_g"), final_norm_g=dg_final[0])
    for k in ("s5_lam_re", "s5_lam_im", "s5_log_dt", "s5_b_re", "s5_b_im", "s5_c_re", "s5_c_im"):
        small[k] = stack(k)
    pieces = [small[k] for k in _SMALL] + [ds_ctx]
    packed = jnp.concatenate([_pad_rows(a.reshape(-1), mult=1) for a in pieces], axis=0)
    packed = jnp.pad(packed, ((0, -packed.shape[0] % 8), (0, 0)))
    total = _sum8(_all_gather(packed, name="gather_small_grads"), name="sum_small_grads")
    reduced, off = {}, 0
    for k, a in zip(_SMALL + ("ds_ctx",), pieces):
        rows = -(-a.size // 128)
        reduced[k] = total[off:off + rows].reshape(-1)[:a.size].reshape(a.shape)
        off += rows

    def my_cols(a):
        w = a.shape[-1] // N_DEV
        return lax.dynamic_slice_in_dim(a, me * w, w, axis=a.ndim - 1)

    dsilu = jax.vjp(_silu, c_ctx)[1](reduced.pop("ds_ctx"))[0]
    grad = dict(reduced, c_ctx=dsilu, w_ada=grad_w_ada, b_ada=grad_b_ada, w_in=grad_w_in, w_out=grad_w_out,
                s5_w_glu=grad_w_glu)
    for k in ("hgrn_lb_logits", "gla_w_gk", "gla_b_gk"):
        grad[k] = my_cols(grad[k])

    delta, new_m, new_v = {}, {}, {}
    for k in _WEIGHTS:
        delta[k], new_m[k], new_v[k] = _adamw(given[k], grad[k], given["m_" + k], given["v_" + k], name="adamw_" + k)
    grad_x = grads["h0"][n_ctx:][None]
    return (loss, grad_x, *[grad[k] for k in _WEIGHTS], *[delta[k] for k in _WEIGHTS],
            *[new_m[k] for k in _WEIGHTS], *[new_v[k] for k in _WEIGHTS])
```

```python
import functools
import math

import jax
import jax.numpy as jnp
from jax import lax
from jax.experimental import pallas as pl
from jax.experimental.pallas import tpu as pltpu

F32 = jnp.float32
BF16 = jnp.bfloat16

EPS = 1e-6
BRANCH_W = 1024
CHUNK = 64
HEAD_DK = 128
A_MIN_FORGET = 1e-6
B_GATE_NORM = 16.0
B_GATE_RANK = 16
C_GROUP = 16
C_GROUPS = 64
C_STATE = 64
C_MAX_RE = -1e-4
GRID_W = 64
ROPE_BASE = 10000.0
N_DEV = 8

ADAM_LR = 0.001
ADAM_B1 = 0.9
ADAM_B2 = 0.999
ADAM_EPS = 1e-08
ADAM_WD = 0.01
ADAM_STEP = 10

VMEM_LIMIT = 52 * 1024 * 1024
SCAN_ROWS = 256
S5_ROWS = 128
S5_GB = 8
S5_GBW = S5_GB * C_STATE


def _tile(dim, cands):
    for c in cands:
        if dim % c == 0:
            return c
    return dim


def _cparams(sem):
    return pltpu.CompilerParams(dimension_semantics=sem, vmem_limit_bytes=VMEM_LIMIT)


MM_VMEM_BUDGET = 40 * 1024 * 1024


def _mm_tiles(M, N, K, sa, sb, so):
    tms = [t for t in (1088, 1024, 544, 512) if M % t == 0] or [M]
    tns = [t for t in (1024, 896, 640, 512, 384, 256) if N % t == 0] or [N]
    tks = [t for t in (K, 4480, 4352, 4096, 2688, 2176, 2048, 1920, 1024, 896, 640, 512, 384, 256, 128) if K % t == 0]
    fits = lambda tm, tn, tk: (2 * (tm * tk * sa + tk * tn * sb) + (tm * tn * 4 if tk < K else 0)
                               + 2 * tm * tn * so <= MM_VMEM_BUDGET)
    for tm in tms:
        for tn in tns:
            if fits(tm, tn, K):
                return tm, tn, K
    for tm in tms:
        for tn in tns:
            for tk in tks:
                if fits(tm, tn, tk):
                    return tm, tn, tk
    return tms[-1], tns[-1], tks[-1]


def _mm(a, b, *, nt=False, out_dtype=F32, name):
    M, K = a.shape
    N = b.shape[0] if nt else b.shape[1]
    tm, tn, tk = _mm_tiles(M, N, K, a.dtype.itemsize, b.dtype.itemsize, jnp.dtype(out_dtype).itemsize)
    nk = K // tk
    dims = (((1,), (1,)), ((), ())) if nt else (((1,), (0,)), ((), ()))

    if nk == 1:
        def body1(a_ref, b_ref, o_ref):
            o_ref[...] = lax.dot_general(a_ref[...].astype(BF16), b_ref[...].astype(BF16), dims,
                                         preferred_element_type=F32).astype(o_ref.dtype)

        b_spec1 = (pl.BlockSpec((tn, K), lambda i, j: (j, 0)) if nt else pl.BlockSpec((K, tn), lambda i, j: (0, j)))
        return pl.pallas_call(
            body1, name=name, grid=(M // tm, N // tn),
            in_specs=[pl.BlockSpec((tm, K), lambda i, j: (i, 0)), b_spec1],
            out_specs=pl.BlockSpec((tm, tn), lambda i, j: (i, j)),
            out_shape=jax.ShapeDtypeStruct((M, N), out_dtype),
            compiler_params=_cparams(("parallel", "arbitrary")),
        )(a, b)

    def body(a_ref, b_ref, o_ref, acc_ref):
        k = pl.program_id(2)

        @pl.when(k == 0)
        def _():
            acc_ref[...] = jnp.zeros_like(acc_ref)

        acc_ref[...] += lax.dot_general(a_ref[...].astype(BF16), b_ref[...].astype(BF16), dims,
                                        preferred_element_type=F32)

        @pl.when(k == nk - 1)
        def _():
            o_ref[...] = acc_ref[...].astype(o_ref.dtype)

    b_spec = (pl.BlockSpec((tn, tk), lambda i, j, k: (j, k)) if nt
              else pl.BlockSpec((tk, tn), lambda i, j, k: (k, j)))
    return pl.pallas_call(
        body, name=name, grid=(M // tm, N // tn, nk),
        in_specs=[pl.BlockSpec((tm, tk), lambda i, j, k: (i, k)), b_spec],
        out_specs=pl.BlockSpec((tm, tn), lambda i, j, k: (i, j)),
        out_shape=jax.ShapeDtypeStruct((M, N), out_dtype),
        scratch_shapes=[pltpu.VMEM((tm, tn), F32)],
        compiler_params=_cparams(("parallel", "parallel", "arbitrary")),
    )(a, b)


def _linear(name):
    @jax.custom_vjp
    def op(a, w, shadow):
        return _mm(a, w, name=name + "_fwd")

    def fwd(a, w, shadow):
        return _mm(a, w, name=name + "_fwd"), (a, w)

    def bwd(res, dy):
        a, w = res
        dy = dy.astype(BF16)
        da = _mm(dy, w, nt=True, out_dtype=a.dtype, name=name + "_da")
        dw = _mm(a.T, dy, name=name + "_dw")
        return da, jnp.zeros_like(w), dw

    op.defvjp(fwd, bwd)
    return op


def _mm_gather(a, b, sides, *, name):
    M, K = a.shape
    N = b.shape[1]
    tm, tn, tk = _mm_tiles(M, N, K, a.dtype.itemsize, b.dtype.itemsize, 4)
    assert tk == K
    gi, gj = M // tm, N // tn
    last = gi * gj - 1
    mid = last // 2
    ns = len(sides)

    def body(a_ref, b_ref, *refs):
        side_refs, o_ref, got_refs = refs[:ns], refs[ns], refs[ns + 1:2 * ns + 1]
        send_sems, recv_sems, local_sems = refs[2 * ns + 1:]
        step = pl.program_id(0) * gj + pl.program_id(1)
        x_, y_, c_ = _place()
        me, sibling = (x_, y_, c_), (x_, y_, 1 - c_)
        chips = [(1 - x_, y_), (x_, 1 - y_), (1 - x_, 1 - y_)]

        def gather(s):
            side_ref, got_ref = side_refs[s], got_refs[s]

            def slab(px, py, pc):
                return got_ref.at[4 * px + 2 * py + pc]

            def copy(k, block, to, src=None):
                return pltpu.make_async_remote_copy(
                    src_ref=slab(*block) if src is None else src, dst_ref=slab(*block), send_sem=send_sems.at[7 * s + k],
                    recv_sem=recv_sems.at[7 * s + k], device_id=to, device_id_type=MESH)

            mine = pltpu.make_async_copy(side_ref, slab(*me), local_sems.at[s])
            first = [copy(0, me, sibling, src=side_ref)] + [copy(1 + j, me, (*chip, c_), src=side_ref)
                                                             for j, chip in enumerate(chips)]
            passed = [copy(4 + j, (*chip, c_), sibling) for j, chip in enumerate(chips)]
            return copy, mine, first, passed

        parts = [gather(s) for s in range(ns)]

        @pl.when(step == 0)
        def _():
            for copy, mine, first, passed in parts:
                mine.start()
                for cp in first:
                    cp.start()

        o_ref[...] = jnp.dot(a_ref[...].astype(BF16), b_ref[...].astype(BF16), preferred_element_type=F32)

        @pl.when(step == mid)
        def _():
            for copy, mine, first, passed in parts:
                for j, chip in enumerate(chips):
                    copy(1 + j, (*chip, c_), me).wait_recv()
                    passed[j].start()

        @pl.when(step == last)
        def _():
            for copy, mine, first, passed in parts:
                copy(0, sibling, me).wait_recv()
                for j, chip in enumerate(chips):
                    copy(4 + j, (*chip, 1 - c_), me).wait_recv()
                for cp in first + passed:
                    cp.wait_send()
                mine.wait()

    return pl.pallas_call(
        body, name=name, grid=(gi, gj),
        in_specs=[pl.BlockSpec((tm, K), lambda i, j: (i, 0)), pl.BlockSpec((K, tn), lambda i, j: (0, j))] + [ANY] * ns,
        out_specs=[pl.BlockSpec((tm, tn), lambda i, j: (i, j))] + [ANY] * ns,
        out_shape=[jax.ShapeDtypeStruct((M, N), F32)]
        + [jax.ShapeDtypeStruct((N_DEV,) + s.shape, s.dtype) for s in sides],
        scratch_shapes=[pltpu.SemaphoreType.DMA((7 * ns,)), pltpu.SemaphoreType.DMA((7 * ns,)),
                        pltpu.SemaphoreType.DMA((ns,))],
        compiler_params=_cparams(("arbitrary", "arbitrary")),
    )(a, b, *sides)


def _linear_gather(name):
    @jax.custom_vjp
    def op(a, w, shadow, sides):
        return tuple(_mm_gather(a, w, sides, name=name + "_fwd"))

    def fwd(a, w, shadow, sides):
        return tuple(_mm_gather(a, w, sides, name=name + "_fwd")), (a, w, sides)

    def bwd(res, cts):
        a, w, sides = res
        dy = cts[0].astype(BF16)
        da = _mm(dy, w, nt=True, out_dtype=a.dtype, name=name + "_da")
        dw = _mm(a.T, dy, name=name + "_dw")
        return da, jnp.zeros_like(w), dw, tuple(jnp.zeros_like(s) for s in sides)

    op.defvjp(fwd, bwd)
    return op


def _rowwise(fn, specs, out_defs, *, ncb, tr, n_ctx, name):
    ncx = n_ctx // tr
    diff = [k for k, s in enumerate(specs) if s in ("row", "const", "col", "seg")]

    def in_spec(s, arr):
        if s == "row":
            return pl.BlockSpec((tr, arr.shape[1] // ncb), lambda j, i: (i, j))
        if s == "rowc":
            return pl.BlockSpec((tr, arr.shape[1]), lambda j, i: (i, 0))
        if s in ("const", "fixed"):
            return pl.BlockSpec(arr.shape, lambda j, i: (0, 0))
        if s == "col":
            return pl.BlockSpec((None,) + arr.shape[1:], lambda j, i: (j, 0, 0))
        if s == "seg":
            return pl.BlockSpec((None,) + arr.shape[1:], lambda j, i: ((i >= ncx).astype(jnp.int32), 0, 0))
        raise ValueError(s)

    def fwd_call(*arrays):
        L = arrays[specs.index("row")].shape[0]
        n_in = len(arrays)

        def body(*refs):
            outs = fn(*[r[...] for r in refs[:n_in]])
            for o_ref, o in zip(refs[n_in:], outs):
                o_ref[...] = o.astype(o_ref.dtype)

        return pl.pallas_call(
            body, name=name + "_fwd", grid=(ncb, L // tr),
            in_specs=[in_spec(s, a) for s, a in zip(specs, arrays)],
            out_specs=[pl.BlockSpec((tr, c // ncb), lambda j, i: (i, j)) for c, _ in out_defs],
            out_shape=[jax.ShapeDtypeStruct((L, c), dt) for c, dt in out_defs],
            compiler_params=_cparams(("parallel", "arbitrary")),
        )(*arrays)

    def bwd_call(arrays, cts):
        L = arrays[specs.index("row")].shape[0]
        n_in, n_ct = len(arrays), len(cts)

        def body(*refs):
            i = pl.program_id(1)
            vals = [r[...] for r in refs[:n_in]]
            ct_vals = tuple(r[...] for r in refs[n_in:n_in + n_ct])
            g_refs = refs[n_in + n_ct:]

            def f(*dv):
                full = list(vals)
                for k, v in zip(diff, dv):
                    full[k] = v
                return tuple(o.astype(dt) for o, (_, dt) in zip(fn(*full), out_defs))

            _, vjp = jax.vjp(f, *[vals[k] for k in diff])
            grads = vjp(ct_vals)
            for k, g_ref, g in zip(diff, g_refs, grads):
                if specs[k] == "row":
                    g_ref[...] = g.astype(g_ref.dtype)
                else:
                    first = (i == 0) | (i == ncx) if specs[k] == "seg" else (i == 0)

                    @pl.when(first)
                    def _(g_ref=g_ref, g=g):
                        g_ref[...] = g

                    @pl.when(jnp.logical_not(first))
                    def _(g_ref=g_ref, g=g):
                        g_ref[...] += g

        out_specs, out_shape = [], []
        for k in diff:
            s, a = specs[k], arrays[k]
            if s == "row":
                out_specs.append(pl.BlockSpec((tr, a.shape[1] // ncb), lambda j, i: (i, j)))
                out_shape.append(jax.ShapeDtypeStruct(a.shape, a.dtype))
            elif s == "const":
                out_specs.append(pl.BlockSpec((None,) + a.shape, lambda j, i: (j, 0, 0)))
                out_shape.append(jax.ShapeDtypeStruct((ncb,) + a.shape, F32))
            elif s == "col":
                out_specs.append(pl.BlockSpec((None,) + a.shape[1:], lambda j, i: (j, 0, 0)))
                out_shape.append(jax.ShapeDtypeStruct(a.shape, F32))
            else:
                out_specs.append(pl.BlockSpec((None, None) + a.shape[1:],
                                              lambda j, i: (j, (i >= ncx).astype(jnp.int32), 0, 0)))
                out_shape.append(jax.ShapeDtypeStruct((ncb,) + a.shape, F32))
        ct_specs = [pl.BlockSpec((tr, c // ncb), lambda j, i: (i, j)) for c, _ in out_defs]
        grads = pl.pallas_call(
            body, name=name + "_bwd", grid=(ncb, L // tr),
            in_specs=[in_spec(s, a) for s, a in zip(specs, arrays)] + ct_specs,
            out_specs=out_specs, out_shape=out_shape,
            compiler_params=_cparams(("parallel", "arbitrary")),
        )(*arrays, *cts)
        full = [jnp.zeros_like(a) for a in arrays]
        for k, g in zip(diff, grads):
            full[k] = jnp.sum(g, axis=0) if specs[k] in ("const", "seg") else g
        return tuple(full)

    @jax.custom_vjp
    def op(*arrays):
        return tuple(fwd_call(*arrays))

    def op_fwd(*arrays):
        return tuple(fwd_call(*arrays)), arrays

    def op_bwd(arrays, cts):
        return bwd_call(arrays, cts)

    op.defvjp(op_fwd, op_bwd)
    return op


def _sigmoid(x):
    return 1.0 / (1.0 + jnp.exp(-x))


def _log_sigmoid(x):
    return jnp.minimum(x, 0.0) - jnp.log(1.0 + jnp.exp(-jnp.abs(x)))


def _silu(x):
    return x * _sigmoid(x)


def _rms(x, g):
    return x * lax.rsqrt(jnp.mean(x * x, axis=-1, keepdims=True) + EPS) * g


def _modnorm_fn(h, g, sc, sh):
    return ((_rms(h, g) * (1.0 + sc) + sh).astype(BF16),)


def _resid_fn(h, y, gt):
    return (h + gt * y,)


def _hgrn_pro_fn(zf, zb, lbf, lbb):
    def one(z, lb):
        f = lb + (1.0 - lb) * _sigmoid(z)
        return (1.0 - lb) * _sigmoid(-z), jnp.log(jnp.maximum(f, A_MIN_FORGET))

    kf, gf = one(zf, lbf)
    kb, gb = one(zb, lbb)
    return kf, gf, kb, gb


def _hgrn_epi_fn(of, ob, gate, g):
    return ((_rms(of + ob, g) * _silu(gate)).astype(BF16),)


def _gla_pro_fn(q, lr, wf, wb, bf, bb):
    gf = _log_sigmoid(jnp.dot(lr, wf, preferred_element_type=F32) + bf) / B_GATE_NORM
    gb = _log_sigmoid(jnp.dot(lr, wb, preferred_element_type=F32) + bb) / B_GATE_NORM
    return q * (HEAD_DK ** -0.5), gf, gb


def _ret_pro_fn(q, k, cos2, sin2, lgf, lgb, swap):
    def rope(x):
        return x * cos2 + jnp.dot(x, swap, preferred_element_type=F32, precision=lax.Precision.HIGHEST) * sin2

    zero = jnp.zeros_like(q)
    return rope(q) * (HEAD_DK ** -0.5), rope(k), zero + lgf, zero + lgb


def _ret_epi_fn(of, ob, gate, g):
    x = of + ob
    mu = jnp.mean(x, axis=-1, keepdims=True)
    var = jnp.mean(jnp.square(x - mu), axis=-1, keepdims=True)
    return (((x - mu) * lax.rsqrt(var + EPS) * g * _silu(gate)).astype(BF16),)


def _s5_mid_fn(yf, yb, u, d):
    return (jax.nn.gelu(yf + yb + d * u),)


def _s5_epi_fn(z, t, gate, b):
    return ((z * _sigmoid(t + b) * _silu(gate)).astype(BF16),)


def _blk(mode, i, nb, nc):
    if mode == 0:
        return i
    if mode == 1:
        return nb - 1 - i
    if mode == 2:
        return jnp.where(i < nc, nc - 1 - i, nb - 1 - i + nc)
    return jnp.where(i < nb - nc, i + nc, nc - nb + i)


def _split3(x):
    hi = x.astype(BF16)
    r1 = x - hi.astype(F32)
    mid = r1.astype(BF16)
    lo = (r1 - mid.astype(F32)).astype(BF16)
    return hi, mid, lo


def _cum(tri, x):
    w = x.shape[1]
    hi = x.astype(BF16)
    lo = (x - hi.astype(F32)).astype(BF16)
    r = jnp.dot(tri, jnp.concatenate([hi, lo], axis=1), preferred_element_type=F32)
    return r[:, :w] + r[:, w:]


def _dot(a, b):
    return jnp.dot(a.astype(BF16), b.astype(BF16), preferred_element_type=F32)


def _dot_nt(a, b):
    return lax.dot_general(a.astype(BF16), b.astype(BF16), (((1,), (1,)), ((), ())), preferred_element_type=F32)


def _dot_tn(a, b):
    return jnp.dot(a.astype(F32).T.astype(BF16), b.astype(BF16), preferred_element_type=F32)


def _dot3(a, b):
    ah = a.astype(BF16)
    al = (a - ah.astype(F32)).astype(BF16)
    bh = b.astype(BF16)
    bl = (b - bh.astype(F32)).astype(BF16)
    d = lambda p, q: jnp.dot(p, q, preferred_element_type=F32)
    return d(ah, bh) + d(ah, bl) + d(al, bh)


def _dot3_tn(a, b):
    return _dot3(a.astype(F32).T, b)


def _gla_masks(flip):
    r = lax.broadcasted_iota(jnp.int32, (CHUNK, CHUNK), 0)
    c = lax.broadcasted_iota(jnp.int32, (CHUNK, CHUNK), 1)
    causal = (c >= r) if flip else (c <= r)
    tri = causal.astype(BF16)
    tri_t = jnp.logical_not(causal) | (r == c)
    return causal, tri, tri_t.astype(BF16)


def _gla_chunk_terms(q, k, g, tri, flip):
    b = _cum(tri, g)
    mid = CHUNK // 2
    bm = b[mid:mid + 1, :]
    bl = b[0:1, :] if flip else b[CHUNK - 1:CHUNK, :]
    e_q = jnp.exp(b - bm)
    e_k = jnp.exp(bm - b)
    return b, bl, e_q, e_k, q * e_q, k * e_k


def _gla_fwd_call(q, k, v, g, *, flip, n_ctx, name):
    L = q.shape[0]
    H = q.shape[1] // HEAD_DK
    dv = v.shape[1] // H
    nb, nc, sub = L // SCAN_ROWS, n_ctx // SCAN_ROWS, SCAN_ROWS // CHUNK
    mode = 2 if flip else 0
    order = list(range(sub))[::-1] if flip else list(range(sub))

    def body(q_ref, k_ref, v_ref, g_ref, o_ref, sp_ref, st_ref):
        @pl.when(pl.program_id(1) == 0)
        def _():
            st_ref[...] = jnp.zeros_like(st_ref)

        causal, tri, _ = _gla_masks(flip)
        for s in order:
            sl = pl.ds(s * CHUNK, CHUNK)
            qc, kc, vc, gc = q_ref[sl, :], k_ref[sl, :], v_ref[sl, :], g_ref[sl, :]
            b, bl, _, _, qt, kt = _gla_chunk_terms(qc, kc, gc, tri, flip)
            a = jnp.where(causal, _dot_nt(qt, kt), 0.0)
            sp = st_ref[...]
            sp_ref[s] = sp
            o_ref[sl, :] = _dot(a, vc) + _dot_nt(qc * jnp.exp(b), sp)
            st_ref[...] = sp * jnp.exp(bl) + _dot3_tn(vc, kc * jnp.exp(bl - b))

    row = lambda w: pl.BlockSpec((SCAN_ROWS, w), lambda h, i: (_blk(mode, i, nb, nc), h))
    return pl.pallas_call(
        body, name=name, grid=(H, nb),
        in_specs=[row(HEAD_DK), row(HEAD_DK), row(dv), row(HEAD_DK)],
        out_specs=[row(dv), pl.BlockSpec((None, None, sub, dv, HEAD_DK),
                                         lambda h, i: (h, _blk(mode, i, nb, nc), 0, 0, 0))],
        out_shape=[jax.ShapeDtypeStruct((L, H * dv), F32),
                   jax.ShapeDtypeStruct((H, nb, sub, dv, HEAD_DK), F32)],
        scratch_shapes=[pltpu.VMEM((dv, HEAD_DK), F32)],
        compiler_params=_cparams(("parallel", "arbitrary")),
    )(q, k, v, g)


def _gla_bwd_call(q, k, v, g, sp, do, *, flip, n_ctx, name):
    L = q.shape[0]
    H = q.shape[1] // HEAD_DK
    dv = v.shape[1] // H
    nb, nc, sub = L // SCAN_ROWS, n_ctx // SCAN_ROWS, SCAN_ROWS // CHUNK
    mode = 3 if flip else 1
    order = list(range(sub)) if flip else list(range(sub))[::-1]
    last = 0 if flip else CHUNK - 1

    def body(q_ref, k_ref, v_ref, g_ref, sp_ref, do_ref, dq_ref, dk_ref, dv_ref, dg_ref, ds_ref):
        @pl.when(pl.program_id(1) == 0)
        def _():
            ds_ref[...] = jnp.zeros_like(ds_ref)

        causal, tri, tri_t = _gla_masks(flip)
        is_last = lax.broadcasted_iota(jnp.int32, (CHUNK, HEAD_DK), 0) == last
        for s in order:
            sl = pl.ds(s * CHUNK, CHUNK)
            qc, kc, vc, gc, doc = q_ref[sl, :], k_ref[sl, :], v_ref[sl, :], g_ref[sl, :], do_ref[sl, :]
            b, bl, e_q, e_k, qt, kt = _gla_chunk_terms(qc, kc, gc, tri, flip)
            e_b, e_l = jnp.exp(b), jnp.exp(bl - b)
            a = jnp.where(causal, _dot_nt(qt, kt), 0.0)
            p, kd = qc * e_b, kc * e_l
            spv, ds = sp_ref[s], ds_ref[...]
            dv_ref[sl, :] = _dot_tn(a, doc) + _dot_nt(kd, ds)
            da = jnp.where(causal, _dot_nt(doc, vc), 0.0)
            dkd = _dot3(vc, ds)
            dq = _dot3(da, kt) * e_q + _dot3(doc, spv) * e_b
            dk = _dot3_tn(da, qt) * e_k + dkd * e_l
            dq_ref[sl, :] = dq
            dk_ref[sl, :] = dk
            db_last = (jnp.sum(dkd * kd, axis=0, keepdims=True)
                       + jnp.exp(bl) * jnp.sum(ds * spv, axis=0, keepdims=True))
            db = qc * dq - kc * dk + jnp.where(is_last, db_last, 0.0)
            dg_ref[sl, :] = _cum(tri_t, db)
            ds_ref[...] = ds * jnp.exp(bl) + _dot3_tn(doc, p)

    row = lambda w: pl.BlockSpec((SCAN_ROWS, w), lambda h, i: (_blk(mode, i, nb, nc), h))
    return pl.pallas_call(
        body, name=name, grid=(H, nb),
        in_specs=[row(HEAD_DK), row(HEAD_DK), row(dv), row(HEAD_DK),
                  pl.BlockSpec((None, None, sub, dv, HEAD_DK), lambda h, i: (h, _blk(mode, i, nb, nc), 0, 0, 0)),
                  row(dv)],
        out_specs=[row(HEAD_DK), row(HEAD_DK), row(dv), row(HEAD_DK)],
        out_shape=[jax.ShapeDtypeStruct(q.shape, F32), jax.ShapeDtypeStruct(k.shape, F32),
                   jax.ShapeDtypeStruct(v.shape, F32), jax.ShapeDtypeStruct(g.shape, F32)],
        scratch_shapes=[pltpu.VMEM((dv, HEAD_DK), F32)],
        compiler_params=_cparams(("parallel", "arbitrary")),
    )(q, k, v, g, sp, do)


def _gla(name, *, flip, n_ctx):
    @jax.custom_vjp
    def op(q, k, v, g):
        return _gla_fwd_call(q, k, v, g, flip=flip, n_ctx=n_ctx, name=name + "_fwd")[0]

    def fwd(q, k, v, g):
        o, sp = _gla_fwd_call(q, k, v, g, flip=flip, n_ctx=n_ctx, name=name + "_fwd")
        return o, (q, k, v, g, sp)

    def bwd(res, do):
        return tuple(_gla_bwd_call(*res, do, flip=flip, n_ctx=n_ctx, name=name + "_bwd"))

    op.defvjp(fwd, bwd)
    return op


def _s5_table(ar, ai, desc):
    def mul(p, q):
        return p[0] * q[0] - p[1] * q[1], p[0] * q[1] + p[1] * q[0]

    a1 = (ar.reshape(-1), ai.reshape(-1))
    ar = a1[0]
    a2 = mul(a1, a1)
    a4 = mul(a2, a2)
    pw = [a1]
    for _ in range(7):
        pw.append(mul(pw[-1], a1))
    if desc:
        pw = pw[::-1]
    zero = jnp.zeros_like(ar)
    rows = [a1[0], a1[1], a2[0], a2[1], a4[0], a4[1], zero, zero] + [p[0] for p in pw] + [p[1] for p in pw]
    return jnp.stack(rows, axis=0)


S5_LANES = C_GROUPS * C_STATE
S5_LC = 1024


def _s5_scan(xr_ref, xi_ref, tab_ref, car_ref, desc, on_slab=None):
    nslab = S5_ROWS // 8
    rows = lax.broadcasted_iota(jnp.int32, (8, S5_LC), 0)

    def slab(t, carry):
        r0 = pl.multiple_of(((nslab - 1 - t) if desc else t) * 8, 8)
        for lc in range(S5_LANES // S5_LC):
            cs = pl.ds(lc * S5_LC, S5_LC)
            xr, xi = xr_ref[pl.ds(r0, 8), cs], xi_ref[pl.ds(r0, 8), cs]
            for n, s in enumerate((1, 2, 4)):
                ar, ai = tab_ref[2 * n:2 * n + 1, cs], tab_ref[2 * n + 1:2 * n + 2, cs]
                keep = (rows < 8 - s) if desc else (rows >= s)
                sr = jnp.where(keep, pltpu.roll(xr, (8 - s) if desc else s, 0), 0.0)
                si = jnp.where(keep, pltpu.roll(xi, (8 - s) if desc else s, 0), 0.0)
                xr, xi = xr + ar * sr - ai * si, xi + ar * si + ai * sr
            cr, ci = car_ref[0:1, cs], car_ref[1:2, cs]
            pr, pi = tab_ref[8:16, cs], tab_ref[16:24, cs]
            xr, xi = xr + pr * cr - pi * ci, xi + pr * ci + pi * cr
            if on_slab is not None:
                on_slab(r0, cs, xr, xi, cr, ci, rows)
            xr_ref[pl.ds(r0, 8), cs] = xr
            xi_ref[pl.ds(r0, 8), cs] = xi
            e = 0 if desc else 7
            car_ref[0:1, cs] = xr[e:e + 1, :]
            car_ref[1:2, cs] = xi[e:e + 1, :]
        return carry

    lax.fori_loop(0, nslab, slab, 0)


def _s5_fwd_call(u, tab, wb_re, wb_im, wc_re, wc_im, *, flip, n_ctx, name):
    L = u.shape[0]
    nb, nc = L // S5_ROWS, n_ctx // S5_ROWS
    mode, desc = (2, True) if flip else (0, False)

    def body(u_ref, tab_ref, wbr_ref, wbi_ref, wcr_ref, wci_ref, y_ref, xr_ref, xi_ref, car_ref):
        @pl.when(pl.program_id(0) == 0)
        def _():
            car_ref[...] = jnp.zeros_like(car_ref)

        for gb in range(S5_GB):
            us, cs = pl.ds(gb * 128, 128), pl.ds(gb * S5_GBW, S5_GBW)
            ub = u_ref[:, us]
            xr_ref[:, cs] = _dot(ub, wbr_ref[gb])
            xi_ref[:, cs] = _dot(ub, wbi_ref[gb])
        _s5_scan(xr_ref, xi_ref, tab_ref, car_ref, desc)
        for gb in range(S5_GB):
            us, cs = pl.ds(gb * 128, 128), pl.ds(gb * S5_GBW, S5_GBW)
            y_ref[:, us] = _dot(xr_ref[:, cs], wcr_ref[gb]) - _dot(xi_ref[:, cs], wci_ref[gb])

    row = lambda w: pl.BlockSpec((S5_ROWS, w), lambda i: (_blk(mode, i, nb, nc), 0))
    full = lambda a: pl.BlockSpec(a.shape, lambda i: (0,) * a.ndim)
    return pl.pallas_call(
        body, name=name, grid=(nb,),
        in_specs=[row(BRANCH_W), full(tab), full(wb_re), full(wb_im), full(wc_re), full(wc_im)],
        out_specs=[row(BRANCH_W), row(S5_LANES), row(S5_LANES)],
        out_shape=[jax.ShapeDtypeStruct((L, BRANCH_W), F32), jax.ShapeDtypeStruct((L, S5_LANES), F32),
                   jax.ShapeDtypeStruct((L, S5_LANES), F32)],
        scratch_shapes=[pltpu.VMEM((8, S5_LANES), F32)],
        compiler_params=_cparams(("arbitrary",)),
    )(u, tab, wb_re, wb_im, wc_re, wc_im)


def _s5_bwd_call(dy, u, xr, xi, tab, wbt_re, wbt_im, wct_re, wct_im, *, flip, n_ctx, name):
    L = u.shape[0]
    nb, nc = L // S5_ROWS, n_ctx // S5_ROWS
    mode, desc = (3, False) if flip else (1, True)

    def body(dy_ref, u_ref, xr_ref, xi_ref, tab_ref, wbtr_ref, wbti_ref, wctr_ref, wcti_ref,
             du_ref, dwbr_ref, dwbi_ref, dwcr_ref, dwci_ref, ga_ref, gr_ref, gi_ref, car_ref):
        @pl.when(pl.program_id(0) == 0)
        def _():
            car_ref[...] = jnp.zeros_like(car_ref)
            ga_ref[...] = jnp.zeros_like(ga_ref)
            for r in (dwbr_ref, dwbi_ref, dwcr_ref, dwci_ref):
                r[...] = jnp.zeros_like(r)

        for gb in range(S5_GB):
            us, cs = pl.ds(gb * 128, 128), pl.ds(gb * S5_GBW, S5_GBW)
            dyb = dy_ref[:, us]
            gr_ref[:, cs] = _dot(dyb, wctr_ref[gb])
            gi_ref[:, cs] = -_dot(dyb, wcti_ref[gb])

        def lam_grad(r0, cs, gr, gi, cr, ci, rows):
            edge = 7 if desc else 0
            pr = jnp.where(rows == edge, cr, pltpu.roll(gr, 7 if desc else 1, 0))
            pi = jnp.where(rows == edge, ci, pltpu.roll(gi, 7 if desc else 1, 0))
            x_r, x_i = xr_ref[pl.ds(r0, 8), cs], xi_ref[pl.ds(r0, 8), cs]
            ga_ref[0:8, cs] += x_r * pr + x_i * pi
            ga_ref[8:16, cs] += x_r * pi - x_i * pr

        _s5_scan(gr_ref, gi_ref, tab_ref, car_ref, desc, on_slab=lam_grad)
        for gb in range(S5_GB):
            us, cs = pl.ds(gb * 128, 128), pl.ds(gb * S5_GBW, S5_GBW)
            g_r, g_i, ub, dyb = gr_ref[:, cs], gi_ref[:, cs], u_ref[:, us], dy_ref[:, us]
            du_ref[:, us] = _dot(g_r, wbtr_ref[gb]) + _dot(g_i, wbti_ref[gb])
            dwbr_ref[gb] += _dot_tn(ub, g_r)
            dwbi_ref[gb] += _dot_tn(ub, g_i)
            dwcr_ref[gb] += _dot_tn(xr_ref[:, cs], dyb)
            dwci_ref[gb] -= _dot_tn(xi_ref[:, cs], dyb)

    row = lambda w: pl.BlockSpec((S5_ROWS, w), lambda i: (_blk(mode, i, nb, nc), 0))
    full = lambda shape: pl.BlockSpec(shape, lambda i: (0,) * len(shape))
    wb_shape, wc_shape = (S5_GB, 128, S5_GBW), (S5_GB, S5_GBW, 128)
    return pl.pallas_call(
        body, name=name, grid=(nb,),
        in_specs=[row(BRANCH_W), row(BRANCH_W), row(S5_LANES), row(S5_LANES), full(tab.shape),
                  full(wc_shape), full(wc_shape), full(wb_shape), full(wb_shape)],
        out_specs=[row(BRANCH_W), full(wb_shape), full(wb_shape), full(wc_shape), full(wc_shape),
                   full((16, S5_LANES))],
        out_shape=[jax.ShapeDtypeStruct((L, BRANCH_W), F32),
                   jax.ShapeDtypeStruct(wb_shape, F32), jax.ShapeDtypeStruct(wb_shape, F32),
                   jax.ShapeDtypeStruct(wc_shape, F32), jax.ShapeDtypeStruct(wc_shape, F32),
                   jax.ShapeDtypeStruct((16, S5_LANES), F32)],
        scratch_shapes=[pltpu.VMEM((S5_ROWS, S5_LANES), F32), pltpu.VMEM((S5_ROWS, S5_LANES), F32),
                        pltpu.VMEM((8, S5_LANES), F32)],
        compiler_params=_cparams(("arbitrary",)),
    )(dy, u, xr, xi, tab, wbt_re, wbt_im, wct_re, wct_im)


def _s5(name, *, flip, n_ctx):
    desc_f = flip
    desc_b = not flip

    def run_fwd(u, lr, li, wbr, wbi, wcr, wci):
        tab = _s5_table(lr, li, desc_f)
        c16 = lambda a: a.astype(BF16)
        return _s5_fwd_call(u, tab, c16(wbr), c16(wbi), c16(wcr), c16(wci), flip=flip, n_ctx=n_ctx, name=name + "_fwd")

    @jax.custom_vjp
    def op(u, lr, li, wbr, wbi, wcr, wci):
        return run_fwd(u, lr, li, wbr, wbi, wcr, wci)[0]

    def fwd(u, lr, li, wbr, wbi, wcr, wci):
        y, xr, xi = run_fwd(u, lr, li, wbr, wbi, wcr, wci)
        return y, (u, xr, xi, lr, li, wbr, wbi, wcr, wci)

    def bwd(res, dy):
        u, xr, xi, lr, li, wbr, wbi, wcr, wci = res
        tab = _s5_table(lr, -li, desc_b)
        t16 = lambda a: jnp.swapaxes(a, 1, 2).astype(BF16)
        du, dwbr, dwbi, dwcr, dwci, ga = _s5_bwd_call(
            dy, u, xr, xi, tab, t16(wbr), t16(wbi), t16(wcr), t16(wci), flip=flip, n_ctx=n_ctx, name=name + "_bwd")
        return (du, jnp.sum(ga[0:8], axis=0, keepdims=True), jnp.sum(ga[8:16], axis=0, keepdims=True),
                dwbr, dwbi, dwcr, dwci)

    op.defvjp(fwd, bwd)
    return op


def _loss_call(h, tgt, g, *, n_ctx, name):
    L, D = h.shape
    tr = 128
    ncx = n_ctx // tr

    def body(h_ref, t_ref, g_ref, dh_ref, dg_ref, loss_ref):
        i = pl.program_id(0)

        @pl.when(i == 0)
        def _():
            dg_ref[...] = jnp.zeros_like(dg_ref)
            loss_ref[...] = jnp.zeros_like(loss_ref)

        @pl.when(i < ncx)
        def _():
            dh_ref[...] = jnp.zeros_like(dh_ref)

        @pl.when(i >= ncx)
        def _():
            t = t_ref[...]

            def f(hv, gv):
                e = _rms(hv, gv) - t
                return 0.5 * jnp.sum(e * e) / D

            l, (dh, dg) = jax.value_and_grad(f, argnums=(0, 1))(h_ref[...], g_ref[...])
            dh_ref[...] = dh
            dg_ref[...] += dg
            loss_ref[...] += l

    return pl.pallas_call(
        body, name=name, grid=(L // tr,),
        in_specs=[pl.BlockSpec((tr, D), lambda i: (i, 0)),
                  pl.BlockSpec((tr, D), lambda i: (jnp.maximum(i - ncx, 0), 0)),
                  pl.BlockSpec((1, D), lambda i: (0, 0))],
        out_specs=[pl.BlockSpec((tr, D), lambda i: (i, 0)), pl.BlockSpec((1, D), lambda i: (0, 0)),
                   pl.BlockSpec((8, 128), lambda i: (0, 0))],
        out_shape=[jax.ShapeDtypeStruct((L, D), F32), jax.ShapeDtypeStruct((1, D), F32),
                   jax.ShapeDtypeStruct((8, 128), F32)],
        compiler_params=_cparams(("arbitrary",)),
    )(h, tgt, g)


def _adamw_call(w, g, m, v, *, name):
    R, C = w.shape
    tr = R if R * C * 4 <= (1 << 20) else _tile(R, tuple(t for t in (512, 256, 128, 64, 32, 16, 8) if t * C * 4 <= (1 << 20)))
    c1 = 1.0 / (1.0 - ADAM_B1 ** ADAM_STEP)
    c2 = 1.0 / (1.0 - ADAM_B2 ** ADAM_STEP)

    def body(w_ref, g_ref, m_ref, v_ref, d_ref, nm_ref, nv_ref):
        gv = g_ref[...]
        nm = ADAM_B1 * m_ref[...] + (1.0 - ADAM_B1) * gv
        nv = ADAM_B2 * v_ref[...] + (1.0 - ADAM_B2) * gv * gv
        d_ref[...] = -ADAM_LR * ((nm * c1) / (jnp.sqrt(nv * c2) + ADAM_EPS) + ADAM_WD * w_ref[...])
        nm_ref[...] = nm
        nv_ref[...] = nv

    spec = pl.BlockSpec((tr, C), lambda i: (i, 0))
    return pl.pallas_call(
        body, name=name, grid=(R // tr,), in_specs=[spec] * 4, out_specs=[spec] * 3,
        out_shape=[jax.ShapeDtypeStruct((R, C), F32)] * 3,
        compiler_params=_cparams(("parallel",)),
    )(w, g, m, v)


def _as2d(a):
    n = a.size
    if a.ndim >= 2 and n >= (1 << 20):
        return a.reshape(-1, a.shape[-1])
    if n % 128 == 0:
        return a.reshape(n // 128, 128)
    return a.reshape(1, n)


def _adamw(w, g, m, v, *, name):
    d, nm, nv = _adamw_call(_as2d(w), _as2d(g.astype(F32)), _as2d(m), _as2d(v), name=name)
    return d.reshape(w.shape), nm.reshape(w.shape), nv.reshape(w.shape)


MESH = pl.DeviceIdType.MESH
ANY = pl.BlockSpec(memory_space=pl.ANY)


def _place():
    return lax.axis_index("x"), lax.axis_index("y"), lax.axis_index("c")


def _all_gather(x, *, name):
    M, N = x.shape

    def body(x_ref, out_ref, send_sems, recv_sems, local_sem):
        x_, y_, c_ = _place()
        me, sibling = (x_, y_, c_), (x_, y_, 1 - c_)
        chips = [(1 - x_, y_), (x_, 1 - y_), (1 - x_, 1 - y_)]

        def slab(px, py, pc):
            return out_ref.at[4 * px + 2 * py + pc]

        def copy(k, block, to, src=None):
            return pltpu.make_async_remote_copy(
                src_ref=slab(*block) if src is None else src, dst_ref=slab(*block),
                send_sem=send_sems.at[k], recv_sem=recv_sems.at[k], device_id=to, device_id_type=MESH)

        mine = pltpu.make_async_copy(x_ref, slab(*me), local_sem)
        mine.start()
        first = [copy(0, me, sibling, src=x_ref)]
        first += [copy(1 + j, me, (*chip, c_), src=x_ref) for j, chip in enumerate(chips)]
        for cp in first:
            cp.start()
        passed = [copy(4 + j, (*chip, c_), sibling) for j, chip in enumerate(chips)]
        for j, chip in enumerate(chips):
            copy(1 + j, (*chip, c_), me).wait_recv()
            passed[j].start()
        copy(0, sibling, me).wait_recv()
        for j, chip in enumerate(chips):
            copy(4 + j, (*chip, 1 - c_), me).wait_recv()
        for cp in first + passed:
            cp.wait_send()
        mine.wait()

    return pl.pallas_call(
        body, name=name, out_shape=jax.ShapeDtypeStruct((N_DEV, M, N), x.dtype),
        in_specs=[ANY], out_specs=ANY,
        scratch_shapes=[pltpu.SemaphoreType.DMA((7,)), pltpu.SemaphoreType.DMA((7,)), pltpu.SemaphoreType.DMA],
    )(x)


def _pair_exchange(g, *, name):
    _, R, C = g.shape

    def body(g_ref, out_ref, send_sems, recv_sems):
        x_, y_, c_ = _place()
        copies = []
        for k in range(4):
            cp = pltpu.make_async_remote_copy(
                src_ref=g_ref.at[2 * k + (1 - c_)], dst_ref=out_ref.at[k], send_sem=send_sems.at[k],
                recv_sem=recv_sems.at[k], device_id=(x_, y_, 1 - c_), device_id_type=MESH)
            cp.start()
            copies.append(cp)
        for cp in copies:
            cp.wait()

    return pl.pallas_call(
        body, name=name, out_shape=jax.ShapeDtypeStruct((4, R, C), g.dtype), in_specs=[ANY], out_specs=ANY,
        scratch_shapes=[pltpu.SemaphoreType.DMA((4,)), pltpu.SemaphoreType.DMA((4,))],
    )(g)


def _chip_exchange(p, *, name):
    _, R, C = p.shape

    def body(p_ref, out_ref, send_sems, recv_sems):
        x_, y_, c_ = _place()
        chips = [(1 - x_, y_), (x_, 1 - y_), (1 - x_, 1 - y_)]
        copies = []
        for j, (cx, cy) in enumerate(chips):
            cp = pltpu.make_async_remote_copy(
                src_ref=p_ref.at[2 * cx + cy], dst_ref=out_ref.at[j], send_sem=send_sems.at[j],
                recv_sem=recv_sems.at[j], device_id=(cx, cy, c_), device_id_type=MESH)
            cp.start()
            copies.append(cp)
        for cp in copies:
            cp.wait()

    return pl.pallas_call(
        body, name=name, out_shape=jax.ShapeDtypeStruct((3, R, C), p.dtype), in_specs=[ANY], out_specs=ANY,
        scratch_shapes=[pltpu.SemaphoreType.DMA((3,)), pltpu.SemaphoreType.DMA((3,))],
    )(p)


def _add_tr(R, C):
    return _tile(R, tuple(t for t in (512, 256, 128, 64, 32, 16, 8) if t * C * 4 <= (2 << 20)))


def _pair_add(g, recv, *, name):
    _, R, C = g.shape
    tr = _add_tr(R, C)
    core = lax.axis_index("c").astype(jnp.int32).reshape(1)

    def body(c_ref, g_ref, r_ref, o_ref):
        o_ref[...] = (g_ref[...] + r_ref[...]).astype(o_ref.dtype)

    return pl.pallas_call(
        body, name=name, out_shape=jax.ShapeDtypeStruct((4, R, C), BF16),
        grid_spec=pltpu.PrefetchScalarGridSpec(
            num_scalar_prefetch=1, grid=(4, R // tr),
            in_specs=[pl.BlockSpec((None, tr, C), lambda k, i, c: (2 * k + c[0], i, 0)),
                      pl.BlockSpec((None, tr, C), lambda k, i, c: (k, i, 0))],
            out_specs=pl.BlockSpec((None, tr, C), lambda k, i, c: (k, i, 0))),
        compiler_params=_cparams(("parallel", "parallel")),
    )(core, g, recv)


def _chip_add(p, recv, *, name):
    _, R, C = p.shape
    tr = _add_tr(R, C)
    chip = (2 * lax.axis_index("x") + lax.axis_index("y")).astype(jnp.int32).reshape(1)

    def body(c_ref, p_ref, r0_ref, r1_ref, r2_ref, o_ref):
        f = lambda r: r[...].astype(F32)
        o_ref[...] = ((f(p_ref) + f(r0_ref)) + f(r1_ref)) + f(r2_ref)

    rspec = lambda j: pl.BlockSpec((None, tr, C), lambda i, c: (j, i, 0))
    return pl.pallas_call(
        body, name=name, out_shape=jax.ShapeDtypeStruct((R, C), F32),
        grid_spec=pltpu.PrefetchScalarGridSpec(
            num_scalar_prefetch=1, grid=(R // tr,),
            in_specs=[pl.BlockSpec((None, tr, C), lambda i, c: (c[0], i, 0)), rspec(0), rspec(1), rspec(2)],
            out_specs=pl.BlockSpec((tr, C), lambda i, c: (i, 0))),
        compiler_params=_cparams(("parallel",)),
    )(chip, p, recv, recv, recv)


def _reduce_scatter(g, *, name):
    p = _pair_add(g, _pair_exchange(g, name=name + "_pair"), name=name + "_pair_add")
    return _chip_add(p, _chip_exchange(p, name=name + "_chip"), name=name + "_chip_add")


def _sum8(a, *, name):
    _, R, C = a.shape
    tr = _tile(R, tuple(t for t in (512, 256, 128, 64, 32, 16, 8) if t * C * 4 * 8 <= (4 << 20)))

    def body(a_ref, o_ref):
        acc = a_ref[0]
        for d in range(1, N_DEV):
            acc = acc + a_ref[d]
        o_ref[...] = acc

    return pl.pallas_call(
        body, name=name, grid=(R // tr,), in_specs=[pl.BlockSpec((N_DEV, tr, C), lambda i: (0, i, 0))],
        out_specs=pl.BlockSpec((tr, C), lambda i: (i, 0)), out_shape=jax.ShapeDtypeStruct((R, C), F32),
        compiler_params=_cparams(("parallel",)),
    )(a)


def _s5_params(lam_re, lam_im, log_dt, b_re, b_im, c_re, c_im):
    lr = jnp.minimum(lam_re, C_MAX_RE)
    li = lam_im
    dt = jnp.exp(log_dt)[:, None]
    mag = jnp.exp(lr * dt)
    ar, ai = mag * jnp.cos(li * dt), mag * jnp.sin(li * dt)
    nr, ni, den = ar - 1.0, ai, lr * lr + li * li
    fr, fi = (nr * lr + ni * li) / den, (ni * lr - nr * li) / den
    bbr = fr[..., None] * b_re - fi[..., None] * b_im
    bbi = fr[..., None] * b_im + fi[..., None] * b_re
    eye = jnp.eye(S5_GB, dtype=F32)

    def dense_b(bb):
        t = bb.reshape(S5_GB, S5_GB, C_STATE, C_GROUP).transpose(0, 1, 3, 2)
        return (t[:, :, :, None, :] * eye[None, :, None, :, None]).reshape(S5_GB, S5_GB * C_GROUP, S5_GBW)

    def dense_c(cc):
        t = cc.reshape(S5_GB, S5_GB, C_GROUP, C_STATE).transpose(0, 1, 3, 2)
        return (t[:, :, :, None, :] * eye[None, :, None, :, None]).reshape(S5_GB, S5_GBW, S5_GB * C_GROUP)

    flat = lambda a: a.reshape(1, S5_LANES)
    return flat(ar), flat(ai), dense_b(bbr), dense_b(bbi), dense_c(c_re), dense_c(c_im)


def _rope_tables(n_ctx, n_lat):
    quarter = HEAD_DK // 4
    freqs = ROPE_BASE ** (-jnp.arange(quarter, dtype=F32) / quarter)
    t = jnp.arange(n_lat)
    r = (t // GRID_W).astype(F32)
    col = (t % GRID_W).astype(F32)
    ang = jnp.concatenate([r[:, None] * freqs, col[:, None] * freqs], axis=-1)
    ang = jnp.concatenate([jnp.zeros((n_ctx, HEAD_DK // 2), F32), ang], axis=0)
    cos, sin = jnp.cos(ang), jnp.sin(ang)
    idx = jnp.arange(HEAD_DK)
    swap = (idx[:, None] == (idx[None, :] + HEAD_DK // 2) % HEAD_DK).astype(F32)
    return jnp.concatenate([cos, cos], axis=1), jnp.concatenate([-sin, sin], axis=1), swap


W_IN = 13344
W_INP = 13440
_COL = dict(a_q=0, a_ff=1024, a_fb=2048, a_i=3072, a_g=4096, b_q=5120, b_k=5632, b_v=6144, b_g=7168,
            c_u=8192, c_g=9216, d_q=10240, d_k=10752, d_v=11264, d_g=12288, b_lr=13312)


_COL_ORDER = (("a_q", 1024), ("a_ff", 1024), ("a_fb", 1024), ("a_i", 1024), ("a_g", 1024), ("b_q", 512), ("b_k", 512),
              ("b_v", 1024), ("b_g", 1024), ("c_u", 1024), ("c_g", 1024), ("d_q", 512), ("d_k", 512), ("d_v", 1024),
              ("d_g", 1024), ("b_lr", 128))
_W_IN_MAP = ((0, 7168, 0), (7200, W_IN, 7168), (7168, 7200, 13312))
W_IN_SHARD = W_IN // N_DEV


def _w_in_from_shards(g):
    parts = []
    for lo, hi, _ in _W_IN_MAP:
        for d in range(N_DEV):
            a, b = max(lo, d * W_IN_SHARD), min(hi, (d + 1) * W_IN_SHARD)
            if a < b:
                parts.append(g[d, ..., a - d * W_IN_SHARD:b - d * W_IN_SHARD])
    parts.append(jnp.zeros(g.shape[1:-1] + (W_INP - W_IN,), g.dtype))
    return jnp.concatenate(parts, axis=-1)


def _w_in_to_shards(w):
    shards = []
    for d in range(N_DEV):
        parts = []
        for lo, hi, here in sorted(_W_IN_MAP):
            a, b = max(lo, d * W_IN_SHARD), min(hi, (d + 1) * W_IN_SHARD)
            if a < b:
                parts.append(w[..., here + a - lo:here + b - lo])
        shards.append(jnp.concatenate(parts, axis=-1))
    return jnp.stack(shards)


@jax.custom_vjp
def _split_cols(proj):
    return tuple(lax.slice_in_dim(proj, _COL[n], _COL[n] + w, axis=1) for n, w in _COL_ORDER)


def _split_cols_fwd(proj):
    return _split_cols(proj), None


def _split_cols_bwd(_, cts):
    return (jnp.concatenate(cts, axis=1),)


_split_cols.defvjp(_split_cols_fwd, _split_cols_bwd)


def _layer_ops(l, n_ctx):
    n = f"l{l}_"
    rw = functools.partial(_rowwise, n_ctx=n_ctx)
    wide = [(BRANCH_W, F32)]
    return dict(
        modnorm=rw(_modnorm_fn, ["row", "const", "seg", "seg"], [(4096, BF16)], ncb=1, tr=128, name=n + "modnorm"),
        resid=rw(_resid_fn, ["row", "row", "seg"], [(4096, F32)], ncb=1, tr=128, name=n + "resid"),
        lin_in=_linear(n + "lin_in"), lin_in_gather=_linear_gather(n + "lin_in"),
        lin_out=_linear(n + "lin_out"), lin_glu=_linear(n + "lin_glu"),
        hgrn_pro=rw(_hgrn_pro_fn, ["row", "row", "const", "const"], wide * 4, ncb=1, tr=256, name=n + "hgrn_pro"),
        hgrn_f=_gla(n + "hgrn_f", flip=False, n_ctx=n_ctx), hgrn_b=_gla(n + "hgrn_b", flip=True, n_ctx=n_ctx),
        hgrn_epi=rw(_hgrn_epi_fn, ["row", "row", "row", "const"], [(BRANCH_W, BF16)], ncb=8, tr=256, name=n + "hgrn_epi"),
        gla_pro=rw(_gla_pro_fn, ["row", "row", "const", "const", "const", "const"], [(512, F32)] * 3, ncb=1, tr=256,
                   name=n + "gla_pro"),
        gla_f=_gla(n + "gla_f", flip=False, n_ctx=n_ctx), gla_b=_gla(n + "gla_b", flip=True, n_ctx=n_ctx),
        gla_epi=rw(_hgrn_epi_fn, ["row", "row", "row", "const"], [(BRANCH_W, BF16)], ncb=4, tr=256, name=n + "gla_epi"),
        s5_f=_s5(n + "s5_f", flip=False, n_ctx=n_ctx), s5_b=_s5(n + "s5_b", flip=True, n_ctx=n_ctx),
        s5_mid=rw(_s5_mid_fn, ["row", "row", "row", "const"], wide, ncb=1, tr=256, name=n + "s5_mid"),
        s5_epi=rw(_s5_epi_fn, ["row", "row", "row", "const"], [(BRANCH_W, BF16)], ncb=1, tr=256, name=n + "s5_epi"),
        ret_pro=rw(_ret_pro_fn, ["row", "row", "rowc", "rowc", "col", "col", "fixed"], [(512, F32)] * 4, ncb=4, tr=256,
                   name=n + "ret_pro"),
        ret_f=_gla(n + "ret_f", flip=False, n_ctx=n_ctx), ret_b=_gla(n + "ret_b", flip=True, n_ctx=n_ctx),
        ret_epi=rw(_ret_epi_fn, ["row", "row", "row", "const"], [(BRANCH_W, BF16)], ncb=4, tr=256, name=n + "ret_epi"),
    )


def _layer(ops, h, p, w_in, w_out, w_glu, lb, rope, sides=()):
    cos2, sin2, swap = rope
    (hn,) = ops["modnorm"](h, p["norm_g"], p["sc"], p["sh"])
    if not sides:
        proj, got = ops["lin_in"](hn, w_in, p["shadow_in"]), ()
    else:
        proj, *got = ops["lin_in_gather"](hn, w_in, p["shadow_in"], tuple(sides))
        got = [lax.stop_gradient(g) for g in got]
    w_out = w_out(got) if callable(w_out) else w_out
    w_glu = w_glu(got) if callable(w_glu) else w_glu
    cols = dict(zip([n for n, _ in _COL_ORDER], _split_cols(proj)))
    col = lambda name, w: cols[name]

    a_q, a_i = col("a_q", 1024), col("a_i", 1024)
    kf, gf, kb, gb = ops["hgrn_pro"](col("a_ff", 1024), col("a_fb", 1024), lb[0:1], lb[1:2])
    (o_a,) = ops["hgrn_epi"](ops["hgrn_f"](a_q, kf, a_i, gf), ops["hgrn_b"](a_q, kb, a_i, gb), col("a_g", 1024),
                             p["hgrn_norm_g"])

    rank = B_GATE_RANK
    wf = jnp.pad(p["gla_w_gk"][0], ((0, 128 - rank), (0, 0)))
    wb = jnp.pad(p["gla_w_gk"][1], ((rank, 128 - 2 * rank), (0, 0)))
    b_k, b_v = col("b_k", 512), col("b_v", 1024)
    qs, gf, gb = ops["gla_pro"](col("b_q", 512), col("b_lr", 128), wf, wb, p["gla_b_gk"][0:1], p["gla_b_gk"][1:2])
    (o_b,) = ops["gla_epi"](ops["gla_f"](qs, b_k, b_v, gf), ops["gla_b"](qs, b_k, b_v, gb), col("b_g", 1024),
                            p["gla_norm_g"])

    c_u = col("c_u", 1024)
    s5p = lambda d: _s5_params(*[p[k][d] for k in ("s5_lam_re", "s5_lam_im", "s5_log_dt", "s5_b_re", "s5_b_im",
                                                  "s5_c_re", "s5_c_im")])
    (z,) = ops["s5_mid"](ops["s5_f"](c_u, *s5p(0)), ops["s5_b"](c_u, *s5p(1)), c_u, p["s5_d"].reshape(1, BRANCH_W))
    t = ops["lin_glu"](z, w_glu, p["shadow_glu"])
    (o_c,) = ops["s5_epi"](z, t, col("c_g", 1024), p["s5_b_glu"])

    log_gamma = _log_sigmoid(p["ret_decay_logit"])
    lg = lambda d: jnp.broadcast_to(log_gamma[d][:, None, None], (4, 1, HEAD_DK))
    d_v = col("d_v", 1024)
    qs, ks, gf, gb = ops["ret_pro"](col("d_q", 512), col("d_k", 512), cos2, sin2, lg(0), lg(1), swap)
    (o_d,) = ops["ret_epi"](ops["ret_f"](qs, ks, d_v, gf), ops["ret_b"](qs, ks, d_v, gb), col("d_g", 1024),
                            p["ret_norm_g"])

    o = jnp.concatenate([o_a, o_b, o_c, o_d], axis=1)
    y = ops["lin_out"](o, w_out, p["shadow_out"])
    (h,) = ops["resid"](h, y, p["gt"])
    return h, got


_SMALL = ("norm_g", "hgrn_lb_logits", "hgrn_norm_g", "gla_w_gk", "gla_b_gk", "gla_norm_g", "s5_lam_re", "s5_lam_im",
          "s5_log_dt", "s5_b_re", "s5_b_im", "s5_c_re", "s5_c_im", "s5_d", "s5_b_glu", "ret_decay_logit", "ret_norm_g",
          "final_norm_g")
_WEIGHTS = ("c_ctx", "norm_g", "w_ada", "b_ada", "w_in", "hgrn_lb_logits", "hgrn_norm_g", "gla_w_gk", "gla_b_gk",
            "gla_norm_g", "s5_lam_re", "s5_lam_im", "s5_log_dt", "s5_b_re", "s5_b_im", "s5_c_re", "s5_c_im", "s5_d",
            "s5_w_glu", "s5_b_glu", "ret_decay_logit", "ret_norm_g", "w_out", "final_norm_g")


def _pad_rows(flat, mult=8):
    rows = -(-flat.size // 128)
    rows = -(-rows // mult) * mult
    return jnp.pad(flat, (0, rows * 128 - flat.size)).reshape(rows, 128)


def _unshard_cols(g, inner):
    k = len(inner)
    a = g.reshape((N_DEV,) + tuple(inner))
    return jnp.moveaxis(a, 0, k - 1).reshape(tuple(inner[:-1]) + (N_DEV * inner[-1],))


def kernel(x, c, ctx, c_ctx, norm_g, w_ada, b_ada, w_in, hgrn_lb_logits, hgrn_norm_g, gla_w_gk, gla_b_gk, gla_norm_g, s5_lam_re, s5_lam_im, s5_log_dt, s5_b_re, s5_b_im, s5_c_re, s5_c_im, s5_d, s5_w_glu, s5_b_glu, ret_decay_logit, ret_norm_g, w_out, final_norm_g, loss_target, m_c_ctx, m_norm_g, m_w_ada, m_b_ada, m_w_in, m_hgrn_lb_logits, m_hgrn_norm_g, m_gla_w_gk, m_gla_b_gk, m_gla_norm_g, m_s5_lam_re, m_s5_lam_im, m_s5_log_dt, m_s5_b_re, m_s5_b_im, m_s5_c_re, m_s5_c_im, m_s5_d, m_s5_w_glu, m_s5_b_glu, m_ret_decay_logit, m_ret_norm_g, m_w_out, m_final_norm_g, v_c_ctx, v_norm_g, v_w_ada, v_b_ada, v_w_in, v_hgrn_lb_logits, v_hgrn_norm_g, v_gla_w_gk, v_gla_b_gk, v_gla_norm_g, v_s5_lam_re, v_s5_lam_im, v_s5_log_dt, v_s5_b_re, v_s5_b_im, v_s5_c_re, v_s5_c_im, v_s5_d, v_s5_w_glu, v_s5_b_glu, v_ret_decay_logit, v_ret_norm_g, v_w_out, v_final_norm_g):
    given = dict(locals())
    depth = norm_g.shape[0]
    n_ctx, n_lat, d_model = ctx.shape[1], x.shape[1], x.shape[2]
    me = 4 * lax.axis_index("x") + 2 * lax.axis_index("y") + lax.axis_index("c")
    w_ada_cols = w_ada.shape[2]

    pack = jnp.concatenate([c.reshape(-1), hgrn_lb_logits.reshape(-1), gla_w_gk.reshape(-1), gla_b_gk.reshape(-1)])
    got = _all_gather(_pad_rows(pack), name="gather_small").reshape(N_DEV, -1)
    o0 = d_model
    o1 = o0 + hgrn_lb_logits.size
    o2 = o1 + gla_w_gk.size
    o3 = o2 + gla_b_gk.size
    c_all = got[:, :o0]
    lb_logits = _unshard_cols(got[:, o0:o1], hgrn_lb_logits.shape)
    w_gk = _unshard_cols(got[:, o1:o2], gla_w_gk.shape)
    b_gk = _unshard_cols(got[:, o2:o3], gla_b_gk.shape)

    cond = jnp.concatenate([c_all, c_ctx[None, :], jnp.zeros((7, d_model), F32)], axis=0)
    act = _silu(cond).astype(BF16)
    mod_part = jnp.concatenate([_mm(act, w_ada[l], name=f"ada{l}_fwd") for l in range(depth)], axis=0)
    mod = _all_gather(mod_part, name="gather_mod").reshape(N_DEV, depth, 16, w_ada_cols)
    mod = jnp.moveaxis(mod, 0, 2).reshape(depth, 16, N_DEV * w_ada_cols) + b_ada[:, None, :]
    mod_lat = lax.dynamic_index_in_dim(mod, me, axis=1, keepdims=False)
    mod_ctx = mod[:, 8]
    seg = lambda l, j: jnp.stack([mod_ctx[l, j * d_model:(j + 1) * d_model],
                                  mod_lat[l, j * d_model:(j + 1) * d_model]])[:, None, :]

    w_in_shards = [w_in[l].astype(BF16) for l in range(depth)]
    w_out_shards = [w_out[l].astype(BF16) for l in range(depth)]
    w_glu_shard = s5_w_glu.astype(BF16).reshape(-1, BRANCH_W)
    w_in_first = _w_in_from_shards(_all_gather(w_in_shards[0], name="gather_w_in"))

    rope = _rope_tables(n_ctx, n_lat)
    ops = [_layer_ops(l, n_ctx) for l in range(depth)]
    params = dict(
        layers=[dict(
            norm_g=norm_g[l][None, :], sh=seg(l, 0), sc=seg(l, 1), gt=seg(l, 2),
            shadow_in=jnp.zeros((d_model, W_INP), F32), shadow_out=jnp.zeros((4 * BRANCH_W, d_model), F32),
            shadow_glu=jnp.zeros((BRANCH_W, BRANCH_W), F32),
            hgrn_norm_g=hgrn_norm_g[l][None, :], gla_w_gk=w_gk[l], gla_b_gk=b_gk[l], gla_norm_g=gla_norm_g[l][None, :],
            s5_lam_re=s5_lam_re[l], s5_lam_im=s5_lam_im[l], s5_log_dt=s5_log_dt[l], s5_b_re=s5_b_re[l],
            s5_b_im=s5_b_im[l], s5_c_re=s5_c_re[l], s5_c_im=s5_c_im[l], s5_d=s5_d[l], s5_b_glu=s5_b_glu[l][None, :],
            ret_decay_logit=ret_decay_logit[l], ret_norm_g=ret_norm_g[l][None, :]) for l in range(depth)],
        lb_logits=lb_logits,
        h0=jnp.concatenate([ctx[0], x[0]], axis=0))

    def body(p):
        lb_p = jax.nn.softmax(p["lb_logits"], axis=0)
        lower = jnp.cumsum(lb_p, axis=0) - lb_p[0:1]
        h = p["h0"]
        w_in_l, w_glu_all = w_in_first, None
        glu_of = lambda g: jnp.moveaxis(g.reshape(N_DEV, depth, -1, BRANCH_W), 0, 1).reshape(depth, BRANCH_W, BRANCH_W)
        for l in range(depth):
            more = l + 1 < depth
            sides = ([w_in_shards[l + 1]] if more else []) + [w_out_shards[l]] + ([w_glu_shard] if l == 0 else [])
            k_out = 1 if more else 0
            pick_out = lambda got, k=k_out: got[k].reshape(4 * BRANCH_W, d_model)
            pick_glu = (lambda got, k=k_out + 1: glu_of(got[k])[0]) if l == 0 else w_glu_all[l]
            h, got = _layer(ops[l], h, p["layers"][l], w_in_l, pick_out, pick_glu, lower[l], rope, sides=sides)
            if l == 0:
                w_glu_all = glu_of(got[k_out + 1])
            if more:
                w_in_l = _w_in_from_shards(got[0])
        return h

    h_out, pullback = jax.vjp(body, params)
    dh, dg_final, loss_part = _loss_call(h_out, loss_target[0], final_norm_g[None, :], n_ctx=n_ctx, name="loss_head")
    (grads,) = pullback(dh)
    loss = lax.psum(loss_part[0, 0], ("x", "y", "c"))
    gl = grads["layers"]

    dmod = jnp.stack([jnp.concatenate([gl[l][k][s, 0] for k in ("sh", "sc", "gt")])
                      for l in range(depth) for s in (1, 0)])
    dmod = jnp.pad(dmod, ((0, 8 - 2 * depth), (0, 0)))
    dmod_all = _all_gather(dmod, name="gather_dmod")
    dmod_sum = _sum8(dmod_all, name="sum_dmod")
    grad_b_ada = jnp.stack([dmod_sum[2 * l] + dmod_sum[2 * l + 1] for l in range(depth)])
    grad_w_ada, ds_ctx = [], jnp.zeros((d_model,), F32)
    for l in range(depth):
        rows = jnp.concatenate([dmod_all[:, 2 * l], dmod_sum[2 * l + 1][None, :], jnp.zeros((7, 3 * d_model), F32)])
        mine = lax.dynamic_slice_in_dim(rows, me * w_ada_cols, w_ada_cols, axis=1)
        grad_w_ada.append(_mm(act.T, mine, name=f"ada{l}_dw"))
        ds_ctx = ds_ctx + _mm(mine, w_ada[l], nt=True, name=f"ada{l}_da")[8]
    grad_w_ada = jnp.stack(grad_w_ada)

    def scattered(g, axis, name):
        parts = g.reshape(g.shape[:axis] + (N_DEV, g.shape[axis] // N_DEV) + g.shape[axis + 1:])
        parts = jnp.moveaxis(parts, axis, 0)
        local = parts.shape[1:]
        return _reduce_scatter(parts.reshape(N_DEV, -1, local[-1]), name=name).reshape(local)

    w_in_slabs = _w_in_to_shards(jnp.stack([gl[l]["shadow_in"] for l in range(depth)]))
    grad_w_in = _reduce_scatter(w_in_slabs.reshape(N_DEV, depth * d_model, W_IN_SHARD), name="rs_w_in").reshape(w_in.shape)
    grad_w_out = scattered(jnp.stack([gl[l]["shadow_out"] for l in range(depth)]), 1, "rs_w_out")
    grad_w_glu = scattered(jnp.stack([gl[l]["shadow_glu"] for l in range(depth)]), 1, "rs_w_glu")

    stack = lambda k, shape=None: jnp.stack([gl[l][k] for l in range(depth)]).reshape(shape or given[k].shape)
    small = dict(norm_g=stack("norm_g"), hgrn_lb_logits=grads["lb_logits"], hgrn_norm_g=stack("hgrn_norm_g"),
                 gla_w_gk=stack("gla_w_gk", w_gk.shape), gla_b_gk=stack("gla_b_gk", b_gk.shape),
                 gla_norm_g=stack("gla_norm_g"), s5_d=stack("s5_d"), s5_b_glu=stack("s5_b_glu"),
                 ret_decay_logit=stack("ret_decay_logit"), ret_norm_g=stack("ret_norm_g"), final_norm_g=dg_final[0])
    for k in ("s5_lam_re", "s5_lam_im", "s5_log_dt", "s5_b_re", "s5_b_im", "s5_c_re", "s5_c_im"):
        small[k] = stack(k)
    pieces = [small[k] for k in _SMALL] + [ds_ctx]
    packed = jnp.concatenate([_pad_rows(a.reshape(-1), mult=1) for a in pieces], axis=0)
    packed = jnp.pad(packed, ((0, -packed.shape[0] % 8), (0, 0)))
    total = _sum8(_all_gather(packed, name="gather_small_grads"), name="sum_small_grads")
    reduced, off = {}, 0
    for k, a in zip(_SMALL + ("ds_ctx",), pieces):
        rows = -(-a.size // 128)
        reduced[k] = total[off:off + rows].reshape(-1)[:a.size].reshape(a.shape)
        off += rows

    def my_cols(a):
        w = a.shape[-1] // N_DEV
        return lax.dynamic_slice_in_dim(a, me * w, w, axis=a.ndim - 1)

    dsilu = jax.vjp(_silu, c_ctx)[1](reduced.pop("ds_ctx"))[0]
    grad = dict(reduced, c_ctx=dsilu, w_ada=grad_w_ada, b_ada=grad_b_ada, w_in=grad_w_in, w_out=grad_w_out,
                s5_w_glu=grad_w_glu)
    for k in ("hgrn_lb_logits", "gla_w_gk", "gla_b_gk"):
        grad[k] = my_cols(grad[k])

    delta, new_m, new_v = {}, {}, {}
    for k in _WEIGHTS:
        delta[k], new_m[k], new_v[k] = _adamw(given[k], grad[k], given["m_" + k], given["v_" + k], name="adamw_" + k)
    grad_x = grads["h0"][n_ctx:][None]
    return (loss, grad_x, *[grad[k] for k in _WEIGHTS], *[delta[k] for k in _WEIGHTS],
            *[new_m[k] for k in _WEIGHTS], *[new_v[k] for k in _WEIGHTS])
```

```python
import functools
import math

import jax
import jax.numpy as jnp
from jax import lax
from jax.experimental import pallas as pl
from jax.experimental.pallas import tpu as pltpu

F32 = jnp.float32
BF16 = jnp.bfloat16

EPS = 1e-6
BRANCH_W = 1024
CHUNK = 64
HEAD_DK = 128
A_MIN_FORGET = 1e-6
B_GATE_NORM = 16.0
B_GATE_RANK = 16
C_GROUP = 16
C_GROUPS = 64
C_STATE = 64
C_MAX_RE = -1e-4
GRID_W = 64
ROPE_BASE = 10000.0
N_DEV = 8

ADAM_LR = 0.001
ADAM_B1 = 0.9
ADAM_B2 = 0.999
ADAM_EPS = 1e-08
ADAM_WD = 0.01
ADAM_STEP = 10

VMEM_LIMIT = 52 * 1024 * 1024
SCAN_ROWS = 256
S5_ROWS = 128
S5_GB = 8
S5_GBW = S5_GB * C_STATE


def _tile(dim, cands):
    for c in cands:
        if dim % c == 0:
            return c
    return dim


def _cparams(sem):
    return pltpu.CompilerParams(dimension_semantics=sem, vmem_limit_bytes=VMEM_LIMIT)


MM_VMEM_BUDGET = 40 * 1024 * 1024


def _mm_tiles(M, N, K, sa, sb, so):
    tms = [t for t in (1088, 1024, 544, 512) if M % t == 0] or [M]
    tns = [t for t in (1024, 896, 640, 512, 384, 256) if N % t == 0] or [N]
    tks = [t for t in (K, 4480, 4352, 4096, 2688, 2176, 2048, 1920, 1024, 896, 640, 512, 384, 256, 128) if K % t == 0]
    fits = lambda tm, tn, tk: (2 * (tm * tk * sa + tk * tn * sb) + (tm * tn * 4 if tk < K else 0)
                               + 2 * tm * tn * so <= MM_VMEM_BUDGET)
    for tm in tms:
        for tn in tns:
            if fits(tm, tn, K):
                return tm, tn, K
    for tm in tms:
        for tn in tns:
            for tk in tks:
                if fits(tm, tn, tk):
                    return tm, tn, tk
    return tms[-1], tns[-1], tks[-1]


def _mm(a, b, *, nt=False, out_dtype=F32, name):
    M, K = a.shape
    N = b.shape[0] if nt else b.shape[1]
    tm, tn, tk = _mm_tiles(M, N, K, a.dtype.itemsize, b.dtype.itemsize, jnp.dtype(out_dtype).itemsize)
    nk = K // tk
    dims = (((1,), (1,)), ((), ())) if nt else (((1,), (0,)), ((), ()))

    if nk == 1:
        def body1(a_ref, b_ref, o_ref):
            o_ref[...] = lax.dot_general(a_ref[...].astype(BF16), b_ref[...].astype(BF16), dims,
                                         preferred_element_type=F32).astype(o_ref.dtype)

        b_spec1 = (pl.BlockSpec((tn, K), lambda i, j: (j, 0)) if nt else pl.BlockSpec((K, tn), lambda i, j: (0, j)))
        return pl.pallas_call(
            body1, name=name, grid=(M // tm, N // tn),
            in_specs=[pl.BlockSpec((tm, K), lambda i, j: (i, 0)), b_spec1],
            out_specs=pl.BlockSpec((tm, tn), lambda i, j: (i, j)),
            out_shape=jax.ShapeDtypeStruct((M, N), out_dtype),
            compiler_params=_cparams(("parallel", "arbitrary")),
        )(a, b)

    def body(a_ref, b_ref, o_ref, acc_ref):
        k = pl.program_id(2)

        @pl.when(k == 0)
        def _():
            acc_ref[...] = jnp.zeros_like(acc_ref)

        acc_ref[...] += lax.dot_general(a_ref[...].astype(BF16), b_ref[...].astype(BF16), dims,
                                        preferred_element_type=F32)

        @pl.when(k == nk - 1)
        def _():
            o_ref[...] = acc_ref[...].astype(o_ref.dtype)

    b_spec = (pl.BlockSpec((tn, tk), lambda i, j, k: (j, k)) if nt
              else pl.BlockSpec((tk, tn), lambda i, j, k: (k, j)))
    return pl.pallas_call(
        body, name=name, grid=(M // tm, N // tn, nk),
        in_specs=[pl.BlockSpec((tm, tk), lambda i, j, k: (i, k)), b_spec],
        out_specs=pl.BlockSpec((tm, tn), lambda i, j, k: (i, j)),
        out_shape=jax.ShapeDtypeStruct((M, N), out_dtype),
        scratch_shapes=[pltpu.VMEM((tm, tn), F32)],
        compiler_params=_cparams(("parallel", "parallel", "arbitrary")),
    )(a, b)


def _linear(name):
    @jax.custom_vjp
    def op(a, w, shadow):
        return _mm(a, w, name=name + "_fwd")

    def fwd(a, w, shadow):
        return _mm(a, w, name=name + "_fwd"), (a, w)

    def bwd(res, dy):
        a, w = res
        dy = dy.astype(BF16)
        da = _mm(dy, w, nt=True, out_dtype=a.dtype, name=name + "_da")
        dw = _mm(a.T, dy, name=name + "_dw")
        return da, jnp.zeros_like(w), dw

    op.defvjp(fwd, bwd)
    return op


def _mm_gather(a, b, sides, *, name):
    M, K = a.shape
    N = b.shape[1]
    tm, tn, tk = _mm_tiles(M, N, K, a.dtype.itemsize, b.dtype.itemsize, 4)
    assert tk == K
    gi, gj = M // tm, N // tn
    last = gi * gj - 1
    mid = last // 2
    ns = len(sides)

    def body(a_ref, b_ref, *refs):
        side_refs, o_ref, got_refs = refs[:ns], refs[ns], refs[ns + 1:2 * ns + 1]
        send_sems, recv_sems, local_sems = refs[2 * ns + 1:]
        step = pl.program_id(0) * gj + pl.program_id(1)
        x_, y_, c_ = _place()
        me, sibling = (x_, y_, c_), (x_, y_, 1 - c_)
        chips = [(1 - x_, y_), (x_, 1 - y_), (1 - x_, 1 - y_)]

        def gather(s):
            side_ref, got_ref = side_refs[s], got_refs[s]

            def slab(px, py, pc):
                return got_ref.at[4 * px + 2 * py + pc]

            def copy(k, block, to, src=None):
                return pltpu.make_async_remote_copy(
                    src_ref=slab(*block) if src is None else src, dst_ref=slab(*block), send_sem=send_sems.at[7 * s + k],
                    recv_sem=recv_sems.at[7 * s + k], device_id=to, device_id_type=MESH)

            mine = pltpu.make_async_copy(side_ref, slab(*me), local_sems.at[s])
            first = [copy(0, me, sibling, src=side_ref)] + [copy(1 + j, me, (*chip, c_), src=side_ref)
                                                             for j, chip in enumerate(chips)]
            passed = [copy(4 + j, (*chip, c_), sibling) for j, chip in enumerate(chips)]
            return copy, mine, first, passed

        parts = [gather(s) for s in range(ns)]

        @pl.when(step == 0)
        def _():
            for copy, mine, first, passed in parts:
                mine.start()
                for cp in first:
                    cp.start()

        o_ref[...] = jnp.dot(a_ref[...].astype(BF16), b_ref[...].astype(BF16), preferred_element_type=F32)

        @pl.when(step == mid)
        def _():
            for copy, mine, first, passed in parts:
                for j, chip in enumerate(chips):
                    copy(1 + j, (*chip, c_), me).wait_recv()
                    passed[j].start()

        @pl.when(step == last)
        def _():
            for copy, mine, first, passed in parts:
                copy(0, sibling, me).wait_recv()
                for j, chip in enumerate(chips):
                    copy(4 + j, (*chip, 1 - c_), me).wait_recv()
                for cp in first + passed:
                    cp.wait_send()
                mine.wait()

    return pl.pallas_call(
        body, name=name, grid=(gi, gj),
        in_specs=[pl.BlockSpec((tm, K), lambda i, j: (i, 0)), pl.BlockSpec((K, tn), lambda i, j: (0, j))] + [ANY] * ns,
        out_specs=[pl.BlockSpec((tm, tn), lambda i, j: (i, j))] + [ANY] * ns,
        out_shape=[jax.ShapeDtypeStruct((M, N), F32)]
        + [jax.ShapeDtypeStruct((N_DEV,) + s.shape, s.dtype) for s in sides],
        scratch_shapes=[pltpu.SemaphoreType.DMA((7 * ns,)), pltpu.SemaphoreType.DMA((7 * ns,)),
                        pltpu.SemaphoreType.DMA((ns,))],
        compiler_params=_cparams(("arbitrary", "arbitrary")),
    )(a, b, *sides)


def _linear_gather(name):
    @jax.custom_vjp
    def op(a, w, shadow, sides):
        return tuple(_mm_gather(a, w, sides, name=name + "_fwd"))

    def fwd(a, w, shadow, sides):
        return tuple(_mm_gather(a, w, sides, name=name + "_fwd")), (a, w, sides)

    def bwd(res, cts):
        a, w, sides = res
        dy = cts[0].astype(BF16)
        da = _mm(dy, w, nt=True, out_dtype=a.dtype, name=name + "_da")
        dw = _mm(a.T, dy, name=name + "_dw")
        return da, jnp.zeros_like(w), dw, tuple(jnp.zeros_like(s) for s in sides)

    op.defvjp(fwd, bwd)
    return op


def _rowwise(fn, specs, out_defs, *, ncb, tr, n_ctx, name):
    ncx = n_ctx // tr
    diff = [k for k, s in enumerate(specs) if s in ("row", "const", "col", "seg")]

    def in_spec(s, arr):
        if s == "row":
            return pl.BlockSpec((tr, arr.shape[1] // ncb), lambda j, i: (i, j))
        if s == "rowc":
            return pl.BlockSpec((tr, arr.shape[1]), lambda j, i: (i, 0))
        if s in ("const", "fixed"):
            return pl.BlockSpec(arr.shape, lambda j, i: (0, 0))
        if s == "col":
            return pl.BlockSpec((None,) + arr.shape[1:], lambda j, i: (j, 0, 0))
        if s == "seg":
            return pl.BlockSpec((None,) + arr.shape[1:], lambda j, i: ((i >= ncx).astype(jnp.int32), 0, 0))
        raise ValueError(s)

    def fwd_call(*arrays):
        L = arrays[specs.index("row")].shape[0]
        n_in = len(arrays)

        def body(*refs):
            outs = fn(*[r[...] for r in refs[:n_in]])
            for o_ref, o in zip(refs[n_in:], outs):
                o_ref[...] = o.astype(o_ref.dtype)

        return pl.pallas_call(
            body, name=name + "_fwd", grid=(ncb, L // tr),
            in_specs=[in_spec(s, a) for s, a in zip(specs, arrays)],
            out_specs=[pl.BlockSpec((tr, c // ncb), lambda j, i: (i, j)) for c, _ in out_defs],
            out_shape=[jax.ShapeDtypeStruct((L, c), dt) for c, dt in out_defs],
            compiler_params=_cparams(("parallel", "arbitrary")),
        )(*arrays)

    def bwd_call(arrays, cts):
        L = arrays[specs.index("row")].shape[0]
        n_in, n_ct = len(arrays), len(cts)

        def body(*refs):
            i = pl.program_id(1)
            vals = [r[...] for r in refs[:n_in]]
            ct_vals = tuple(r[...] for r in refs[n_in:n_in + n_ct])
            g_refs = refs[n_in + n_ct:]

            def f(*dv):
                full = list(vals)
                for k, v in zip(diff, dv):
                    full[k] = v
                return tuple(o.astype(dt) for o, (_, dt) in zip(fn(*full), out_defs))

            _, vjp = jax.vjp(f, *[vals[k] for k in diff])
            grads = vjp(ct_vals)
            for k, g_ref, g in zip(diff, g_refs, grads):
                if specs[k] == "row":
                    g_ref[...] = g.astype(g_ref.dtype)
                else:
                    first = (i == 0) | (i == ncx) if specs[k] == "seg" else (i == 0)

                    @pl.when(first)
                    def _(g_ref=g_ref, g=g):
                        g_ref[...] = g

                    @pl.when(jnp.logical_not(first))
                    def _(g_ref=g_ref, g=g):
                        g_ref[...] += g

        out_specs, out_shape = [], []
        for k in diff:
            s, a = specs[k], arrays[k]
            if s == "row":
                out_specs.append(pl.BlockSpec((tr, a.shape[1] // ncb), lambda j, i: (i, j)))
                out_shape.append(jax.ShapeDtypeStruct(a.shape, a.dtype))
            elif s == "const":
                out_specs.append(pl.BlockSpec((None,) + a.shape, lambda j, i: (j, 0, 0)))
                out_shape.append(jax.ShapeDtypeStruct((ncb,) + a.shape, F32))
            elif s == "col":
                out_specs.append(pl.BlockSpec((None,) + a.shape[1:], lambda j, i: (j, 0, 0)))
                out_shape.append(jax.ShapeDtypeStruct(a.shape, F32))
            else:
                out_specs.append(pl.BlockSpec((None, None) + a.shape[1:],
                                              lambda j, i: (j, (i >= ncx).astype(jnp.int32), 0, 0)))
                out_shape.append(jax.ShapeDtypeStruct((ncb,) + a.shape, F32))
        ct_specs = [pl.BlockSpec((tr, c // ncb), lambda j, i: (i, j)) for c, _ in out_defs]
        grads = pl.pallas_call(
            body, name=name + "_bwd", grid=(ncb, L // tr),
            in_specs=[in_spec(s, a) for s, a in zip(specs, arrays)] + ct_specs,
            out_specs=out_specs, out_shape=out_shape,
            compiler_params=_cparams(("parallel", "arbitrary")),
        )(*arrays, *cts)
        full = [jnp.zeros_like(a) for a in arrays]
        for k, g in zip(diff, grads):
            full[k] = jnp.sum(g, axis=0) if specs[k] in ("const", "seg") else g
        return tuple(full)

    @jax.custom_vjp
    def op(*arrays):
        return tuple(fwd_call(*arrays))

    def op_fwd(*arrays):
        return tuple(fwd_call(*arrays)), arrays

    def op_bwd(arrays, cts):
        return bwd_call(arrays, cts)

    op.defvjp(op_fwd, op_bwd)
    return op


def _sigmoid(x):
    return 1.0 / (1.0 + jnp.exp(-x))


def _log_sigmoid(x):
    return jnp.minimum(x, 0.0) - jnp.log(1.0 + jnp.exp(-jnp.abs(x)))


def _silu(x):
    return x * _sigmoid(x)


def _rms(x, g):
    return x * lax.rsqrt(jnp.mean(x * x, axis=-1, keepdims=True) + EPS) * g


def _modnorm_fn(h, g, sc, sh):
    return ((_rms(h, g) * (1.0 + sc) + sh).astype(BF16),)


def _resid_fn(h, y, gt):
    return (h + gt * y,)


def _hgrn_pro_fn(zf, zb, lbf, lbb):
    def one(z, lb):
        f = lb + (1.0 - lb) * _sigmoid(z)
        return (1.0 - lb) * _sigmoid(-z), jnp.log(jnp.maximum(f, A_MIN_FORGET))

    kf, gf = one(zf, lbf)
    kb, gb = one(zb, lbb)
    return kf, gf, kb, gb


def _hgrn_epi_fn(of, ob, gate, g):
    return ((_rms(of + ob, g) * _silu(gate)).astype(BF16),)


def _gla_pro_fn(q, lr, wf, wb, bf, bb):
    gf = _log_sigmoid(jnp.dot(lr, wf, preferred_element_type=F32) + bf) / B_GATE_NORM
    gb = _log_sigmoid(jnp.dot(lr, wb, preferred_element_type=F32) + bb) / B_GATE_NORM
    return q * (HEAD_DK ** -0.5), gf, gb


def _ret_pro_fn(q, k, cos2, sin2, lgf, lgb, swap):
    def rope(x):
        return x * cos2 + jnp.dot(x, swap, preferred_element_type=F32, precision=lax.Precision.HIGHEST) * sin2

    zero = jnp.zeros_like(q)
    return rope(q) * (HEAD_DK ** -0.5), rope(k), zero + lgf, zero + lgb


def _ret_epi_fn(of, ob, gate, g):
    x = of + ob
    mu = jnp.mean(x, axis=-1, keepdims=True)
    var = jnp.mean(jnp.square(x - mu), axis=-1, keepdims=True)
    return (((x - mu) * lax.rsqrt(var + EPS) * g * _silu(gate)).astype(BF16),)


def _s5_mid_fn(yf, yb, u, d):
    return (jax.nn.gelu(yf + yb + d * u),)


def _s5_epi_fn(z, t, gate, b):
    return ((z * _sigmoid(t + b) * _silu(gate)).astype(BF16),)


def _blk(mode, i, nb, nc):
    if mode == 0:
        return i
    if mode == 1:
        return nb - 1 - i
    if mode == 2:
        return jnp.where(i < nc, nc - 1 - i, nb - 1 - i + nc)
    return jnp.where(i < nb - nc, i + nc, nc - nb + i)


def _split3(x):
    hi = x.astype(BF16)
    r1 = x - hi.astype(F32)
    mid = r1.astype(BF16)
    lo = (r1 - mid.astype(F32)).astype(BF16)
    return hi, mid, lo


def _cum(tri, x):
    w = x.shape[1]
    hi = x.astype(BF16)
    lo = (x - hi.astype(F32)).astype(BF16)
    r = jnp.dot(tri, jnp.concatenate([hi, lo], axis=1), preferred_element_type=F32)
    return r[:, :w] + r[:, w:]


def _dot(a, b):
    return jnp.dot(a.astype(BF16), b.astype(BF16), preferred_element_type=F32)


def _dot_nt(a, b):
    return lax.dot_general(a.astype(BF16), b.astype(BF16), (((1,), (1,)), ((), ())), preferred_element_type=F32)


def _dot_tn(a, b):
    return jnp.dot(a.astype(F32).T.astype(BF16), b.astype(BF16), preferred_element_type=F32)


def _dot3(a, b):
    ah = a.astype(BF16)
    al = (a - ah.astype(F32)).astype(BF16)
    bh = b.astype(BF16)
    bl = (b - bh.astype(F32)).astype(BF16)
    d = lambda p, q: jnp.dot(p, q, preferred_element_type=F32)
    return d(ah, bh) + d(ah, bl) + d(al, bh)


def _dot3_tn(a, b):
    return _dot3(a.astype(F32).T, b)


def _gla_masks(flip):
    r = lax.broadcasted_iota(jnp.int32, (CHUNK, CHUNK), 0)
    c = lax.broadcasted_iota(jnp.int32, (CHUNK, CHUNK), 1)
    causal = (c >= r) if flip else (c <= r)
    tri = causal.astype(BF16)
    tri_t = jnp.logical_not(causal) | (r == c)
    return causal, tri, tri_t.astype(BF16)


def _gla_chunk_terms(q, k, g, tri, flip):
    b = _cum(tri, g)
    mid = CHUNK // 2
    bm = b[mid:mid + 1, :]
    bl = b[0:1, :] if flip else b[CHUNK - 1:CHUNK, :]
    e_q = jnp.exp(b - bm)
    e_k = jnp.exp(bm - b)
    return b, bl, e_q, e_k, q * e_q, k * e_k


def _gla_fwd_call(q, k, v, g, *, flip, n_ctx, name):
    L = q.shape[0]
    H = q.shape[1] // HEAD_DK
    dv = v.shape[1] // H
    nb, nc, sub = L // SCAN_ROWS, n_ctx // SCAN_ROWS, SCAN_ROWS // CHUNK
    mode = 2 if flip else 0
    order = list(range(sub))[::-1] if flip else list(range(sub))

    def body(q_ref, k_ref, v_ref, g_ref, o_ref, sp_ref, st_ref):
        @pl.when(pl.program_id(1) == 0)
        def _():
            st_ref[...] = jnp.zeros_like(st_ref)

        causal, tri, _ = _gla_masks(flip)
        for s in order:
            sl = pl.ds(s * CHUNK, CHUNK)
            qc, kc, vc, gc = q_ref[sl, :], k_ref[sl, :], v_ref[sl, :], g_ref[sl, :]
            b, bl, _, _, qt, kt = _gla_chunk_terms(qc, kc, gc, tri, flip)
            a = jnp.where(causal, _dot_nt(qt, kt), 0.0)
            sp = st_ref[...]
            sp_ref[s] = sp
            o_ref[sl, :] = _dot(a, vc) + _dot_nt(qc * jnp.exp(b), sp)
            st_ref[...] = sp * jnp.exp(bl) + _dot3_tn(vc, kc * jnp.exp(bl - b))

    row = lambda w: pl.BlockSpec((SCAN_ROWS, w), lambda h, i: (_blk(mode, i, nb, nc), h))
    return pl.pallas_call(
        body, name=name, grid=(H, nb),
        in_specs=[row(HEAD_DK), row(HEAD_DK), row(dv), row(HEAD_DK)],
        out_specs=[row(dv), pl.BlockSpec((None, None, sub, dv, HEAD_DK),
                                         lambda h, i: (h, _blk(mode, i, nb, nc), 0, 0, 0))],
        out_shape=[jax.ShapeDtypeStruct((L, H * dv), F32),
                   jax.ShapeDtypeStruct((H, nb, sub, dv, HEAD_DK), F32)],
        scratch_shapes=[pltpu.VMEM((dv, HEAD_DK), F32)],
        compiler_params=_cparams(("parallel", "arbitrary")),
    )(q, k, v, g)


def _gla_bwd_call(q, k, v, g, sp, do, *, flip, n_ctx, name):
    L = q.shape[0]
    H = q.shape[1] // HEAD_DK
    dv = v.shape[1] // H
    nb, nc, sub = L // SCAN_ROWS, n_ctx // SCAN_ROWS, SCAN_ROWS // CHUNK
    mode = 3 if flip else 1
    order = list(range(sub)) if flip else list(range(sub))[::-1]
    last = 0 if flip else CHUNK - 1

    def body(q_ref, k_ref, v_ref, g_ref, sp_ref, do_ref, dq_ref, dk_ref, dv_ref, dg_ref, ds_ref):
        @pl.when(pl.program_id(1) == 0)
        def _():
            ds_ref[...] = jnp.zeros_like(ds_ref)

        causal, tri, tri_t = _gla_masks(flip)
        is_last = lax.broadcasted_iota(jnp.int32, (CHUNK, HEAD_DK), 0) == last
        for s in order:
            sl = pl.ds(s * CHUNK, CHUNK)
            qc, kc, vc, gc, doc = q_ref[sl, :], k_ref[sl, :], v_ref[sl, :], g_ref[sl, :], do_ref[sl, :]
            b, bl, e_q, e_k, qt, kt = _gla_chunk_terms(qc, kc, gc, tri, flip)
            e_b, e_l = jnp.exp(b), jnp.exp(bl - b)
            a = jnp.where(causal, _dot_nt(qt, kt), 0.0)
            p, kd = qc * e_b, kc * e_l
            spv, ds = sp_ref[s], ds_ref[...]
            dv_ref[sl, :] = _dot_tn(a, doc) + _dot_nt(kd, ds)
            da = jnp.where(causal, _dot_nt(doc, vc), 0.0)
            dkd = _dot3(vc, ds)
            dq = _dot3(da, kt) * e_q + _dot3(doc, spv) * e_b
            dk = _dot3_tn(da, qt) * e_k + dkd * e_l
            dq_ref[sl, :] = dq
            dk_ref[sl, :] = dk
            db_last = (jnp.sum(dkd * kd, axis=0, keepdims=True)
                       + jnp.exp(bl) * jnp.sum(ds * spv, axis=0, keepdims=True))
            db = qc * dq - kc * dk + jnp.where(is_last, db_last, 0.0)
            dg_ref[sl, :] = _cum(tri_t, db)
            ds_ref[...] = ds * jnp.exp(bl) + _dot3_tn(doc, p)

    row = lambda w: pl.BlockSpec((SCAN_ROWS, w), lambda h, i: (_blk(mode, i, nb, nc), h))
    return pl.pallas_call(
        body, name=name, grid=(H, nb),
        in_specs=[row(HEAD_DK), row(HEAD_DK), row(dv), row(HEAD_DK),
                  pl.BlockSpec((None, None, sub, dv, HEAD_DK), lambda h, i: (h, _blk(mode, i, nb, nc), 0, 0, 0)),
                  row(dv)],
        out_specs=[row(HEAD_DK), row(HEAD_DK), row(dv), row(HEAD_DK)],
        out_shape=[jax.ShapeDtypeStruct(q.shape, F32), jax.ShapeDtypeStruct(k.shape, F32),
                   jax.ShapeDtypeStruct(v.shape, F32), jax.ShapeDtypeStruct(g.shape, F32)],
        scratch_shapes=[pltpu.VMEM((dv, HEAD_DK), F32)],
        compiler_params=_cparams(("parallel", "arbitrary")),
    )(q, k, v, g, sp, do)


def _gla(name, *, flip, n_ctx):
    @jax.custom_vjp
    def op(q, k, v, g):
        return _gla_fwd_call(q, k, v, g, flip=flip, n_ctx=n_ctx, name=name + "_fwd")[0]

    def fwd(q, k, v, g):
        o, sp = _gla_fwd_call(q, k, v, g, flip=flip, n_ctx=n_ctx, name=name + "_fwd")
        return o, (q, k, v, g, sp)

    def bwd(res, do):
        return tuple(_gla_bwd_call(*res, do, flip=flip, n_ctx=n_ctx, name=name + "_bwd"))

    op.defvjp(fwd, bwd)
    return op


def _s5_table(ar, ai, desc):
    def mul(p, q):
        return p[0] * q[0] - p[1] * q[1], p[0] * q[1] + p[1] * q[0]

    a1 = (ar.reshape(-1), ai.reshape(-1))
    ar = a1[0]
    a2 = mul(a1, a1)
    a4 = mul(a2, a2)
    pw = [a1]
    for _ in range(7):
        pw.append(mul(pw[-1], a1))
    if desc:
        pw = pw[::-1]
    zero = jnp.zeros_like(ar)
    rows = [a1[0], a1[1], a2[0], a2[1], a4[0], a4[1], zero, zero] + [p[0] for p in pw] + [p[1] for p in pw]
    return jnp.stack(rows, axis=0)


S5_LANES = C_GROUPS * C_STATE
S5_LC = 1024


def _s5_scan(xr_ref, xi_ref, tab_ref, car_ref, desc, on_slab=None):
    nslab = S5_ROWS // 8
    rows = lax.broadcasted_iota(jnp.int32, (8, S5_LC), 0)

    def slab(t, carry):
        r0 = pl.multiple_of(((nslab - 1 - t) if desc else t) * 8, 8)
        for lc in range(S5_LANES // S5_LC):
            cs = pl.ds(lc * S5_LC, S5_LC)
            xr, xi = xr_ref[pl.ds(r0, 8), cs], xi_ref[pl.ds(r0, 8), cs]
            for n, s in enumerate((1, 2, 4)):
                ar, ai = tab_ref[2 * n:2 * n + 1, cs], tab_ref[2 * n + 1:2 * n + 2, cs]
                keep = (rows < 8 - s) if desc else (rows >= s)
                sr = jnp.where(keep, pltpu.roll(xr, (8 - s) if desc else s, 0), 0.0)
                si = jnp.where(keep, pltpu.roll(xi, (8 - s) if desc else s, 0), 0.0)
                xr, xi = xr + ar * sr - ai * si, xi + ar * si + ai * sr
            cr, ci = car_ref[0:1, cs], car_ref[1:2, cs]
            pr, pi = tab_ref[8:16, cs], tab_ref[16:24, cs]
            xr, xi = xr + pr * cr - pi * ci, xi + pr * ci + pi * cr
            if on_slab is not None:
                on_slab(r0, cs, xr, xi, cr, ci, rows)
            xr_ref[pl.ds(r0, 8), cs] = xr
            xi_ref[pl.ds(r0, 8), cs] = xi
            e = 0 if desc else 7
            car_ref[0:1, cs] = xr[e:e + 1, :]
            car_ref[1:2, cs] = xi[e:e + 1, :]
        return carry

    lax.fori_loop(0, nslab, slab, 0)


def _s5_fwd_call(u, tab, wb_re, wb_im, wc_re, wc_im, *, flip, n_ctx, name):
    L = u.shape[0]
    nb, nc = L // S5_ROWS, n_ctx // S5_ROWS
    mode, desc = (2, True) if flip else (0, False)

    def body(u_ref, tab_ref, wbr_ref, wbi_ref, wcr_ref, wci_ref, y_ref, xr_ref, xi_ref, car_ref):
        @pl.when(pl.program_id(0) == 0)
        def _():
            car_ref[...] = jnp.zeros_like(car_ref)

        for gb in range(S5_GB):
            us, cs = pl.ds(gb * 128, 128), pl.ds(gb * S5_GBW, S5_GBW)
            ub = u_ref[:, us]
            xr_ref[:, cs] = _dot(ub, wbr_ref[gb])
            xi_ref[:, cs] = _dot(ub, wbi_ref[gb])
        _s5_scan(xr_ref, xi_ref, tab_ref, car_ref, desc)
        for gb in range(S5_GB):
            us, cs = pl.ds(gb * 128, 128), pl.ds(gb * S5_GBW, S5_GBW)
            y_ref[:, us] = _dot(xr_ref[:, cs], wcr_ref[gb]) - _dot(xi_ref[:, cs], wci_ref[gb])

    row = lambda w: pl.BlockSpec((S5_ROWS, w), lambda i: (_blk(mode, i, nb, nc), 0))
    full = lambda a: pl.BlockSpec(a.shape, lambda i: (0,) * a.ndim)
    return pl.pallas_call(
        body, name=name, grid=(nb,),
        in_specs=[row(BRANCH_W), full(tab), full(wb_re), full(wb_im), full(wc_re), full(wc_im)],
        out_specs=[row(BRANCH_W), row(S5_LANES), row(S5_LANES)],
        out_shape=[jax.ShapeDtypeStruct((L, BRANCH_W), F32), jax.ShapeDtypeStruct((L, S5_LANES), F32),
                   jax.ShapeDtypeStruct((L, S5_LANES), F32)],
        scratch_shapes=[pltpu.VMEM((8, S5_LANES), F32)],
        compiler_params=_cparams(("arbitrary",)),
    )(u, tab, wb_re, wb_im, wc_re, wc_im)


def _s5_bwd_call(dy, u, xr, xi, tab, wbt_re, wbt_im, wct_re, wct_im, *, flip, n_ctx, name):
    L = u.shape[0]
    nb, nc = L // S5_ROWS, n_ctx // S5_ROWS
    mode, desc = (3, False) if flip else (1, True)

    def body(dy_ref, u_ref, xr_ref, xi_ref, tab_ref, wbtr_ref, wbti_ref, wctr_ref, wcti_ref,
             du_ref, dwbr_ref, dwbi_ref, dwcr_ref, dwci_ref, ga_ref, gr_ref, gi_ref, car_ref):
        @pl.when(pl.program_id(0) == 0)
        def _():
            car_ref[...] = jnp.zeros_like(car_ref)
            ga_ref[...] = jnp.zeros_like(ga_ref)
            for r in (dwbr_ref, dwbi_ref, dwcr_ref, dwci_ref):
                r[...] = jnp.zeros_like(r)

        for gb in range(S5_GB):
            us, cs = pl.ds(gb * 128, 128), pl.ds(gb * S5_GBW, S5_GBW)
            dyb = dy_ref[:, us]
            gr_ref[:, cs] = _dot(dyb, wctr_ref[gb])
            gi_ref[:, cs] = -_dot(dyb, wcti_ref[gb])

        def lam_grad(r0, cs, gr, gi, cr, ci, rows):
            edge = 7 if desc else 0
            pr = jnp.where(rows == edge, cr, pltpu.roll(gr, 7 if desc else 1, 0))
            pi = jnp.where(rows == edge, ci, pltpu.roll(gi, 7 if desc else 1, 0))
            x_r, x_i = xr_ref[pl.ds(r0, 8), cs], xi_ref[pl.ds(r0, 8), cs]
            ga_ref[0:8, cs] += x_r * pr + x_i * pi
            ga_ref[8:16, cs] += x_r * pi - x_i * pr

        _s5_scan(gr_ref, gi_ref, tab_ref, car_ref, desc, on_slab=lam_grad)
        for gb in range(S5_GB):
            us, cs = pl.ds(gb * 128, 128), pl.ds(gb * S5_GBW, S5_GBW)
            g_r, g_i, ub, dyb = gr_ref[:, cs], gi_ref[:, cs], u_ref[:, us], dy_ref[:, us]
            du_ref[:, us] = _dot(g_r, wbtr_ref[gb]) + _dot(g_i, wbti_ref[gb])
            dwbr_ref[gb] += _dot_tn(ub, g_r)
            dwbi_ref[gb] += _dot_tn(ub, g_i)
            dwcr_ref[gb] += _dot_tn(xr_ref[:, cs], dyb)
            dwci_ref[gb] -= _dot_tn(xi_ref[:, cs], dyb)

    row = lambda w: pl.BlockSpec((S5_ROWS, w), lambda i: (_blk(mode, i, nb, nc), 0))
    full = lambda shape: pl.BlockSpec(shape, lambda i: (0,) * len(shape))
    wb_shape, wc_shape = (S5_GB, 128, S5_GBW), (S5_GB, S5_GBW, 128)
    return pl.pallas_call(
        body, name=name, grid=(nb,),
        in_specs=[row(BRANCH_W), row(BRANCH_W), row(S5_LANES), row(S5_LANES), full(tab.shape),
                  full(wc_shape), full(wc_shape), full(wb_shape), full(wb_shape)],
        out_specs=[row(BRANCH_W), full(wb_shape), full(wb_shape), full(wc_shape), full(wc_shape),
                   full((16, S5_LANES))],
        out_shape=[jax.ShapeDtypeStruct((L, BRANCH_W), F32),
                   jax.ShapeDtypeStruct(wb_shape, F32), jax.ShapeDtypeStruct(wb_shape, F32),
                   jax.ShapeDtypeStruct(wc_shape, F32), jax.ShapeDtypeStruct(wc_shape, F32),
                   jax.ShapeDtypeStruct((16, S5_LANES), F32)],
        scratch_shapes=[pltpu.VMEM((S5_ROWS, S5_LANES), F32), pltpu.VMEM((S5_ROWS, S5_LANES), F32),
                        pltpu.VMEM((8, S5_LANES), F32)],
        compiler_params=_cparams(("arbitrary",)),
    )(dy, u, xr, xi, tab, wbt_re, wbt_im, wct_re, wct_im)


def _s5(name, *, flip, n_ctx):
    desc_f = flip
    desc_b = not flip

    def run_fwd(u, lr, li, wbr, wbi, wcr, wci):
        tab = _s5_table(lr, li, desc_f)
        c16 = lambda a: a.astype(BF16)
        return _s5_fwd_call(u, tab, c16(wbr), c16(wbi), c16(wcr), c16(wci), flip=flip, n_ctx=n_ctx, name=name + "_fwd")

    @jax.custom_vjp
    def op(u, lr, li, wbr, wbi, wcr, wci):
        return run_fwd(u, lr, li, wbr, wbi, wcr, wci)[0]

    def fwd(u, lr, li, wbr, wbi, wcr, wci):
        y, xr, xi = run_fwd(u, lr, li, wbr, wbi, wcr, wci)
        return y, (u, xr, xi, lr, li, wbr, wbi, wcr, wci)

    def bwd(res, dy):
        u, xr, xi, lr, li, wbr, wbi, wcr, wci = res
        tab = _s5_table(lr, -li, desc_b)
        t16 = lambda a: jnp.swapaxes(a, 1, 2).astype(BF16)
        du, dwbr, dwbi, dwcr, dwci, ga = _s5_bwd_call(
            dy, u, xr, xi, tab, t16(wbr), t16(wbi), t16(wcr), t16(wci), flip=flip, n_ctx=n_ctx, name=name + "_bwd")
        return (du, jnp.sum(ga[0:8], axis=0, keepdims=True), jnp.sum(ga[8:16], axis=0, keepdims=True),
                dwbr, dwbi, dwcr, dwci)

    op.defvjp(fwd, bwd)
    return op


def _loss_call(h, tgt, g, *, n_ctx, name):
    L, D = h.shape
    tr = 128
    ncx = n_ctx // tr

    def body(h_ref, t_ref, g_ref, dh_ref, dg_ref, loss_ref):
        i = pl.program_id(0)

        @pl.when(i == 0)
        def _():
            dg_ref[...] = jnp.zeros_like(dg_ref)
            loss_ref[...] = jnp.zeros_like(loss_ref)

        @pl.when(i < ncx)
        def _():
            dh_ref[...] = jnp.zeros_like(dh_ref)

        @pl.when(i >= ncx)
        def _():
            t = t_ref[...]

            def f(hv, gv):
                e = _rms(hv, gv) - t
                return 0.5 * jnp.sum(e * e) / D

            l, (dh, dg) = jax.value_and_grad(f, argnums=(0, 1))(h_ref[...], g_ref[...])
            dh_ref[...] = dh
            dg_ref[...] += dg
            loss_ref[...] += l

    return pl.pallas_call(
        body, name=name, grid=(L // tr,),
        in_specs=[pl.BlockSpec((tr, D), lambda i: (i, 0)),
                  pl.BlockSpec((tr, D), lambda i: (jnp.maximum(i - ncx, 0), 0)),
                  pl.BlockSpec((1, D), lambda i: (0, 0))],
        out_specs=[pl.BlockSpec((tr, D), lambda i: (i, 0)), pl.BlockSpec((1, D), lambda i: (0, 0)),
                   pl.BlockSpec((8, 128), lambda i: (0, 0))],
        out_shape=[jax.ShapeDtypeStruct((L, D), F32), jax.ShapeDtypeStruct((1, D), F32),
                   jax.ShapeDtypeStruct((8, 128), F32)],
        compiler_params=_cparams(("arbitrary",)),
    )(h, tgt, g)


def _adamw_call(w, g, m, v, *, name):
    R, C = w.shape
    tr = R if R * C * 4 <= (1 << 20) else _tile(R, tuple(t for t in (512, 256, 128, 64, 32, 16, 8) if t * C * 4 <= (1 << 20)))
    c1 = 1.0 / (1.0 - ADAM_B1 ** ADAM_STEP)
    c2 = 1.0 / (1.0 - ADAM_B2 ** ADAM_STEP)

    def body(w_ref, g_ref, m_ref, v_ref, d_ref, nm_ref, nv_ref):
        gv = g_ref[...]
        nm = ADAM_B1 * m_ref[...] + (1.0 - ADAM_B1) * gv
        nv = ADAM_B2 * v_ref[...] + (1.0 - ADAM_B2) * gv * gv
        d_ref[...] = -ADAM_LR * ((nm * c1) / (jnp.sqrt(nv * c2) + ADAM_EPS) + ADAM_WD * w_ref[...])
        nm_ref[...] = nm
        nv_ref[...] = nv

    spec = pl.BlockSpec((tr, C), lambda i: (i, 0))
    return pl.pallas_call(
        body, name=name, grid=(R // tr,), in_specs=[spec] * 4, out_specs=[spec] * 3,
        out_shape=[jax.ShapeDtypeStruct((R, C), F32)] * 3,
        compiler_params=_cparams(("parallel",)),
    )(w, g, m, v)


def _as2d(a):
    n = a.size
    if a.ndim >= 2 and n >= (1 << 20):
        return a.reshape(-1, a.shape[-1])
    if n % 128 == 0:
        return a.reshape(n // 128, 128)
    return a.reshape(1, n)


def _adamw(w, g, m, v, *, name):
    d, nm, nv = _adamw_call(_as2d(w), _as2d(g.astype(F32)), _as2d(m), _as2d(v), name=name)
    return d.reshape(w.shape), nm.reshape(w.shape), nv.reshape(w.shape)


MESH = pl.DeviceIdType.MESH
ANY = pl.BlockSpec(memory_space=pl.ANY)


def _place():
    return lax.axis_index("x"), lax.axis_index("y"), lax.axis_index("c")


def _all_gather(x, *, name):
    M, N = x.shape

    def body(x_ref, out_ref, send_sems, recv_sems, local_sem):
        x_, y_, c_ = _place()
        me, sibling = (x_, y_, c_), (x_, y_, 1 - c_)
        chips = [(1 - x_, y_), (x_, 1 - y_), (1 - x_, 1 - y_)]

        def slab(px, py, pc):
            return out_ref.at[4 * px + 2 * py + pc]

        def copy(k, block, to, src=None):
            return pltpu.make_async_remote_copy(
                src_ref=slab(*block) if src is None else src, dst_ref=slab(*block),
                send_sem=send_sems.at[k], recv_sem=recv_sems.at[k], device_id=to, device_id_type=MESH)

        mine = pltpu.make_async_copy(x_ref, slab(*me), local_sem)
        mine.start()
        first = [copy(0, me, sibling, src=x_ref)]
        first += [copy(1 + j, me, (*chip, c_), src=x_ref) for j, chip in enumerate(chips)]
        for cp in first:
            cp.start()
        passed = [copy(4 + j, (*chip, c_), sibling) for j, chip in enumerate(chips)]
        for j, chip in enumerate(chips):
            copy(1 + j, (*chip, c_), me).wait_recv()
            passed[j].start()
        copy(0, sibling, me).wait_recv()
        for j, chip in enumerate(chips):
            copy(4 + j, (*chip, 1 - c_), me).wait_recv()
        for cp in first + passed:
            cp.wait_send()
        mine.wait()

    return pl.pallas_call(
        body, name=name, out_shape=jax.ShapeDtypeStruct((N_DEV, M, N), x.dtype),
        in_specs=[ANY], out_specs=ANY,
        scratch_shapes=[pltpu.SemaphoreType.DMA((7,)), pltpu.SemaphoreType.DMA((7,)), pltpu.SemaphoreType.DMA],
    )(x)


def _pair_exchange(g, *, name):
    _, R, C = g.shape

    def body(g_ref, out_ref, send_sems, recv_sems):
        x_, y_, c_ = _place()
        copies = []
        for k in range(4):
            cp = pltpu.make_async_remote_copy(
                src_ref=g_ref.at[2 * k + (1 - c_)], dst_ref=out_ref.at[k], send_sem=send_sems.at[k],
                recv_sem=recv_sems.at[k], device_id=(x_, y_, 1 - c_), device_id_type=MESH)
            cp.start()
            copies.append(cp)
        for cp in copies:
            cp.wait()

    return pl.pallas_call(
        body, name=name, out_shape=jax.ShapeDtypeStruct((4, R, C), g.dtype), in_specs=[ANY], out_specs=ANY,
        scratch_shapes=[pltpu.SemaphoreType.DMA((4,)), pltpu.SemaphoreType.DMA((4,))],
    )(g)


def _chip_exchange(p, *, name):
    _, R, C = p.shape

    def body(p_ref, out_ref, send_sems, recv_sems):
        x_, y_, c_ = _place()
        chips = [(1 - x_, y_), (x_, 1 - y_), (1 - x_, 1 - y_)]
        copies = []
        for j, (cx, cy) in enumerate(chips):
            cp = pltpu.make_async_remote_copy(
                src_ref=p_ref.at[2 * cx + cy], dst_ref=out_ref.at[j], send_sem=send_sems.at[j],
                recv_sem=recv_sems.at[j], device_id=(cx, cy, c_), device_id_type=MESH)
            cp.start()
            copies.append(cp)
        for cp in copies:
            cp.wait()

    return pl.pallas_call(
        body, name=name, out_shape=jax.ShapeDtypeStruct((3, R, C), p.dtype), in_specs=[ANY], out_specs=ANY,
        scratch_shapes=[pltpu.SemaphoreType.DMA((3,)), pltpu.SemaphoreType.DMA((3,))],
    )(p)


def _add_tr(R, C):
    return _tile(R, tuple(t for t in (512, 256, 128, 64, 32, 16, 8) if t * C * 4 <= (2 << 20)))


def _pair_add(g, recv, *, name):
    _, R, C = g.shape
    tr = _add_tr(R, C)
    core = lax.axis_index("c").astype(jnp.int32).reshape(1)

    def body(c_ref, g_ref, r_ref, o_ref):
        o_ref[...] = (g_ref[...].astype(F32) + r_ref[...].astype(F32)).astype(o_ref.dtype)

    return pl.pallas_call(
        body, name=name, out_shape=jax.ShapeDtypeStruct((4, R, C), BF16),
        grid_spec=pltpu.PrefetchScalarGridSpec(
            num_scalar_prefetch=1, grid=(4, R // tr),
            in_specs=[pl.BlockSpec((None, tr, C), lambda k, i, c: (2 * k + c[0], i, 0)),
                      pl.BlockSpec((None, tr, C), lambda k, i, c: (k, i, 0))],
            out_specs=pl.BlockSpec((None, tr, C), lambda k, i, c: (k, i, 0))),
        compiler_params=_cparams(("parallel", "parallel")),
    )(core, g, recv)


def _chip_add(p, recv, *, name):
    _, R, C = p.shape
    tr = _add_tr(R, C)
    chip = (2 * lax.axis_index("x") + lax.axis_index("y")).astype(jnp.int32).reshape(1)

    def body(c_ref, p_ref, r0_ref, r1_ref, r2_ref, o_ref):
        f = lambda r: r[...].astype(F32)
        o_ref[...] = ((f(p_ref) + f(r0_ref)) + f(r1_ref)) + f(r2_ref)

    rspec = lambda j: pl.BlockSpec((None, tr, C), lambda i, c: (j, i, 0))
    return pl.pallas_call(
        body, name=name, out_shape=jax.ShapeDtypeStruct((R, C), F32),
        grid_spec=pltpu.PrefetchScalarGridSpec(
            num_scalar_prefetch=1, grid=(R // tr,),
            in_specs=[pl.BlockSpec((None, tr, C), lambda i, c: (c[0], i, 0)), rspec(0), rspec(1), rspec(2)],
            out_specs=pl.BlockSpec((tr, C), lambda i, c: (i, 0))),
        compiler_params=_cparams(("parallel",)),
    )(chip, p, recv, recv, recv)


def _reduce_scatter(g, *, name):
    p = _pair_add(g, _pair_exchange(g, name=name + "_pair"), name=name + "_pair_add")
    return _chip_add(p, _chip_exchange(p, name=name + "_chip"), name=name + "_chip_add")


def _sum8(a, *, name):
    _, R, C = a.shape
    tr = _tile(R, tuple(t for t in (512, 256, 128, 64, 32, 16, 8) if t * C * 4 * 8 <= (4 << 20)))

    def body(a_ref, o_ref):
        acc = a_ref[0]
        for d in range(1, N_DEV):
            acc = acc + a_ref[d]
        o_ref[...] = acc

    return pl.pallas_call(
        body, name=name, grid=(R // tr,), in_specs=[pl.BlockSpec((N_DEV, tr, C), lambda i: (0, i, 0))],
        out_specs=pl.BlockSpec((tr, C), lambda i: (i, 0)), out_shape=jax.ShapeDtypeStruct((R, C), F32),
        compiler_params=_cparams(("parallel",)),
    )(a)


def _s5_params(lam_re, lam_im, log_dt, b_re, b_im, c_re, c_im):
    lr = jnp.minimum(lam_re, C_MAX_RE)
    li = lam_im
    dt = jnp.exp(log_dt)[:, None]
    mag = jnp.exp(lr * dt)
    ar, ai = mag * jnp.cos(li * dt), mag * jnp.sin(li * dt)
    nr, ni, den = ar - 1.0, ai, lr * lr + li * li
    fr, fi = (nr * lr + ni * li) / den, (ni * lr - nr * li) / den
    bbr = fr[..., None] * b_re - fi[..., None] * b_im
    bbi = fr[..., None] * b_im + fi[..., None] * b_re
    eye = jnp.eye(S5_GB, dtype=F32)

    def dense_b(bb):
        t = bb.reshape(S5_GB, S5_GB, C_STATE, C_GROUP).transpose(0, 1, 3, 2)
        return (t[:, :, :, None, :] * eye[None, :, None, :, None]).reshape(S5_GB, S5_GB * C_GROUP, S5_GBW)

    def dense_c(cc):
        t = cc.reshape(S5_GB, S5_GB, C_GROUP, C_STATE).transpose(0, 1, 3, 2)
        return (t[:, :, :, None, :] * eye[None, :, None, :, None]).reshape(S5_GB, S5_GBW, S5_GB * C_GROUP)

    flat = lambda a: a.reshape(1, S5_LANES)
    return flat(ar), flat(ai), dense_b(bbr), dense_b(bbi), dense_c(c_re), dense_c(c_im)


def _rope_tables(n_ctx, n_lat):
    quarter = HEAD_DK // 4
    freqs = ROPE_BASE ** (-jnp.arange(quarter, dtype=F32) / quarter)
    t = jnp.arange(n_lat)
    r = (t // GRID_W).astype(F32)
    col = (t % GRID_W).astype(F32)
    ang = jnp.concatenate([r[:, None] * freqs, col[:, None] * freqs], axis=-1)
    ang = jnp.concatenate([jnp.zeros((n_ctx, HEAD_DK // 2), F32), ang], axis=0)
    cos, sin = jnp.cos(ang), jnp.sin(ang)
    idx = jnp.arange(HEAD_DK)
    swap = (idx[:, None] == (idx[None, :] + HEAD_DK // 2) % HEAD_DK).astype(F32)
    return jnp.concatenate([cos, cos], axis=1), jnp.concatenate([-sin, sin], axis=1), swap


W_IN = 13344
W_INP = 13440
_COL = dict(a_q=0, a_ff=1024, a_fb=2048, a_i=3072, a_g=4096, b_q=5120, b_k=5632, b_v=6144, b_g=7168,
            c_u=8192, c_g=9216, d_q=10240, d_k=10752, d_v=11264, d_g=12288, b_lr=13312)


_COL_ORDER = (("a_q", 1024), ("a_ff", 1024), ("a_fb", 1024), ("a_i", 1024), ("a_g", 1024), ("b_q", 512), ("b_k", 512),
              ("b_v", 1024), ("b_g", 1024), ("c_u", 1024), ("c_g", 1024), ("d_q", 512), ("d_k", 512), ("d_v", 1024),
              ("d_g", 1024), ("b_lr", 128))
_W_IN_MAP = ((0, 7168, 0), (7200, W_IN, 7168), (7168, 7200, 13312))
W_IN_SHARD = W_IN // N_DEV


def _w_in_from_shards(g):
    parts = []
    for lo, hi, _ in _W_IN_MAP:
        for d in range(N_DEV):
            a, b = max(lo, d * W_IN_SHARD), min(hi, (d + 1) * W_IN_SHARD)
            if a < b:
                parts.append(g[d, ..., a - d * W_IN_SHARD:b - d * W_IN_SHARD])
    parts.append(jnp.zeros(g.shape[1:-1] + (W_INP - W_IN,), g.dtype))
    return jnp.concatenate(parts, axis=-1)


def _w_in_to_shards(w):
    shards = []
    for d in range(N_DEV):
        parts = []
        for lo, hi, here in sorted(_W_IN_MAP):
            a, b = max(lo, d * W_IN_SHARD), min(hi, (d + 1) * W_IN_SHARD)
            if a < b:
                parts.append(w[..., here + a - lo:here + b - lo])
        shards.append(jnp.concatenate(parts, axis=-1))
    return jnp.stack(shards)


@jax.custom_vjp
def _split_cols(proj):
    return tuple(lax.slice_in_dim(proj, _COL[n], _COL[n] + w, axis=1) for n, w in _COL_ORDER)


def _split_cols_fwd(proj):
    return _split_cols(proj), None


def _split_cols_bwd(_, cts):
    return (jnp.concatenate(cts, axis=1),)


_split_cols.defvjp(_split_cols_fwd, _split_cols_bwd)


def _layer_ops(l, n_ctx):
    n = f"l{l}_"
    rw = functools.partial(_rowwise, n_ctx=n_ctx)
    wide = [(BRANCH_W, F32)]
    return dict(
        modnorm=rw(_modnorm_fn, ["row", "const", "seg", "seg"], [(4096, BF16)], ncb=1, tr=128, name=n + "modnorm"),
        resid=rw(_resid_fn, ["row", "row", "seg"], [(4096, F32)], ncb=1, tr=128, name=n + "resid"),
        lin_in=_linear(n + "lin_in"), lin_in_gather=_linear_gather(n + "lin_in"),
        lin_out=_linear(n + "lin_out"), lin_glu=_linear(n + "lin_glu"),
        hgrn_pro=rw(_hgrn_pro_fn, ["row", "row", "const", "const"], wide * 4, ncb=1, tr=256, name=n + "hgrn_pro"),
        hgrn_f=_gla(n + "hgrn_f", flip=False, n_ctx=n_ctx), hgrn_b=_gla(n + "hgrn_b", flip=True, n_ctx=n_ctx),
        hgrn_epi=rw(_hgrn_epi_fn, ["row", "row", "row", "const"], [(BRANCH_W, BF16)], ncb=8, tr=256, name=n + "hgrn_epi"),
        gla_pro=rw(_gla_pro_fn, ["row", "row", "const", "const", "const", "const"], [(512, F32)] * 3, ncb=1, tr=256,
                   name=n + "gla_pro"),
        gla_f=_gla(n + "gla_f", flip=False, n_ctx=n_ctx), gla_b=_gla(n + "gla_b", flip=True, n_ctx=n_ctx),
        gla_epi=rw(_hgrn_epi_fn, ["row", "row", "row", "const"], [(BRANCH_W, BF16)], ncb=4, tr=256, name=n + "gla_epi"),
        s5_f=_s5(n + "s5_f", flip=False, n_ctx=n_ctx), s5_b=_s5(n + "s5_b", flip=True, n_ctx=n_ctx),
        s5_mid=rw(_s5_mid_fn, ["row", "row", "row", "const"], wide, ncb=1, tr=256, name=n + "s5_mid"),
        s5_epi=rw(_s5_epi_fn, ["row", "row", "row", "const"], [(BRANCH_W, BF16)], ncb=1, tr=256, name=n + "s5_epi"),
        ret_pro=rw(_ret_pro_fn, ["row", "row", "rowc", "rowc", "col", "col", "fixed"], [(512, F32)] * 4, ncb=4, tr=256,
                   name=n + "ret_pro"),
        ret_f=_gla(n + "ret_f", flip=False, n_ctx=n_ctx), ret_b=_gla(n + "ret_b", flip=True, n_ctx=n_ctx),
        ret_epi=rw(_ret_epi_fn, ["row", "row", "row", "const"], [(BRANCH_W, BF16)], ncb=4, tr=256, name=n + "ret_epi"),
    )


def _layer(ops, h, p, w_in, w_out, w_glu, lb, rope, sides=()):
    cos2, sin2, swap = rope
    (hn,) = ops["modnorm"](h, p["norm_g"], p["sc"], p["sh"])
    if not sides:
        proj, got = ops["lin_in"](hn, w_in, p["shadow_in"]), ()
    else:
        proj, *got = ops["lin_in_gather"](hn, w_in, p["shadow_in"], tuple(sides))
        got = [lax.stop_gradient(g) for g in got]
    w_out = w_out(got) if callable(w_out) else w_out
    w_glu = w_glu(got) if callable(w_glu) else w_glu
    cols = dict(zip([n for n, _ in _COL_ORDER], _split_cols(proj)))
    col = lambda name, w: cols[name]

    a_q, a_i = col("a_q", 1024), col("a_i", 1024)
    kf, gf, kb, gb = ops["hgrn_pro"](col("a_ff", 1024), col("a_fb", 1024), lb[0:1], lb[1:2])
    (o_a,) = ops["hgrn_epi"](ops["hgrn_f"](a_q, kf, a_i, gf), ops["hgrn_b"](a_q, kb, a_i, gb), col("a_g", 1024),
                             p["hgrn_norm_g"])

    rank = B_GATE_RANK
    wf = jnp.pad(p["gla_w_gk"][0], ((0, 128 - rank), (0, 0)))
    wb = jnp.pad(p["gla_w_gk"][1], ((rank, 128 - 2 * rank), (0, 0)))
    b_k, b_v = col("b_k", 512), col("b_v", 1024)
    qs, gf, gb = ops["gla_pro"](col("b_q", 512), col("b_lr", 128), wf, wb, p["gla_b_gk"][0:1], p["gla_b_gk"][1:2])
    (o_b,) = ops["gla_epi"](ops["gla_f"](qs, b_k, b_v, gf), ops["gla_b"](qs, b_k, b_v, gb), col("b_g", 1024),
                            p["gla_norm_g"])

    c_u = col("c_u", 1024)
    s5p = lambda d: _s5_params(*[p[k][d] for k in ("s5_lam_re", "s5_lam_im", "s5_log_dt", "s5_b_re", "s5_b_im",
                                                  "s5_c_re", "s5_c_im")])
    (z,) = ops["s5_mid"](ops["s5_f"](c_u, *s5p(0)), ops["s5_b"](c_u, *s5p(1)), c_u, p["s5_d"].reshape(1, BRANCH_W))
    t = ops["lin_glu"](z, w_glu, p["shadow_glu"])
    (o_c,) = ops["s5_epi"](z, t, col("c_g", 1024), p["s5_b_glu"])

    log_gamma = _log_sigmoid(p["ret_decay_logit"])
    lg = lambda d: jnp.broadcast_to(log_gamma[d][:, None, None], (4, 1, HEAD_DK))
    d_v = col("d_v", 1024)
    qs, ks, gf, gb = ops["ret_pro"](col("d_q", 512), col("d_k", 512), cos2, sin2, lg(0), lg(1), swap)
    (o_d,) = ops["ret_epi"](ops["ret_f"](qs, ks, d_v, gf), ops["ret_b"](qs, ks, d_v, gb), col("d_g", 1024),
                            p["ret_norm_g"])

    o = jnp.concatenate([o_a, o_b, o_c, o_d], axis=1)
    y = ops["lin_out"](o, w_out, p["shadow_out"])
    (h,) = ops["resid"](h, y, p["gt"])
    return h, got


_SMALL = ("norm_g", "hgrn_lb_logits", "hgrn_norm_g", "gla_w_gk", "gla_b_gk", "gla_norm_g", "s5_lam_re", "s5_lam_im",
          "s5_log_dt", "s5_b_re", "s5_b_im", "s5_c_re", "s5_c_im", "s5_d", "s5_b_glu", "ret_decay_logit", "ret_norm_g",
          "final_norm_g")
_WEIGHTS = ("c_ctx", "norm_g", "w_ada", "b_ada", "w_in", "hgrn_lb_logits", "hgrn_norm_g", "gla_w_gk", "gla_b_gk",
            "gla_norm_g", "s5_lam_re", "s5_lam_im", "s5_log_dt", "s5_b_re", "s5_b_im", "s5_c_re", "s5_c_im", "s5_d",
            "s5_w_glu", "s5_b_glu", "ret_decay_logit", "ret_norm_g", "w_out", "final_norm_g")


def _pad_rows(flat, mult=8):
    rows = -(-flat.size // 128)
    rows = -(-rows // mult) * mult
    return jnp.pad(flat, (0, rows * 128 - flat.size)).reshape(rows, 128)


def _unshard_cols(g, inner):
    k = len(inner)
    a = g.reshape((N_DEV,) + tuple(inner))
    return jnp.moveaxis(a, 0, k - 1).reshape(tuple(inner[:-1]) + (N_DEV * inner[-1],))


def kernel(x, c, ctx, c_ctx, norm_g, w_ada, b_ada, w_in, hgrn_lb_logits, hgrn_norm_g, gla_w_gk, gla_b_gk, gla_norm_g, s5_lam_re, s5_lam_im, s5_log_dt, s5_b_re, s5_b_im, s5_c_re, s5_c_im, s5_d, s5_w_glu, s5_b_glu, ret_decay_logit, ret_norm_g, w_out, final_norm_g, loss_target, m_c_ctx, m_norm_g, m_w_ada, m_b_ada, m_w_in, m_hgrn_lb_logits, m_hgrn_norm_g, m_gla_w_gk, m_gla_b_gk, m_gla_norm_g, m_s5_lam_re, m_s5_lam_im, m_s5_log_dt, m_s5_b_re, m_s5_b_im, m_s5_c_re, m_s5_c_im, m_s5_d, m_s5_w_glu, m_s5_b_glu, m_ret_decay_logit, m_ret_norm_g, m_w_out, m_final_norm_g, v_c_ctx, v_norm_g, v_w_ada, v_b_ada, v_w_in, v_hgrn_lb_logits, v_hgrn_norm_g, v_gla_w_gk, v_gla_b_gk, v_gla_norm_g, v_s5_lam_re, v_s5_lam_im, v_s5_log_dt, v_s5_b_re, v_s5_b_im, v_s5_c_re, v_s5_c_im, v_s5_d, v_s5_w_glu, v_s5_b_glu, v_ret_decay_logit, v_ret_norm_g, v_w_out, v_final_norm_g):
    given = dict(locals())
    depth = norm_g.shape[0]
    n_ctx, n_lat, d_model = ctx.shape[1], x.shape[1], x.shape[2]
    me = 4 * lax.axis_index("x") + 2 * lax.axis_index("y") + lax.axis_index("c")
    w_ada_cols = w_ada.shape[2]

    pack = jnp.concatenate([c.reshape(-1), hgrn_lb_logits.reshape(-1), gla_w_gk.reshape(-1), gla_b_gk.reshape(-1)])
    got = _all_gather(_pad_rows(pack), name="gather_small").reshape(N_DEV, -1)
    o0 = d_model
    o1 = o0 + hgrn_lb_logits.size
    o2 = o1 + gla_w_gk.size
    o3 = o2 + gla_b_gk.size
    c_all = got[:, :o0]
    lb_logits = _unshard_cols(got[:, o0:o1], hgrn_lb_logits.shape)
    w_gk = _unshard_cols(got[:, o1:o2], gla_w_gk.shape)
    b_gk = _unshard_cols(got[:, o2:o3], gla_b_gk.shape)

    cond = jnp.concatenate([c_all, c_ctx[None, :], jnp.zeros((7, d_model), F32)], axis=0)
    act = _silu(cond).astype(BF16)
    mod_part = jnp.concatenate([_mm(act, w_ada[l], name=f"ada{l}_fwd") for l in range(depth)], axis=0)
    mod = _all_gather(mod_part, name="gather_mod").reshape(N_DEV, depth, 16, w_ada_cols)
    mod = jnp.moveaxis(mod, 0, 2).reshape(depth, 16, N_DEV * w_ada_cols) + b_ada[:, None, :]
    mod_lat = lax.dynamic_index_in_dim(mod, me, axis=1, keepdims=False)
    mod_ctx = mod[:, 8]
    seg = lambda l, j: jnp.stack([mod_ctx[l, j * d_model:(j + 1) * d_model],
                                  mod_lat[l, j * d_model:(j + 1) * d_model]])[:, None, :]

    w_in_shards = [w_in[l].astype(BF16) for l in range(depth)]
    w_out_shards = [w_out[l].astype(BF16) for l in range(depth)]
    w_glu_shard = s5_w_glu.astype(BF16).reshape(-1, BRANCH_W)
    w_in_first = _w_in_from_shards(_all_gather(w_in_shards[0], name="gather_w_in"))

    rope = _rope_tables(n_ctx, n_lat)
    ops = [_layer_ops(l, n_ctx) for l in range(depth)]
    params = dict(
        layers=[dict(
            norm_g=norm_g[l][None, :], sh=seg(l, 0), sc=seg(l, 1), gt=seg(l, 2),
            shadow_in=jnp.zeros((d_model, W_INP), F32), shadow_out=jnp.zeros((4 * BRANCH_W, d_model), F32),
            shadow_glu=jnp.zeros((BRANCH_W, BRANCH_W), F32),
            hgrn_norm_g=hgrn_norm_g[l][None, :], gla_w_gk=w_gk[l], gla_b_gk=b_gk[l], gla_norm_g=gla_norm_g[l][None, :],
            s5_lam_re=s5_lam_re[l], s5_lam_im=s5_lam_im[l], s5_log_dt=s5_log_dt[l], s5_b_re=s5_b_re[l],
            s5_b_im=s5_b_im[l], s5_c_re=s5_c_re[l], s5_c_im=s5_c_im[l], s5_d=s5_d[l], s5_b_glu=s5_b_glu[l][None, :],
            ret_decay_logit=ret_decay_logit[l], ret_norm_g=ret_norm_g[l][None, :]) for l in range(depth)],
        lb_logits=lb_logits,
        h0=jnp.concatenate([ctx[0], x[0]], axis=0))

    def body(p):
        lb_p = jax.nn.softmax(p["lb_logits"], axis=0)
        lower = jnp.cumsum(lb_p, axis=0) - lb_p[0:1]
        h = p["h0"]
        w_in_l, w_glu_all = w_in_first, None
        glu_of = lambda g: jnp.moveaxis(g.reshape(N_DEV, depth, -1, BRANCH_W), 0, 1).reshape(depth, BRANCH_W, BRANCH_W)
        for l in range(depth):
            more = l + 1 < depth
            sides = ([w_in_shards[l + 1]] if more else []) + [w_out_shards[l]] + ([w_glu_shard] if l == 0 else [])
            k_out = 1 if more else 0
            pick_out = lambda got, k=k_out: got[k].reshape(4 * BRANCH_W, d_model)
            pick_glu = (lambda got, k=k_out + 1: glu_of(got[k])[0]) if l == 0 else w_glu_all[l]
            h, got = _layer(ops[l], h, p["layers"][l], w_in_l, pick_out, pick_glu, lower[l], rope, sides=sides)
            if l == 0:
                w_glu_all = glu_of(got[k_out + 1])
            if more:
                w_in_l = _w_in_from_shards(got[0])
        return h

    h_out, pullback = jax.vjp(body, params)
    dh, dg_final, loss_part = _loss_call(h_out, loss_target[0], final_norm_g[None, :], n_ctx=n_ctx, name="loss_head")
    (grads,) = pullback(dh)
    loss = lax.psum(loss_part[0, 0], ("x", "y", "c"))
    gl = grads["layers"]

    dmod = jnp.stack([jnp.concatenate([gl[l][k][s, 0] for k in ("sh", "sc", "gt")])
                      for l in range(depth) for s in (1, 0)])
    dmod = jnp.pad(dmod, ((0, 8 - 2 * depth), (0, 0)))
    dmod_all = _all_gather(dmod, name="gather_dmod")
    dmod_sum = _sum8(dmod_all, name="sum_dmod")
    grad_b_ada = jnp.stack([dmod_sum[2 * l] + dmod_sum[2 * l + 1] for l in range(depth)])
    grad_w_ada, ds_ctx = [], jnp.zeros((d_model,), F32)
    for l in range(depth):
        rows = jnp.concatenate([dmod_all[:, 2 * l], dmod_sum[2 * l + 1][None, :], jnp.zeros((7, 3 * d_model), F32)])
        mine = lax.dynamic_slice_in_dim(rows, me * w_ada_cols, w_ada_cols, axis=1)
        grad_w_ada.append(_mm(act.T, mine, name=f"ada{l}_dw"))
        ds_ctx = ds_ctx + _mm(mine, w_ada[l], nt=True, name=f"ada{l}_da")[8]
    grad_w_ada = jnp.stack(grad_w_ada)

    def scattered(g, axis, name):
        parts = g.reshape(g.shape[:axis] + (N_DEV, g.shape[axis] // N_DEV) + g.shape[axis + 1:])
        parts = jnp.moveaxis(parts, axis, 0)
        local = parts.shape[1:]
        return _reduce_scatter(parts.reshape(N_DEV, -1, local[-1]).astype(BF16), name=name).reshape(local)

    w_in_slabs = _w_in_to_shards(jnp.stack([gl[l]["shadow_in"] for l in range(depth)]).astype(BF16))
    grad_w_in = _reduce_scatter(w_in_slabs.reshape(N_DEV, depth * d_model, W_IN_SHARD), name="rs_w_in").reshape(w_in.shape)
    grad_w_out = scattered(jnp.stack([gl[l]["shadow_out"] for l in range(depth)]), 1, "rs_w_out")
    grad_w_glu = scattered(jnp.stack([gl[l]["shadow_glu"] for l in range(depth)]), 1, "rs_w_glu")

    stack = lambda k, shape=None: jnp.stack([gl[l][k] for l in range(depth)]).reshape(shape or given[k].shape)
    small = dict(norm_g=stack("norm_g"), hgrn_lb_logits=grads["lb_logits"], hgrn_norm_g=stack("hgrn_norm_g"),
                 gla_w_gk=stack("gla_w_gk", w_gk.shape), gla_b_gk=stack("gla_b_gk", b_gk.shape),
                 gla_norm_g=stack("gla_norm_g"), s5_d=stack("s5_d"), s5_b_glu=stack("s5_b_glu"),
                 ret_decay_logit=stack("ret_decay_logit"), ret_norm_g=stack("ret_norm_g"), final_norm_g=dg_final[0])
    for k in ("s5_lam_re", "s5_lam_im", "s5_log_dt", "s5_b_re", "s5_b_im", "s5_c_re", "s5_c_im"):
        small[k] = stack(k)
    pieces = [small[k] for k in _SMALL] + [ds_ctx]
    packed = jnp.concatenate([_pad_rows(a.reshape(-1), mult=1) for a in pieces], axis=0)
    packed = jnp.pad(packed, ((0, -packed.shape[0] % 8), (0, 0)))
    total = _sum8(_all_gather(packed, name="gather_small_grads"), name="sum_small_grads")
    reduced, off = {}, 0
    for k, a in zip(_SMALL + ("ds_ctx",), pieces):
        rows = -(-a.size // 128)
        reduced[k] = total[off:off + rows].reshape(-1)[:a.size].reshape(a.shape)
        off += rows

    def my_cols(a):
        w = a.shape[-1] // N_DEV
        return lax.dynamic_slice_in_dim(a, me * w, w, axis=a.ndim - 1)

    dsilu = jax.vjp(_silu, c_ctx)[1](reduced.pop("ds_ctx"))[0]
    grad = dict(reduced, c_ctx=dsilu, w_ada=grad_w_ada, b_ada=grad_b_ada, w_in=grad_w_in, w_out=grad_w_out,
                s5_w_glu=grad_w_glu)
    for k in ("hgrn_lb_logits", "gla_w_gk", "gla_b_gk"):
        grad[k] = my_cols(grad[k])

    delta, new_m, new_v = {}, {}, {}
    for k in _WEIGHTS:
        delta[k], new_m[k], new_v[k] = _adamw(given[k], grad[k], given["m_" + k], given["v_" + k], name="adamw_" + k)
    grad_x = grads["h0"][n_ctx:][None]
    return (loss, grad_x, *[grad[k] for k in _WEIGHTS], *[delta[k] for k in _WEIGHTS],
            *[new_m[k] for k in _WEIGHTS], *[new_v[k] for k in _WEIGHTS])
```
